```python
import jax, jax.numpy as jnp
from jax import lax
import numpy as np

D_MODEL = 1024
BATCH = 2
SEQ = 8192
DEPTH = 1
DEC_BATCH = 128
DEC_SEQ = 1
PAST_LEN = 16384
PAGE_SIZE = 128

D_MIX = D_MODEL
D_ATT = D_MIX // 2
D_POOL = D_MIX - D_ATT
N_HEADS = 8
V_DIM = D_ATT // N_HEADS
NOPE_DIM = 64
ROPE_DIM = 32
QK_DIM = NOPE_DIM + ROPE_DIM
Q_LORA = 256
KV_LORA = 128
ROPE_BASE = 10000.0
POOL_WINDOWS = (2, 4, 8, 16)
N_POOL_GROUPS = len(POOL_WINDOWS)
POOL_GROUP = D_POOL // N_POOL_GROUPS
POOL_PREV = max(POOL_WINDOWS) - 1
Q_BLOCK = 128
EPS = 1e-6
SM_SCALE = QK_DIM ** -0.5
OFF_Q = 0
OFF_KV = OFF_Q + Q_LORA
OFF_KR = OFF_KV + KV_LORA
OFF_GA = OFF_KR + ROPE_DIM
OFF_U = OFF_GA + D_ATT
OFF_GP = OFF_U + D_POOL
D_IN = OFF_GP + D_POOL

kernel_name = "hymba_mla_pool_decode_step"


def rmsnorm(x, w):
    xf = x.astype(jnp.float32)
    y = xf * lax.rsqrt(jnp.mean(xf * xf, axis=-1, keepdims=True) + EPS)
    return (y * w.astype(jnp.float32)).astype(x.dtype)


def rope(x, pos):
    r = x.shape[-1]
    inv = ROPE_BASE ** (-jnp.arange(0, r, 2, dtype=jnp.float32) / r)
    ang = pos.astype(jnp.float32)[:, None] * inv[None, :]
    shp = (ang.shape[0],) + (1,) * (x.ndim - 3) + (r // 2,)
    cos = jnp.cos(ang).reshape(shp)
    sin = jnp.sin(ang).reshape(shp)
    xf = x.astype(jnp.float32)
    x1, x2 = xf[..., : r // 2], xf[..., r // 2:]
    return jnp.concatenate([x1 * cos - x2 * sin, x2 * cos + x1 * sin], axis=-1).astype(x.dtype)


def branch_inputs(x, c, pos, ada_w, ada_b, norm_w, w_in, q_norm_w, w_uq, kv_norm_w):
    mod = jnp.einsum('bd,de->be', jax.nn.silu(c), ada_w) + ada_b
    shift, scale, gate = jnp.split(mod, 3, axis=-1)
    h = rmsnorm(x, norm_w) * (1 + scale[:, None, :]) + shift[:, None, :]
    z = jnp.einsum('btd,de->bte', h, w_in)
    q_lat = rmsnorm(z[..., OFF_Q:OFF_KV], q_norm_w)
    ckv = rmsnorm(z[..., OFF_KV:OFF_KR], kv_norm_w)
    kr = rope(z[..., OFF_KR:OFF_GA], pos)
    g_att = z[..., OFF_GA:OFF_U]
    u = z[..., OFF_U:OFF_GP]
    g_pool = z[..., OFF_GP:]
    q = jnp.einsum('btq,qhd->bthd', q_lat, w_uq)
    q_nope = q[..., :NOPE_DIM]
    q_rope = rope(q[..., NOPE_DIM:], pos)
    return gate, q_nope, q_rope, ckv, kr, g_att, u, g_pool


def attn_prompt(q_nope, q_rope, ckv, kr, w_uk, w_uv):
    b, s = q_nope.shape[:2]
    k_nope = jnp.einsum('bsk,khd->bshd', ckv, w_uk)
    v = jnp.einsum('bsk,khv->bshv', ckv, w_uv)
    nb = s // Q_BLOCK
    qn = q_nope.reshape(b, nb, Q_BLOCK, N_HEADS, NOPE_DIM).transpose(1, 0, 2, 3, 4)
    qr = q_rope.reshape(b, nb, Q_BLOCK, N_HEADS, ROPE_DIM).transpose(1, 0, 2, 3, 4)
    kpos = jnp.arange(s)

    def block(args):
        qn_b, qr_b, i = args
        sc = (jnp.einsum('bqhd,bshd->bhqs', qn_b, k_nope)
              + jnp.einsum('bqhr,bsr->bhqs', qr_b, kr)).astype(jnp.float32) * SM_SCALE
        qpos = i * Q_BLOCK + jnp.arange(Q_BLOCK)
        sc = jnp.where(kpos[None, :] <= qpos[:, None], sc, -jnp.inf)
        p = jax.nn.softmax(sc, axis=-1).astype(v.dtype)
        return jnp.einsum('bhqs,bshv->bqhv', p, v)

    o = lax.map(block, (qn, qr, jnp.arange(nb)))
    return o.transpose(1, 0, 2, 3, 4).reshape(b, s, N_HEADS * V_DIM)


def attn_sample(q_nope, q_rope, ckv, kr, cache_ckv_l, cache_kr_l, page_table, w_uk, w_uv):
    db, t = q_nope.shape[:2]
    past_ckv = cache_ckv_l[page_table].reshape(db, -1, KV_LORA)
    past_kr = cache_kr_l[page_table].reshape(db, -1, ROPE_DIM)
    n_past = past_ckv.shape[1]
    q_abs = jnp.einsum('bthd,khd->bthk', q_nope, w_uk)
    s_past = jnp.einsum('bthk,bsk->bhts', q_abs, past_ckv) + jnp.einsum('bthr,bsr->bhts', q_rope, past_kr)
    s_new = jnp.einsum('bthk,bsk->bhts', q_abs, ckv) + jnp.einsum('bthr,bsr->bhts', q_rope, kr)
    s_new = jnp.where(jnp.tril(jnp.ones((t, t), dtype=bool)), s_new, -jnp.inf)
    sc = jnp.concatenate([s_past, s_new], axis=-1).astype(jnp.float32) * SM_SCALE
    p = jax.nn.softmax(sc, axis=-1).astype(ckv.dtype)
    o_lat = (jnp.einsum('bhts,bsk->bthk', p[..., :n_past], past_ckv)
             + jnp.einsum('bhts,bsk->bthk', p[..., n_past:], ckv))
    o = jnp.einsum('bthk,khv->bthv', o_lat, w_uv)
    return o.reshape(db, t, N_HEADS * V_DIM)


def pool_branch(u, prev, pos0, w_pool, b_pool, pool_scale):
    b, t, _ = u.shape
    u_ext = jnp.concatenate([prev, u], axis=1)
    cs = jnp.cumsum(u_ext.astype(jnp.float32), axis=1)
    cs0 = jnp.concatenate([jnp.zeros((b, 1, D_POOL), jnp.float32), cs], axis=1)
    pos = (pos0 + jnp.arange(t)).astype(jnp.float32)
    outs = []
    for g, w in enumerate(POOL_WINDOWS):
        lo, hi = g * POOL_GROUP, (g + 1) * POOL_GROUP
        wsum = (cs0[:, POOL_PREV + 1:POOL_PREV + 1 + t, lo:hi]
                - cs0[:, POOL_PREV + 1 - w:POOL_PREV + 1 - w + t, lo:hi])
        cnt = jnp.minimum(pos + 1.0, float(w))[None, :, None]
        outs.append(wsum / cnt - u[..., lo:hi].astype(jnp.float32))
    pooled = jnp.stack(outs, axis=2).astype(u.dtype)
    y = jnp.einsum('btgc,gcd->btgd', pooled, w_pool) + b_pool
    y = y.reshape(b, t, D_POOL) * pool_scale
    return y, u_ext[:, -POOL_PREV:]


def combine(x, gate, att, g_att, pool, g_pool, w_out):
    mixed = jnp.concatenate([att * jax.nn.silu(g_att), pool * jax.nn.silu(g_pool)], axis=-1)
    return x + gate[:, None, :] * jnp.einsum('bte,ed->btd', mixed, w_out)


def setup_inputs(seed: int = 0) -> dict:
    key = jax.random.key(seed)
    ks = jax.random.split(key, 32)
    n_pages = PAST_LEN // PAGE_SIZE
    n_pool_pages = (DEC_BATCH * n_pages * 5) // 4
    f32 = jnp.float32
    nrm = lambda k, shp, s=1.0: jax.random.normal(k, shp, f32) * s
    perm = jax.random.permutation(ks[7], n_pool_pages)[: DEC_BATCH * n_pages]
    return {
        "x_prompt": nrm(ks[0], (BATCH, SEQ, D_MODEL)),
        "x_sample": nrm(ks[1], (DEC_BATCH, DEC_SEQ, D_MODEL)),
        "c_prompt": nrm(ks[2], (BATCH, D_MODEL)),
        "c_sample": nrm(ks[3], (DEC_BATCH, D_MODEL)),
        "cache_ckv": nrm(ks[4], (DEPTH, n_pool_pages, PAGE_SIZE, KV_LORA)),
        "cache_krope": nrm(ks[5], (DEPTH, n_pool_pages, PAGE_SIZE, ROPE_DIM)),
        "state_pool": nrm(ks[6], (DEPTH, DEC_BATCH, POOL_PREV, D_POOL)),
        "page_table": perm.reshape(DEC_BATCH, n_pages).astype(jnp.int32),
        "ada_w": nrm(ks[8], (DEPTH, D_MODEL, 3 * D_MODEL), D_MODEL ** -0.5),
        "ada_b": nrm(ks[9], (DEPTH, 3 * D_MODEL), 0.02),
        "norm_w": 1.0 + nrm(ks[10], (DEPTH, D_MODEL), 0.02),
        "w_in": nrm(ks[11], (DEPTH, D_MODEL, D_IN), D_MODEL ** -0.5),
        "q_norm_w": 1.0 + nrm(ks[12], (DEPTH, Q_LORA), 0.02),
        "w_uq": nrm(ks[13], (DEPTH, Q_LORA, N_HEADS, QK_DIM), Q_LORA ** -0.5),
        "kv_norm_w": 1.0 + nrm(ks[14], (DEPTH, KV_LORA), 0.02),
        "w_uk": nrm(ks[15], (DEPTH, KV_LORA, N_HEADS, NOPE_DIM), KV_LORA ** -0.5),
        "w_uv": nrm(ks[16], (DEPTH, KV_LORA, N_HEADS, V_DIM), KV_LORA ** -0.5),
        "w_pool": nrm(ks[17], (DEPTH, N_POOL_GROUPS, POOL_GROUP, POOL_GROUP), POOL_GROUP ** -0.5),
        "b_pool": nrm(ks[18], (DEPTH, N_POOL_GROUPS, POOL_GROUP), 0.01),
        "pool_scale": 1.0 + nrm(ks[19], (DEPTH, D_POOL), 0.05),
        "w_out": nrm(ks[20], (DEPTH, D_MIX, D_MODEL), D_MIX ** -0.5),
        "final_norm_w": 1.0 + nrm(ks[21], (D_MODEL,), 0.02),
    }


def reference(x_prompt, x_sample, c_prompt, c_sample, cache_ckv, cache_krope, state_pool, page_table,
              ada_w, ada_b, norm_w, w_in, q_norm_w, w_uq, kv_norm_w, w_uk, w_uv,
              w_pool, b_pool, pool_scale, w_out, final_norm_w):
    pos_p = jnp.arange(x_prompt.shape[1])
    pos_s = PAST_LEN + jnp.arange(x_sample.shape[1])
    xp, xs = x_prompt, x_sample
    ckv_p_l, kr_p_l, pool_p_l, ckv_s_l, kr_s_l, pool_s_l = [], [], [], [], [], []
    for l in range(DEPTH):
        gate, qn, qr, ckv, kr, g_att, u, g_pool = branch_inputs(
            xp, c_prompt, pos_p, ada_w[l], ada_b[l], norm_w[l], w_in[l], q_norm_w[l], w_uq[l], kv_norm_w[l])
        att = attn_prompt(qn, qr, ckv, kr, w_uk[l], w_uv[l])
        prev0 = jnp.zeros((xp.shape[0], POOL_PREV, D_POOL), u.dtype)
        pool, new_prev = pool_branch(u, prev0, 0, w_pool[l], b_pool[l], pool_scale[l])
        xp = combine(xp, gate, att, g_att, pool, g_pool, w_out[l])
        ckv_p_l.append(ckv); kr_p_l.append(kr); pool_p_l.append(new_prev)
        gate, qn, qr, ckv, kr, g_att, u, g_pool = branch_inputs(
            xs, c_sample, pos_s, ada_w[l], ada_b[l], norm_w[l], w_in[l], q_norm_w[l], w_uq[l], kv_norm_w[l])
        att = attn_sample(qn, qr, ckv, kr, cache_ckv[l], cache_krope[l], page_table, w_uk[l], w_uv[l])
        pool, new_prev = pool_branch(u, state_pool[l], PAST_LEN, w_pool[l], b_pool[l], pool_scale[l])
        xs = combine(xs, gate, att, g_att, pool, g_pool, w_out[l])
        ckv_s_l.append(ckv); kr_s_l.append(kr); pool_s_l.append(new_prev)
    y_prompt = rmsnorm(xp, final_norm_w)
    y_sample = rmsnorm(xs, final_norm_w)
    return (y_prompt, y_sample,
            jnp.stack(ckv_p_l), jnp.stack(kr_p_l), jnp.stack(pool_p_l),
            jnp.stack(ckv_s_l), jnp.stack(kr_s_l), jnp.stack(pool_s_l))
```

```python
import functools
import math

import jax
import jax.numpy as jnp
from jax import lax
from jax.experimental import pallas as pl
from jax.experimental.pallas import tpu as pltpu

F32 = jnp.float32
BF16 = jnp.bfloat16

N_HEADS = 8
NOPE_DIM = 64
ROPE_DIM = 32
QK_DIM = NOPE_DIM + ROPE_DIM
V_DIM = 64
Q_LORA = 256
KV_LORA = 128
D_POOL = 512
D_ATT = 512
POOL_WINDOWS = (2, 4, 8, 16)
POOL_GROUP = 128
POOL_PREV = 15
PAGE_SIZE = 128
ROPE_BASE = 10000.0
EPS = 1e-6
SM_SCALE = QK_DIM ** -0.5
LOG2E = math.log2(math.e)

HEAD_PAD = 128
HALO = 16
C_Q, C_KV, C_KR, C_GA, C_U, C_GP, C_END = 0, 256, 384, 512, 1024, 1536, 2048

VMEM_LIMIT = 56 * 1024 * 1024


def _rms(x, w):
    return (x * lax.rsqrt(jnp.mean(x * x, axis=-1, keepdims=True) + EPS)) * w


def _silu(x):
    return x * jax.nn.sigmoid(x)


def _bdot(a, b):
    return jnp.dot(a.astype(BF16), b, preferred_element_type=F32)


def _ada_kernel(c_ref, w_ref, b_ref, o_ref):
    o_ref[...] = _bdot(_silu(c_ref[...]), w_ref[...]) + b_ref[...]


def _ada(c_all, ada_w, ada_b):
    rows, d = c_all.shape
    n = ada_w.shape[1]
    bn = 1024
    return pl.pallas_call(
        _ada_kernel,
        grid=(n // bn,),
        in_specs=[pl.BlockSpec((rows, d), lambda j: (0, 0)),
                  pl.BlockSpec((d, bn), lambda j: (0, j)),
                  pl.BlockSpec((1, bn), lambda j: (0, j))],
        out_specs=pl.BlockSpec((rows, bn), lambda j: (0, j)),
        out_shape=jax.ShapeDtypeStruct((rows, n), F32),
        compiler_params=pltpu.CompilerParams(dimension_semantics=("arbitrary",)),
        name="ada",
    )(c_all, ada_w, ada_b)


def _project(x, shift, scale, nw, win, qnw, kvnw, wq, wqr, wk, wv, cosk, sink, cosq, sinq):
    h = _rms(x, nw) * (1.0 + scale) + shift
    z = _bdot(h, win)
    q_lat = _rms(z[:, C_Q:C_KV], qnw)
    ckv = _rms(z[:, C_KV:C_KR], kvnw)
    krc = z[:, C_KR:C_GA]
    kr128 = krc * cosk + pltpu.roll(krc, HEAD_PAD - ROPE_DIM, axis=1) * sink
    ql = q_lat.astype(BF16)
    qa = jnp.dot(ql, wq, preferred_element_type=F32)
    qb = jnp.dot(ql, wqr, preferred_element_type=F32)
    ckv_b = ckv.astype(BF16)
    k = jnp.dot(jnp.concatenate([ckv_b, kr128.astype(BF16)], axis=1), wk, preferred_element_type=F32)
    v = jnp.dot(ckv_b, wv, preferred_element_type=F32)
    return qa, qb, k, v, ckv, kr128, z[:, C_GA:C_U], z[:, C_U:C_GP], z[:, C_GP:C_END]


def _pool_out(pooled_g, g, wpool_ref, bpool_ref, pscale_ref, g_pool):
    lo, hi = g * POOL_GROUP, (g + 1) * POOL_GROUP
    y = _bdot(pooled_g, wpool_ref[g]) + bpool_ref[:, lo:hi]
    y = y * pscale_ref[:, lo:hi]
    return y * _silu(g_pool[:, lo:hi])


def _inproj_p_kernel(x_ref, mod_ref, nw_ref, win_ref, qnw_ref, kvnw_ref, wq_ref, wqr_ref, wk_ref, wv_ref,
                     wpool_ref, bpool_ref, pscale_ref, cosk_ref, sink_ref, cosq_ref, sinq_ref,
                     q_ref, k_ref, v_ref, ckv_ref, kr_ref, sg_ref, mp_ref, utail_ref, uext_ref, *, tm):
    i = pl.program_id(1)
    shift = mod_ref[0, 0:1, :]
    scale = mod_ref[0, 1:2, :]
    cosq, sinq = cosq_ref[...], sinq_ref[...]
    qa, qb, k, v, ckv, kr128, g_att, u, g_pool = _project(
        x_ref[0], shift, scale, nw_ref[...], win_ref[...], qnw_ref[...], kvnw_ref[...],
        wq_ref[...], wqr_ref[...], wk_ref[...], wv_ref[...], cosk_ref[...], sink_ref[...], cosq, sinq)
    for h in range(N_HEADS):
        sl = slice(h * HEAD_PAD, (h + 1) * HEAD_PAD)
        q_ref[0, :, sl] = (qa[:, sl] * cosq + qb[:, sl] * sinq).astype(BF16)
    k_ref[0] = k.astype(BF16)
    v_ref[0] = v.astype(BF16)
    ckv_ref[0] = ckv
    kr_ref[0] = kr128[:, :ROPE_DIM]
    sg_ref[0] = _silu(g_att)

    @pl.when(i == 0)
    def _():
        uext_ref[0:HALO, :] = jnp.zeros((HALO, D_POOL), F32)

    uext_ref[HALO:HALO + tm, :] = u
    pos = (i * tm + lax.broadcasted_iota(jnp.int32, (tm, 1), 0)).astype(F32)
    for g, w in enumerate(POOL_WINDOWS):
        lo, hi = g * POOL_GROUP, (g + 1) * POOL_GROUP
        wsum = uext_ref[HALO:HALO + tm, lo:hi]
        for d in range(1, w):
            wsum = wsum + uext_ref[HALO - d:HALO - d + tm, lo:hi]
        cnt = jnp.minimum(pos + 1.0, float(w))
        pooled = wsum / cnt - u[:, lo:hi]
        mp_ref[0, :, lo:hi] = _pool_out(pooled, g, wpool_ref, bpool_ref, pscale_ref, g_pool).astype(BF16)
    uext_ref[0:HALO, :] = u[tm - HALO:tm, :]

    @pl.when(i == pl.num_programs(1) - 1)
    def _():
        utail_ref[0] = u[tm - HALO:tm, :]


def _full(shape):
    nd = len(shape)
    return pl.BlockSpec(shape, lambda *_: (0,) * nd)


def _inproj_p(x, mod3, wts, tabs, tm=512):
    b, s, d = x.shape
    nt = s // tm
    row = lambda w: pl.BlockSpec((1, tm, w), lambda bi, i: (bi, i, 0))
    tab = pl.BlockSpec((tm, HEAD_PAD), lambda bi, i: (i, 0))
    in_specs = [row(d), pl.BlockSpec((1, 3, d), lambda bi, i: (bi, 0, 0))]
    in_specs += [_full(w.shape) for w in wts]
    in_specs += [tab, tab, tab, tab]
    out_shape = [
        jax.ShapeDtypeStruct((b, s, N_HEADS * HEAD_PAD), BF16),
        jax.ShapeDtypeStruct((b, s, N_HEADS * HEAD_PAD), BF16),
        jax.ShapeDtypeStruct((b, s, D_ATT), BF16),
        jax.ShapeDtypeStruct((b, s, KV_LORA), F32),
        jax.ShapeDtypeStruct((b, s, ROPE_DIM), F32),
        jax.ShapeDtypeStruct((b, s, D_ATT), F32),
        jax.ShapeDtypeStruct((b, s, D_POOL), BF16),
        jax.ShapeDtypeStruct((b, HALO, D_POOL), F32),
    ]
    out_specs = [row(N_HEADS * HEAD_PAD), row(N_HEADS * HEAD_PAD), row(D_ATT), row(KV_LORA), row(ROPE_DIM),
                 row(D_ATT), row(D_POOL), pl.BlockSpec((1, HALO, D_POOL), lambda bi, i: (bi, 0, 0))]
    return pl.pallas_call(
        functools.partial(_inproj_p_kernel, tm=tm),
        grid=(b, nt),
        in_specs=in_specs,
        out_specs=out_specs,
        out_shape=out_shape,
        scratch_shapes=[pltpu.VMEM((tm + HALO, D_POOL), F32)],
        compiler_params=pltpu.CompilerParams(dimension_semantics=("arbitrary", "arbitrary"),
                                             vmem_limit_bytes=VMEM_LIMIT),
        name="inproj_p",
    )(x, mod3, *wts, *tabs)


def _inproj_s_kernel(x_ref, mod_ref, nw_ref, win_ref, qnw_ref, kvnw_ref, wq_ref, wqr_ref, wk_ref, wv_ref,
                     wpool_ref, bpool_ref, pscale_ref, cosk_ref, sink_ref, cosq_ref, sinq_ref,
                     state_ref, wukt_ref,
                     qabs_ref, qrope_ref, ckv_ref, kr_ref, sg_ref, mp_ref, u_ref):
    d = x_ref.shape[1]
    shift = mod_ref[:, 0:d]
    scale = mod_ref[:, d:2 * d]
    cosq, sinq = cosq_ref[...], sinq_ref[...]
    qa, qb, _, _, ckv, kr128, g_att, u, g_pool = _project(
        x_ref[...], shift, scale, nw_ref[...], win_ref[...], qnw_ref[...], kvnw_ref[...],
        wq_ref[...], wqr_ref[...], wk_ref[...], wv_ref[...], cosk_ref[...], sink_ref[...], cosq, sinq)
    for h in range(N_HEADS):
        sl = slice(h * HEAD_PAD, (h + 1) * HEAD_PAD)
        qrope_ref[:, sl] = qa[:, sl] * cosq + qb[:, sl] * sinq
        qabs_ref[:, sl] = _bdot(qa[:, sl], wukt_ref[h])
    ckv_ref[...] = ckv
    kr_ref[...] = kr128[:, :ROPE_DIM]
    sg_ref[...] = _silu(g_att)
    u_ref[...] = u
    for g, w in enumerate(POOL_WINDOWS):
        lo, hi = g * POOL_GROUP, (g + 1) * POOL_GROUP
        wsum = u[:, lo:hi]
        for j in range(1, w):
            wsum = wsum + state_ref[:, POOL_PREV - j, lo:hi]
        pooled = wsum / float(w) - u[:, lo:hi]
        mp_ref[:, lo:hi] = _pool_out(pooled, g, wpool_ref, bpool_ref, pscale_ref, g_pool).astype(BF16)


def _inproj_s(x, mod, wts, tabs, state, wukt):
    n, d = x.shape
    args = (x, mod, *wts, *tabs, state, wukt)
    out_shape = [
        jax.ShapeDtypeStruct((n, N_HEADS * HEAD_PAD), F32),
        jax.ShapeDtypeStruct((n, N_HEADS * HEAD_PAD), F32),
        jax.ShapeDtypeStruct((n, KV_LORA), F32),
        jax.ShapeDtypeStruct((n, ROPE_DIM), F32),
        jax.ShapeDtypeStruct((n, D_ATT), F32),
        jax.ShapeDtypeStruct((n, D_POOL), BF16),
        jax.ShapeDtypeStruct((n, D_POOL), F32),
    ]
    return pl.pallas_call(
        _inproj_s_kernel,
        in_specs=[_full(a.shape) for a in args],
        out_specs=[_full(o.shape) for o in out_shape],
        out_shape=out_shape,
        grid=(1,),
        compiler_params=pltpu.CompilerParams(dimension_semantics=("arbitrary",), vmem_limit_bytes=VMEM_LIMIT),
        name="inproj_s",
    )(*args)


def _flash_kernel(q_ref, k_ref, v_ref, sg_ref, o_ref, *, tq, tk):
    qi = pl.program_id(2)
    c = SM_SCALE * LOG2E
    outs, ls = [], []
    for hh in range(2):
        hs = slice(hh * HEAD_PAD, (hh + 1) * HEAD_PAD)
        q = q_ref[0, :, hs]

        def block(j, carry, masked):
            m, l, acc = carry
            start = pl.multiple_of(j * tk, tk)
            kb = k_ref[0, pl.ds(start, tk), hs]
            vb = v_ref[0, pl.ds(start, tk), :]
            s = lax.dot_general(q, kb, (((1,), (1,)), ((), ())), preferred_element_type=F32)
            if masked:
                qpos = qi * tq + lax.broadcasted_iota(jnp.int32, (tq, tk), 0)
                kpos = j * tk + lax.broadcasted_iota(jnp.int32, (tq, tk), 1)
                s = jnp.where(kpos <= qpos, s, -jnp.inf)
            m_new = jnp.maximum(m, jnp.max(s, axis=1, keepdims=True))
            alpha = jnp.exp2((m - m_new) * c)
            p = jnp.exp2((s - m_new) * c)
            l_new = alpha * l + jnp.sum(p, axis=1, keepdims=True)
            acc_new = alpha * acc + jnp.dot(p.astype(BF16), vb, preferred_element_type=F32)
            return m_new, l_new, acc_new

        init = (jnp.full((tq, 1), -jnp.inf, F32), jnp.zeros((tq, 1), F32), jnp.zeros((tq, 2 * V_DIM), F32))
        n_full = (qi * tq) // tk
        carry = lax.fori_loop(0, n_full, functools.partial(block, masked=False), init)
        n_all = ((qi + 1) * tq + tk - 1) // tk
        carry = lax.fori_loop(n_full, n_all, functools.partial(block, masked=True), carry)
        _, l, acc = carry
        outs.append(acc / l)
    lane = lax.broadcasted_iota(jnp.int32, (tq, 2 * V_DIM), 1)
    att = jnp.where(lane < V_DIM, outs[0], outs[1])
    o_ref[0] = (att * sg_ref[0]).astype(BF16)


def _flash(q, k, v, sg, tq=512, tk=512):
    b, s, _ = q.shape
    n_pairs = N_HEADS // 2
    return pl.pallas_call(
        functools.partial(_flash_kernel, tq=tq, tk=tk),
        grid=(b, n_pairs, s // tq),
        in_specs=[pl.BlockSpec((1, tq, 2 * HEAD_PAD), lambda bi, hp, qi: (bi, qi, hp)),
                  pl.BlockSpec((1, s, 2 * HEAD_PAD), lambda bi, hp, qi: (bi, 0, hp)),
                  pl.BlockSpec((1, s, 2 * V_DIM), lambda bi, hp, qi: (bi, 0, hp)),
                  pl.BlockSpec((1, tq, 2 * V_DIM), lambda bi, hp, qi: (bi, qi, hp))],
        out_specs=pl.BlockSpec((1, tq, 2 * V_DIM), lambda bi, hp, qi: (bi, qi, hp)),
        out_shape=jax.ShapeDtypeStruct((b, s, D_ATT), BF16),
        compiler_params=pltpu.CompilerParams(dimension_semantics=("arbitrary", "arbitrary", "arbitrary"),
                                             vmem_limit_bytes=VMEM_LIMIT),
        name="flash",
    )(q, k, v, sg)


def _decode_kernel(pt_ref, qa_ref, qr_ref, ckvn_ref, krn_ref, wv_ref, sg_ref, cckv_hbm, ckr_hbm,
                   o_ref, ckv_buf, kr_buf, sem, m_ref, l_ref, acc_ref, *, ppc, n_chunks, total):
    bi = pl.program_id(0)
    ci = pl.program_id(1)
    step = bi * n_chunks + ci
    slot = step % 2

    def copies(step_idx, sl):
        out = []
        for p in range(ppc):
            page = pt_ref[step_idx * ppc + p]
            rows = pl.ds(p * PAGE_SIZE, PAGE_SIZE)
            out.append(pltpu.make_async_copy(cckv_hbm.at[page], ckv_buf.at[sl, rows], sem.at[0, sl]))
            out.append(pltpu.make_async_copy(ckr_hbm.at[page], kr_buf.at[sl, rows], sem.at[1, sl]))
        return out

    @pl.when(step == 0)
    def _():
        for cp in copies(0, 0):
            cp.start()

    @pl.when(step + 1 < total)
    def _():
        for cp in copies(step + 1, 1 - slot):
            cp.start()

    @pl.when(ci == 0)
    def _():
        m_ref[...] = jnp.full(m_ref.shape, -jnp.inf, F32)
        l_ref[...] = jnp.zeros(l_ref.shape, F32)
        acc_ref[...] = jnp.zeros(acc_ref.shape, F32)

    for cp in copies(step, slot):
        cp.wait()

    qa = qa_ref[0].astype(BF16)
    qr = qr_ref[0].astype(BF16)
    ckv = ckv_buf[slot].astype(BF16)
    kr = kr_buf[slot].astype(BF16)
    nt = (((1,), (1,)), ((), ()))
    s = (lax.dot_general(qa, ckv, nt, preferred_element_type=F32)
         + lax.dot_general(qr, kr, nt, preferred_element_type=F32))
    c = SM_SCALE * LOG2E
    m = m_ref[...]
    m_new = jnp.maximum(m, jnp.max(s, axis=1, keepdims=True))
    alpha = jnp.exp2((m - m_new) * c)
    p = jnp.exp2((s - m_new) * c)
    l_ref[...] = alpha * l_ref[...] + jnp.sum(p, axis=1, keepdims=True)
    acc_ref[...] = alpha * acc_ref[...] + jnp.dot(p.astype(BF16), ckv, preferred_element_type=F32)
    m_ref[...] = m_new

    @pl.when(ci == n_chunks - 1)
    def _():
        ckvn = ckvn_ref[0].astype(BF16).astype(F32)
        krn = krn_ref[0].astype(BF16).astype(F32)
        s_n = (jnp.sum(qa.astype(F32) * ckvn, axis=1, keepdims=True)
               + jnp.sum(qr.astype(F32) * krn, axis=1, keepdims=True))
        m_f = jnp.maximum(m_new, s_n)
        a_f = jnp.exp2((m_new - m_f) * c)
        p_n = jnp.exp2((s_n - m_f) * c)
        l_f = a_f * l_ref[...] + p_n
        o_lat = (a_f * acc_ref[...] + p_n.astype(BF16).astype(F32) * ckvn) / l_f
        o_all = _bdot(o_lat, wv_ref[...])
        hrow = lax.broadcasted_iota(jnp.int32, o_all.shape, 0)
        hcol = lax.broadcasted_iota(jnp.int32, o_all.shape, 1) // V_DIM
        att = jnp.sum(jnp.where(hrow == hcol, o_all, 0.0), axis=0, keepdims=True)
        o_ref[0] = (att * sg_ref[0]).astype(BF16)


def _decode(page_table, qabs, qrope, ckv_new, kr_new, wv, sg, cache_ckv, cache_kr, ppc=16):
    n, n_pages = page_table.shape
    n_chunks = n_pages // ppc
    keys = ppc * PAGE_SIZE
    total = n * n_chunks
    per_b = lambda w: pl.BlockSpec((1, 1, w), lambda bi, ci, pt: (bi, 0, 0))
    grid_spec = pltpu.PrefetchScalarGridSpec(
        num_scalar_prefetch=1,
        grid=(n, n_chunks),
        in_specs=[pl.BlockSpec((1, N_HEADS, KV_LORA), lambda bi, ci, pt: (bi, 0, 0)),
                  pl.BlockSpec((1, N_HEADS, ROPE_DIM), lambda bi, ci, pt: (bi, 0, 0)),
                  per_b(KV_LORA), per_b(ROPE_DIM),
                  pl.BlockSpec(wv.shape, lambda bi, ci, pt: (0, 0)),
                  per_b(D_ATT),
                  pl.BlockSpec(memory_space=pl.ANY),
                  pl.BlockSpec(memory_space=pl.ANY)],
        out_specs=per_b(D_ATT),
        scratch_shapes=[pltpu.VMEM((2, keys, KV_LORA), F32),
                        pltpu.VMEM((2, keys, ROPE_DIM), F32),
                        pltpu.SemaphoreType.DMA((2, 2)),
                        pltpu.VMEM((N_HEADS, 1), F32),
                        pltpu.VMEM((N_HEADS, 1), F32),
                        pltpu.VMEM((N_HEADS, KV_LORA), F32)],
    )
    return pl.pallas_call(
        functools.partial(_decode_kernel, ppc=ppc, n_chunks=n_chunks, total=total),
        grid_spec=grid_spec,
        out_shape=jax.ShapeDtypeStruct((n, 1, D_ATT), BF16),
        compiler_params=pltpu.CompilerParams(dimension_semantics=("arbitrary", "arbitrary"),
                                             vmem_limit_bytes=VMEM_LIMIT),
        name="decode",
    )(page_table.reshape(-1), qabs, qrope, ckv_new, kr_new, wv, sg, cache_ckv, cache_kr)


def _combine_kernel(x_ref, gate_ref, ag_ref, mp_ref, wo_ref, fnw_ref, y_ref):
    proj = (jnp.dot(ag_ref[0], wo_ref[0:D_ATT, :], preferred_element_type=F32)
            + jnp.dot(mp_ref[0], wo_ref[D_ATT:D_ATT + D_POOL, :], preferred_element_type=F32))
    y = x_ref[0] + gate_ref[0] * proj
    y_ref[0] = _rms(y, fnw_ref[...])


def _combine(x, gate, ag, mp, wo, fnw, tm):
    b, s, d = x.shape
    gr = gate.shape[1]
    gate_spec = (pl.BlockSpec((1, 1, d), lambda bi, i: (bi, 0, 0)) if gr == 1
                 else pl.BlockSpec((1, tm, d), lambda bi, i: (bi, i, 0)))
    row = lambda w: pl.BlockSpec((1, tm, w), lambda bi, i: (bi, i, 0))
    return pl.pallas_call(
        _combine_kernel,
        grid=(b, s // tm),
        in_specs=[row(d), gate_spec, row(D_ATT), row(D_POOL), _full(wo.shape), _full(fnw.shape)],
        out_specs=row(d),
        out_shape=jax.ShapeDtypeStruct((b, s, d), F32),
        compiler_params=pltpu.CompilerParams(dimension_semantics=("arbitrary", "arbitrary"),
                                             vmem_limit_bytes=VMEM_LIMIT),
        name="combine",
    )(x, gate, ag, mp, wo, fnw)


def _rope_tables(pos):
    inv = ROPE_BASE ** (-jnp.arange(0, ROPE_DIM, 2, dtype=F32) / ROPE_DIM)
    ang = pos.astype(F32)[:, None] * inv[None, :]
    cos, sin = jnp.cos(ang), jnp.sin(ang)
    t = pos.shape[0]
    cc, ss = jnp.concatenate([cos, cos], axis=1), jnp.concatenate([sin, sin], axis=1)
    zk = jnp.zeros((t, HEAD_PAD - ROPE_DIM), F32)
    zq = jnp.zeros((t, HEAD_PAD - QK_DIM), F32)
    cosk = jnp.concatenate([cc, zk], axis=1)
    sink = jnp.concatenate([ss, zk], axis=1)
    cosq = jnp.concatenate([jnp.ones((t, NOPE_DIM), F32), cc, zq], axis=1)
    sinq = jnp.concatenate([jnp.zeros((t, NOPE_DIM), F32), ss, zq], axis=1)
    return cosk, sink, cosq, sinq


def _rot_half(w):
    half = w.shape[-1] // 2
    return jnp.concatenate([-w[..., half:], w[..., :half]], axis=-1)


def _pack_weights(norm_w, w_in, q_norm_w, w_uq, kv_norm_w, w_uk, w_uv, w_pool, b_pool, pool_scale):
    d = w_in.shape[0]
    off_kr = Q_LORA + KV_LORA
    krc = w_in[:, off_kr:off_kr + ROPE_DIM]
    win = jnp.concatenate([w_in[:, :off_kr], krc, _rot_half(krc),
                           jnp.zeros((d, HEAD_PAD - 2 * ROPE_DIM), F32), w_in[:, off_kr + ROPE_DIM:]], axis=1)
    pad_q = HEAD_PAD - QK_DIM
    wq = jnp.pad(w_uq, ((0, 0), (0, 0), (0, pad_q))).reshape(Q_LORA, N_HEADS * HEAD_PAD)
    wqr = jnp.pad(_rot_half(w_uq[..., NOPE_DIM:]), ((0, 0), (0, 0), (NOPE_DIM, pad_q)))
    wqr = wqr.reshape(Q_LORA, N_HEADS * HEAD_PAD)
    wk_nope = jnp.pad(w_uk, ((0, 0), (0, 0), (0, HEAD_PAD - NOPE_DIM))).reshape(KV_LORA, N_HEADS * HEAD_PAD)
    place = jnp.pad(jnp.eye(ROPE_DIM, dtype=F32), ((0, HEAD_PAD - ROPE_DIM), (NOPE_DIM, pad_q)))
    wk = jnp.concatenate([wk_nope, jnp.tile(place, (1, N_HEADS))], axis=0)
    wv = w_uv.reshape(KV_LORA, N_HEADS * V_DIM)
    return (norm_w.reshape(1, -1), win.astype(BF16), q_norm_w.reshape(1, -1), kv_norm_w.reshape(1, -1),
            wq.astype(BF16), wqr.astype(BF16), wk.astype(BF16), wv.astype(BF16),
            w_pool.astype(BF16), b_pool.reshape(1, -1), pool_scale.reshape(1, -1))


def kernel(x_prompt, x_sample, c_prompt, c_sample, cache_ckv, cache_krope, state_pool, page_table, ada_w, ada_b, norm_w, w_in, q_norm_w, w_uq, kv_norm_w, w_uk, w_uv, w_pool, b_pool, pool_scale, w_out, final_norm_w):
    assert ada_w.shape[0] == 1, "single-layer trunk only"
    b, s, d = x_prompt.shape
    n = x_sample.shape[0]
    assert x_sample.shape[1] == 1
    past_len = page_table.shape[1] * PAGE_SIZE

    c_all = jnp.concatenate([c_prompt, c_sample], axis=0)
    rows = -(-c_all.shape[0] // 8) * 8
    c_all = jnp.pad(c_all, ((0, rows - c_all.shape[0]), (0, 0)))
    mod = _ada(c_all, ada_w[0].astype(BF16), ada_b[0].reshape(1, -1))
    mod_p = mod[:b].reshape(b, 3, d)
    mod_s = mod[b:b + n]

    wts = _pack_weights(norm_w[0], w_in[0], q_norm_w[0], w_uq[0], kv_norm_w[0], w_uk[0], w_uv[0],
                        w_pool[0], b_pool[0], pool_scale[0])
    wv = wts[7]
    wo = w_out[0].astype(BF16)
    fnw = final_norm_w.reshape(1, -1)

    tabs_p = _rope_tables(jnp.arange(s))
    q, k, v, ckv_p, kr_p, sg_p, mp_p, utail = _inproj_p(x_prompt, mod_p, wts, tabs_p)
    ag_p = _flash(q, k, v, sg_p)
    y_prompt = _combine(x_prompt, mod_p[:, 2:3, :], ag_p, mp_p, wo, fnw, tm=512)

    tabs_s = tuple(jnp.broadcast_to(t, (n, HEAD_PAD)) for t in _rope_tables(jnp.full((1,), past_len)))
    wukt = jnp.pad(jnp.transpose(w_uk[0], (1, 2, 0)), ((0, 0), (0, HEAD_PAD - NOPE_DIM), (0, 0))).astype(BF16)
    qabs, qrope, ckv_s, kr_s, sg_s, mp_s, u_s = _inproj_s(x_sample[:, 0, :], mod_s, wts, tabs_s, state_pool[0], wukt)
    qrope = qrope.reshape(n, N_HEADS, HEAD_PAD)[:, :, NOPE_DIM:QK_DIM]
    ag_s = _decode(page_table, qabs.reshape(n, N_HEADS, KV_LORA), qrope, ckv_s[:, None, :], kr_s[:, None, :],
                   wv, sg_s[:, None, :], cache_ckv[0], cache_krope[0])
    y_sample = _combine(x_sample.reshape(1, n, d), mod_s[None, :, 2 * d:], ag_s.reshape(1, n, D_ATT),
                        mp_s[None], wo, fnw, tm=n).reshape(n, 1, d)

    new_pool_p = utail[:, HALO - POOL_PREV:, :]
    new_pool_s = jnp.concatenate([state_pool[0][:, 1:, :], u_s[:, None, :]], axis=1)
    return (y_prompt, y_sample,
            ckv_p[None], kr_p[None], new_pool_p[None],
            ckv_s[:, None, :][None], kr_s[:, None, :][None], new_pool_s[None])
```

```python
import functools
import math

import jax
import jax.numpy as jnp
from jax import lax
from jax.experimental import pallas as pl
from jax.experimental.pallas import tpu as pltpu

F32 = jnp.float32
BF16 = jnp.bfloat16

N_HEADS = 8
NOPE_DIM = 64
ROPE_DIM = 32
QK_DIM = NOPE_DIM + ROPE_DIM
V_DIM = 64
Q_LORA = 256
KV_LORA = 128
D_POOL = 512
D_ATT = 512
POOL_WINDOWS = (2, 4, 8, 16)
POOL_GROUP = 128
POOL_PREV = 15
PAGE_SIZE = 128
ROPE_BASE = 10000.0
EPS = 1e-6
SM_SCALE = QK_DIM ** -0.5
LOG2E = math.log2(math.e)
QK_PRESCALE = SM_SCALE * LOG2E
_NT = (((1,), (1,)), ((), ()))

HEAD_PAD = 128
HALO = 16
C_Q, C_KV, C_KR, C_GA, C_U, C_GP, C_END = 0, 256, 384, 512, 1024, 1536, 2048

VMEM_LIMIT = 56 * 1024 * 1024


def _rms(x, w):
    return (x * lax.rsqrt(jnp.mean(x * x, axis=-1, keepdims=True) + EPS)) * w


def _silu(x):
    return x * jax.nn.sigmoid(x)


def _bdot(a, b):
    return jnp.dot(a.astype(BF16), b, preferred_element_type=F32)


def _ada_kernel(c_ref, w_ref, b_ref, o_ref):
    o_ref[...] = _bdot(_silu(c_ref[...]), w_ref[...]) + b_ref[...]


def _ada(c_all, ada_w, ada_b):
    rows, d = c_all.shape
    n = ada_w.shape[1]
    bn = 1024
    return pl.pallas_call(
        _ada_kernel,
        grid=(n // bn,),
        in_specs=[pl.BlockSpec((rows, d), lambda j: (0, 0)),
                  pl.BlockSpec((d, bn), lambda j: (0, j)),
                  pl.BlockSpec((1, bn), lambda j: (0, j))],
        out_specs=pl.BlockSpec((rows, bn), lambda j: (0, j)),
        out_shape=jax.ShapeDtypeStruct((rows, n), F32),
        compiler_params=pltpu.CompilerParams(dimension_semantics=("arbitrary",)),
        name="ada",
    )(c_all, ada_w, ada_b)


def _project(x, shift, scale, nw, win, qnw, kvnw, wq, wqr, cosk, sink):
    h = _rms(x, nw) * (1.0 + scale) + shift
    z = _bdot(h, win)
    q_lat = _rms(z[:, C_Q:C_KV], qnw)
    ckv = _rms(z[:, C_KV:C_KR], kvnw)
    krc = z[:, C_KR:C_GA]
    kr128 = krc * cosk + pltpu.roll(krc, HEAD_PAD - ROPE_DIM, axis=1) * sink
    ql = q_lat.astype(BF16)
    qa = jnp.dot(ql, wq, preferred_element_type=F32)
    qb = jnp.dot(ql, wqr, preferred_element_type=F32)
    return qa, qb, ckv, kr128, z[:, C_GA:C_U], z[:, C_U:C_GP], z[:, C_GP:C_END]


def _pool_out(pooled_g, g, wpool_ref, bpool_ref, pscale_ref, g_pool):
    lo, hi = g * POOL_GROUP, (g + 1) * POOL_GROUP
    y = _bdot(pooled_g, wpool_ref[g]) + bpool_ref[:, lo:hi]
    y = y * pscale_ref[:, lo:hi]
    return y * _silu(g_pool[:, lo:hi])


def _inproj_p_kernel(x_ref, mod_ref, nw_ref, win_ref, qnw_ref, kvnw_ref, wq_ref, wqr_ref,
                     wpool_ref, bpool_ref, pscale_ref, wk_ref, wvt_ref, cosk_ref, sink_ref, cosq_ref, sinq_ref,
                     q_ref, k_ref, vt_ref, ckv_ref, kr_ref, sg_ref, mp_ref, utail_ref, uext_ref, *, tm):
    i = pl.program_id(1)
    shift = mod_ref[0, 0:1, :]
    scale = mod_ref[0, 1:2, :]
    qa, qb, ckv, kr128, g_att, u, g_pool = _project(
        x_ref[0], shift, scale, nw_ref[...], win_ref[...], qnw_ref[...], kvnw_ref[...],
        wq_ref[...], wqr_ref[...], cosk_ref[...], sink_ref[...])
    cosq, sinq = cosq_ref[...] * QK_PRESCALE, sinq_ref[...] * QK_PRESCALE
    for h in range(N_HEADS):
        sl = slice(h * HEAD_PAD, (h + 1) * HEAD_PAD)
        q_ref[0, :, sl] = (qa[:, sl] * cosq + qb[:, sl] * sinq).astype(BF16)
    ckv_b = ckv.astype(BF16)
    k = jnp.dot(jnp.concatenate([ckv_b, kr128.astype(BF16)], axis=1), wk_ref[...], preferred_element_type=F32)
    k_ref[0] = k.astype(BF16)
    vt_ref[0] = lax.dot_general(wvt_ref[...], ckv_b, _NT, preferred_element_type=F32).astype(BF16)
    ckv_ref[0] = ckv
    kr_ref[0] = kr128[:, :ROPE_DIM]
    sg_ref[0] = _silu(g_att)

    @pl.when(i == 0)
    def _():
        uext_ref[0:HALO, :] = jnp.zeros((HALO, D_POOL), F32)

    uext_ref[HALO:HALO + tm, :] = u
    pos = (i * tm + lax.broadcasted_iota(jnp.int32, (tm, 1), 0)).astype(F32)
    for g, w in enumerate(POOL_WINDOWS):
        lo, hi = g * POOL_GROUP, (g + 1) * POOL_GROUP
        wsum = uext_ref[HALO:HALO + tm, lo:hi]
        for d in range(1, w):
            wsum = wsum + uext_ref[HALO - d:HALO - d + tm, lo:hi]
        cnt = jnp.minimum(pos + 1.0, float(w))
        pooled = wsum / cnt - u[:, lo:hi]
        mp_ref[0, :, lo:hi] = _pool_out(pooled, g, wpool_ref, bpool_ref, pscale_ref, g_pool).astype(BF16)
    uext_ref[0:HALO, :] = u[tm - HALO:tm, :]

    @pl.when(i == pl.num_programs(1) - 1)
    def _():
        utail_ref[0] = u[tm - HALO:tm, :]


def _full(shape):
    nd = len(shape)
    return pl.BlockSpec(shape, lambda *_: (0,) * nd)


def _inproj_p(x, mod3, wts, wk, wvt, tabs, tm=512):
    b, s, d = x.shape
    nt = s // tm
    row = lambda w: pl.BlockSpec((1, tm, w), lambda bi, i: (bi, i, 0))
    tab = pl.BlockSpec((tm, HEAD_PAD), lambda bi, i: (i, 0))
    in_specs = [row(d), pl.BlockSpec((1, 3, d), lambda bi, i: (bi, 0, 0))]
    in_specs += [_full(w.shape) for w in (*wts, wk, wvt)]
    in_specs += [tab, tab, tab, tab]
    out_shape = [
        jax.ShapeDtypeStruct((b, s, N_HEADS * HEAD_PAD), BF16),
        jax.ShapeDtypeStruct((b, s, N_HEADS * HEAD_PAD), BF16),
        jax.ShapeDtypeStruct((b, D_ATT, s), BF16),
        jax.ShapeDtypeStruct((b, s, KV_LORA), F32),
        jax.ShapeDtypeStruct((b, s, ROPE_DIM), F32),
        jax.ShapeDtypeStruct((b, s, D_ATT), F32),
        jax.ShapeDtypeStruct((b, s, D_POOL), BF16),
        jax.ShapeDtypeStruct((b, HALO, D_POOL), F32),
    ]
    out_specs = [row(N_HEADS * HEAD_PAD), row(N_HEADS * HEAD_PAD),
                 pl.BlockSpec((1, D_ATT, tm), lambda bi, i: (bi, 0, i)), row(KV_LORA), row(ROPE_DIM),
                 row(D_ATT), row(D_POOL), pl.BlockSpec((1, HALO, D_POOL), lambda bi, i: (bi, 0, 0))]
    return pl.pallas_call(
        functools.partial(_inproj_p_kernel, tm=tm),
        grid=(b, nt),
        in_specs=in_specs,
        out_specs=out_specs,
        out_shape=out_shape,
        scratch_shapes=[pltpu.VMEM((tm + HALO, D_POOL), F32)],
        compiler_params=pltpu.CompilerParams(dimension_semantics=("arbitrary", "arbitrary"),
                                             vmem_limit_bytes=VMEM_LIMIT),
        name="inproj_p",
    )(x, mod3, *wts, wk, wvt, *tabs)


def _inproj_s_kernel(x_ref, mod_ref, nw_ref, win_ref, qnw_ref, kvnw_ref, wq_ref, wqr_ref,
                     wpool_ref, bpool_ref, pscale_ref, cosk_ref, sink_ref, cosq_ref, sinq_ref,
                     state_ref, wukt_ref,
                     qabs_ref, qrope_ref, ckv_ref, kr_ref, sg_ref, mp_ref, u_ref):
    d = x_ref.shape[1]
    shift = mod_ref[:, 0:d]
    scale = mod_ref[:, d:2 * d]
    cosq, sinq = cosq_ref[...], sinq_ref[...]
    qa, qb, ckv, kr128, g_att, u, g_pool = _project(
        x_ref[...], shift, scale, nw_ref[...], win_ref[...], qnw_ref[...], kvnw_ref[...],
        wq_ref[...], wqr_ref[...], cosk_ref[...], sink_ref[...])
    for h in range(N_HEADS):
        sl = slice(h * HEAD_PAD, (h + 1) * HEAD_PAD)
        qrope_ref[:, sl] = qa[:, sl] * cosq + qb[:, sl] * sinq
        qabs_ref[:, sl] = _bdot(qa[:, sl], wukt_ref[h])
    ckv_ref[...] = ckv
    kr_ref[...] = kr128[:, :ROPE_DIM]
    sg_ref[...] = _silu(g_att)
    u_ref[...] = u
    for g, w in enumerate(POOL_WINDOWS):
        lo, hi = g * POOL_GROUP, (g + 1) * POOL_GROUP
        wsum = u[:, lo:hi]
        for j in range(1, w):
            wsum = wsum + state_ref[POOL_PREV - j, :, lo:hi]
        pooled = wsum / float(w) - u[:, lo:hi]
        mp_ref[:, lo:hi] = _pool_out(pooled, g, wpool_ref, bpool_ref, pscale_ref, g_pool).astype(BF16)


def _inproj_s(x, mod, wts, tabs, state, wukt):
    n, d = x.shape
    args = (x, mod, *wts, *tabs, state, wukt)
    out_shape = [
        jax.ShapeDtypeStruct((n, N_HEADS * HEAD_PAD), F32),
        jax.ShapeDtypeStruct((n, N_HEADS * HEAD_PAD), F32),
        jax.ShapeDtypeStruct((n, KV_LORA), F32),
        jax.ShapeDtypeStruct((n, ROPE_DIM), F32),
        jax.ShapeDtypeStruct((n, D_ATT), F32),
        jax.ShapeDtypeStruct((n, D_POOL), BF16),
        jax.ShapeDtypeStruct((n, D_POOL), F32),
    ]
    return pl.pallas_call(
        _inproj_s_kernel,
        in_specs=[_full(a.shape) for a in args],
        out_specs=[_full(o.shape) for o in out_shape],
        out_shape=out_shape,
        grid=(1,),
        compiler_params=pltpu.CompilerParams(dimension_semantics=("arbitrary",), vmem_limit_bytes=VMEM_LIMIT),
        name="inproj_s",
    )(*args)


def _flash_kernel(q_ref, k_ref, vt_ref, sg_ref, o_ref, *, tq, tk):
    qi = pl.program_id(2)
    qs = [q_ref[0, :, hh * HEAD_PAD:(hh + 1) * HEAD_PAD] for hh in range(2)]

    def block(j, carry, masked):
        start = pl.multiple_of(j * tk, tk)
        out = []
        for hh in range(2):
            m, l, acc = carry[hh]
            kb = k_ref[0, pl.ds(start, tk), hh * HEAD_PAD:(hh + 1) * HEAD_PAD]
            vtb = vt_ref[0, hh * V_DIM:(hh + 1) * V_DIM, pl.ds(start, tk)]
            st = lax.dot_general(kb, qs[hh], _NT, preferred_element_type=F32)
            if masked:
                kpos = j * tk + lax.broadcasted_iota(jnp.int32, (tk, tq), 0)
                qpos = qi * tq + lax.broadcasted_iota(jnp.int32, (tk, tq), 1)
                st = jnp.where(kpos <= qpos, st, -jnp.inf)
            m_new = jnp.maximum(m, jnp.max(st, axis=0, keepdims=True))
            alpha = jnp.exp2(m - m_new)
            pt = jnp.exp2(st - m_new)
            l_new = alpha * l + jnp.sum(pt, axis=0, keepdims=True)
            acc_new = alpha * acc + jnp.dot(vtb, pt.astype(BF16), preferred_element_type=F32)
            out.append((m_new, l_new, acc_new))
        return tuple(out)

    init = tuple((jnp.full((1, tq), -jnp.inf, F32), jnp.zeros((1, tq), F32), jnp.zeros((V_DIM, tq), F32))
                 for _ in range(2))
    n_full = (qi * tq) // tk
    carry = lax.fori_loop(0, n_full, functools.partial(block, masked=False), init)
    n_all = ((qi + 1) * tq + tk - 1) // tk
    carry = lax.fori_loop(n_full, n_all, functools.partial(block, masked=True), carry)
    att_t = jnp.concatenate([carry[hh][2] / carry[hh][1] for hh in range(2)], axis=0)
    o_ref[0] = (att_t.T * sg_ref[0]).astype(BF16)


def _flash(q, k, vt, sg, tq=512, tk=512):
    b, s, _ = q.shape
    n_pairs = N_HEADS // 2
    return pl.pallas_call(
        functools.partial(_flash_kernel, tq=tq, tk=tk),
        grid=(b, n_pairs, s // tq),
        in_specs=[pl.BlockSpec((1, tq, 2 * HEAD_PAD), lambda bi, hp, qi: (bi, qi, hp)),
                  pl.BlockSpec((1, s, 2 * HEAD_PAD), lambda bi, hp, qi: (bi, 0, hp)),
                  pl.BlockSpec((1, 2 * V_DIM, s), lambda bi, hp, qi: (bi, hp, 0)),
                  pl.BlockSpec((1, tq, 2 * V_DIM), lambda bi, hp, qi: (bi, qi, hp))],
        out_specs=pl.BlockSpec((1, tq, 2 * V_DIM), lambda bi, hp, qi: (bi, qi, hp)),
        out_shape=jax.ShapeDtypeStruct((b, s, D_ATT), BF16),
        compiler_params=pltpu.CompilerParams(dimension_semantics=("arbitrary", "arbitrary", "arbitrary"),
                                             vmem_limit_bytes=VMEM_LIMIT),
        name="flash",
    )(q, k, vt, sg)


def _decode_kernel(pt_ref, qa_ref, qr_ref, ckvn_ref, krn_ref, wv_ref, sg_ref, cckv_hbm, ckrt_hbm,
                   o_ref, ckv_buf, krt_buf, sem, *, n_pages):
    bi = pl.program_id(0)
    slot = bi % 2

    def page_copies(row, sl, p):
        page = pt_ref[row * n_pages + p]
        cols = pl.ds(pl.multiple_of(p * PAGE_SIZE, PAGE_SIZE), PAGE_SIZE)
        return (pltpu.make_async_copy(cckv_hbm.at[0, page], ckv_buf.at[sl, cols], sem.at[0, sl]),
                pltpu.make_async_copy(ckrt_hbm.at[0, page], krt_buf.at[sl, :, cols], sem.at[1, sl]))

    def start_row(row, sl):
        def body(p, _):
            for cp in page_copies(row, sl, p):
                cp.start()
            return 0
        lax.fori_loop(0, n_pages, body, 0, unroll=8)

    @pl.when(bi == 0)
    def _():
        start_row(0, 0)

    @pl.when(bi + 1 < pl.num_programs(0))
    def _():
        start_row(bi + 1, 1 - slot)

    def wait_body(p, _):
        for cp in page_copies(bi, slot, p):
            cp.wait()
        return 0
    lax.fori_loop(0, n_pages, wait_body, 0, unroll=8)

    qa = qa_ref[0].astype(BF16)
    qr = qr_ref[0].astype(BF16)
    ckv = ckv_buf[slot].astype(BF16)
    krt = krt_buf[slot].astype(BF16)
    s = (lax.dot_general(qa, ckv, _NT, preferred_element_type=F32)
         + jnp.dot(qr, krt, preferred_element_type=F32))
    ckvn = ckvn_ref[0].astype(BF16).astype(F32)
    krn = krn_ref[0].astype(BF16).astype(F32)
    s_n = (jnp.sum(qa.astype(F32) * ckvn, axis=1, keepdims=True)
           + jnp.sum(qr.astype(F32) * krn, axis=1, keepdims=True))
    m = jnp.maximum(jnp.max(s, axis=1, keepdims=True), s_n)
    p = jnp.exp2((s - m) * QK_PRESCALE)
    p_n = jnp.exp2((s_n - m) * QK_PRESCALE)
    l = jnp.sum(p, axis=1, keepdims=True) + p_n
    o_lat = (jnp.dot(p.astype(BF16), ckv, preferred_element_type=F32)
             + p_n.astype(BF16).astype(F32) * ckvn) / l
    o_all = _bdot(o_lat, wv_ref[...])
    hrow = lax.broadcasted_iota(jnp.int32, o_all.shape, 0)
    hcol = lax.broadcasted_iota(jnp.int32, o_all.shape, 1) // V_DIM
    att = jnp.sum(jnp.where(hrow == hcol, o_all, 0.0), axis=0, keepdims=True)
    o_ref[0] = (att * sg_ref[0]).astype(BF16)


def _decode(page_table, qabs, qrope, ckv_new, kr_new, wv, sg, cache_ckv, cache_krt):
    n, n_pages = page_table.shape
    past = n_pages * PAGE_SIZE
    per_b = lambda w: pl.BlockSpec((1, 1, w), lambda bi, pt: (bi, 0, 0))
    grid_spec = pltpu.PrefetchScalarGridSpec(
        num_scalar_prefetch=1,
        grid=(n,),
        in_specs=[pl.BlockSpec((1, N_HEADS, KV_LORA), lambda bi, pt: (bi, 0, 0)),
                  pl.BlockSpec((1, N_HEADS, ROPE_DIM), lambda bi, pt: (bi, 0, 0)),
                  per_b(KV_LORA), per_b(ROPE_DIM),
                  pl.BlockSpec(wv.shape, lambda bi, pt: (0, 0)),
                  per_b(D_ATT),
                  pl.BlockSpec(memory_space=pl.ANY),
                  pl.BlockSpec(memory_space=pl.ANY)],
        out_specs=per_b(D_ATT),
        scratch_shapes=[pltpu.VMEM((2, past, KV_LORA), F32),
                        pltpu.VMEM((2, ROPE_DIM, past), F32),
                        pltpu.SemaphoreType.DMA((2, 2))],
    )
    return pl.pallas_call(
        functools.partial(_decode_kernel, n_pages=n_pages),
        grid_spec=grid_spec,
        out_shape=jax.ShapeDtypeStruct((n, 1, D_ATT), BF16),
        compiler_params=pltpu.CompilerParams(dimension_semantics=("arbitrary",), vmem_limit_bytes=VMEM_LIMIT),
        name="decode",
    )(page_table.reshape(-1), qabs, qrope, ckv_new, kr_new, wv, sg, cache_ckv, cache_krt)


def _combine_kernel(x_ref, gate_ref, ag_ref, mp_ref, wo_ref, fnw_ref, y_ref):
    proj = (jnp.dot(ag_ref[0], wo_ref[0:D_ATT, :], preferred_element_type=F32)
            + jnp.dot(mp_ref[0], wo_ref[D_ATT:D_ATT + D_POOL, :], preferred_element_type=F32))
    y = x_ref[0] + gate_ref[0] * proj
    y_ref[0] = _rms(y, fnw_ref[...])


def _combine(x, gate, ag, mp, wo, fnw, tm):
    b, s, d = x.shape
    gr = gate.shape[1]
    gate_spec = (pl.BlockSpec((1, 1, d), lambda bi, i: (bi, 0, 0)) if gr == 1
                 else pl.BlockSpec((1, tm, d), lambda bi, i: (bi, i, 0)))
    row = lambda w: pl.BlockSpec((1, tm, w), lambda bi, i: (bi, i, 0))
    return pl.pallas_call(
        _combine_kernel,
        grid=(b, s // tm),
        in_specs=[row(d), gate_spec, row(D_ATT), row(D_POOL), _full(wo.shape), _full(fnw.shape)],
        out_specs=row(d),
        out_shape=jax.ShapeDtypeStruct((b, s, d), F32),
        compiler_params=pltpu.CompilerParams(dimension_semantics=("arbitrary", "arbitrary"),
                                             vmem_limit_bytes=VMEM_LIMIT),
        name="combine",
    )(x, gate, ag, mp, wo, fnw)


def _rope_tables(pos):
    inv = ROPE_BASE ** (-jnp.arange(0, ROPE_DIM, 2, dtype=F32) / ROPE_DIM)
    ang = pos.astype(F32)[:, None] * inv[None, :]
    cos, sin = jnp.cos(ang), jnp.sin(ang)
    t = pos.shape[0]
    cc, ss = jnp.concatenate([cos, cos], axis=1), jnp.concatenate([sin, sin], axis=1)
    zk = jnp.zeros((t, HEAD_PAD - ROPE_DIM), F32)
    zq = jnp.zeros((t, HEAD_PAD - QK_DIM), F32)
    cosk = jnp.concatenate([cc, zk], axis=1)
    sink = jnp.concatenate([ss, zk], axis=1)
    cosq = jnp.concatenate([jnp.ones((t, NOPE_DIM), F32), cc, zq], axis=1)
    sinq = jnp.concatenate([jnp.zeros((t, NOPE_DIM), F32), ss, zq], axis=1)
    return cosk, sink, cosq, sinq


def _rot_half(w):
    half = w.shape[-1] // 2
    return jnp.concatenate([-w[..., half:], w[..., :half]], axis=-1)


def _pack_weights(norm_w, w_in, q_norm_w, w_uq, kv_norm_w, w_uk, w_uv, w_pool, b_pool, pool_scale):
    d = w_in.shape[0]
    off_kr = Q_LORA + KV_LORA
    krc = w_in[:, off_kr:off_kr + ROPE_DIM]
    win = jnp.concatenate([w_in[:, :off_kr], krc, _rot_half(krc),
                           jnp.zeros((d, HEAD_PAD - 2 * ROPE_DIM), F32), w_in[:, off_kr + ROPE_DIM:]], axis=1)
    pad_q = HEAD_PAD - QK_DIM
    wq = jnp.pad(w_uq, ((0, 0), (0, 0), (0, pad_q))).reshape(Q_LORA, N_HEADS * HEAD_PAD)
    wqr = jnp.pad(_rot_half(w_uq[..., NOPE_DIM:]), ((0, 0), (0, 0), (NOPE_DIM, pad_q)))
    wqr = wqr.reshape(Q_LORA, N_HEADS * HEAD_PAD)
    wk_nope = jnp.pad(w_uk, ((0, 0), (0, 0), (0, HEAD_PAD - NOPE_DIM))).reshape(KV_LORA, N_HEADS * HEAD_PAD)
    place = jnp.pad(jnp.eye(ROPE_DIM, dtype=F32), ((0, HEAD_PAD - ROPE_DIM), (NOPE_DIM, pad_q)))
    wk = jnp.concatenate([wk_nope, jnp.tile(place, (1, N_HEADS))], axis=0)
    wv = w_uv.reshape(KV_LORA, N_HEADS * V_DIM).astype(BF16)
    shared = (norm_w.reshape(1, -1), win.astype(BF16), q_norm_w.reshape(1, -1), kv_norm_w.reshape(1, -1),
              wq.astype(BF16), wqr.astype(BF16), w_pool.astype(BF16), b_pool.reshape(1, -1),
              pool_scale.reshape(1, -1))
    return shared, wk.astype(BF16), wv


def kernel(x_prompt, x_sample, c_prompt, c_sample, cache_ckv, cache_krope, state_pool, page_table, ada_w, ada_b, norm_w, w_in, q_norm_w, w_uq, kv_norm_w, w_uk, w_uv, w_pool, b_pool, pool_scale, w_out, final_norm_w):
    assert ada_w.shape[0] == 1, "single-layer trunk only"
    b, s, d = x_prompt.shape
    n = x_sample.shape[0]
    assert x_sample.shape[1] == 1
    past_len = page_table.shape[1] * PAGE_SIZE

    c_all = jnp.concatenate([c_prompt, c_sample], axis=0)
    rows = -(-c_all.shape[0] // 8) * 8
    c_all = jnp.pad(c_all, ((0, rows - c_all.shape[0]), (0, 0)))
    mod = _ada(c_all, ada_w[0].astype(BF16), ada_b[0].reshape(1, -1))
    mod_p = mod[:b].reshape(b, 3, d)
    mod_s = mod[b:b + n]

    wts, wk, wv = _pack_weights(norm_w[0], w_in[0], q_norm_w[0], w_uq[0], kv_norm_w[0], w_uk[0], w_uv[0],
                                w_pool[0], b_pool[0], pool_scale[0])
    wo = w_out[0].astype(BF16)
    fnw = final_norm_w.reshape(1, -1)

    tabs_p = _rope_tables(jnp.arange(s))
    q, k, vt, ckv_p, kr_p, sg_p, mp_p, utail = _inproj_p(x_prompt, mod_p, wts, wk, wv.T, tabs_p)
    ag_p = _flash(q, k, vt, sg_p)
    y_prompt = _combine(x_prompt, mod_p[:, 2:3, :], ag_p, mp_p, wo, fnw, tm=512)

    tabs_s = tuple(jnp.broadcast_to(t, (n, HEAD_PAD)) for t in _rope_tables(jnp.full((1,), past_len)))
    wukt = jnp.pad(jnp.transpose(w_uk[0], (1, 2, 0)), ((0, 0), (0, HEAD_PAD - NOPE_DIM), (0, 0))).astype(BF16)
    state_t = jnp.transpose(state_pool[0], (1, 0, 2))
    qabs, qrope, ckv_s, kr_s, sg_s, mp_s, u_s = _inproj_s(x_sample[:, 0, :], mod_s, wts, tabs_s, state_t, wukt)
    qrope = qrope.reshape(n, N_HEADS, HEAD_PAD)[:, :, NOPE_DIM:QK_DIM]
    ag_s = _decode(page_table, qabs.reshape(n, N_HEADS, KV_LORA), qrope, ckv_s[:, None, :], kr_s[:, None, :],
                   wv, sg_s[:, None, :], cache_ckv, jnp.swapaxes(cache_krope, 2, 3))
    y_sample = _combine(x_sample.reshape(1, n, d), mod_s[None, :, 2 * d:], ag_s.reshape(1, n, D_ATT),
                        mp_s[None], wo, fnw, tm=n).reshape(n, 1, d)

    new_pool_p = utail[:, HALO - POOL_PREV:, :]
    new_pool_s = jnp.transpose(jnp.concatenate([state_t[1:], u_s[None]], axis=0), (1, 0, 2))
    return (y_prompt, y_sample,
            ckv_p[None], kr_p[None], new_pool_p[None],
            ckv_s[:, None, :][None], kr_s[:, None, :][None], new_pool_s[None])
```

```python
import functools
import math

import jax
import jax.numpy as jnp
from jax import lax
from jax.experimental import pallas as pl
from jax.experimental.pallas import tpu as pltpu

F32 = jnp.float32
BF16 = jnp.bfloat16

N_HEADS = 8
NOPE_DIM = 64
ROPE_DIM = 32
QK_DIM = NOPE_DIM + ROPE_DIM
V_DIM = 64
Q_LORA = 256
KV_LORA = 128
D_POOL = 512
D_ATT = 512
POOL_WINDOWS = (2, 4, 8, 16)
POOL_GROUP = 128
POOL_PREV = 15
PAGE_SIZE = 128
ROPE_BASE = 10000.0
EPS = 1e-6
SM_SCALE = QK_DIM ** -0.5
LOG2E = math.log2(math.e)
QK_PRESCALE = SM_SCALE * LOG2E
_NT = (((1,), (1,)), ((), ()))

HEAD_PAD = 128
HALO = 16
C_Q, C_KV, C_KR, C_GA, C_U, C_GP, C_END = 0, 256, 384, 512, 1024, 1536, 2048

VMEM_LIMIT = 56 * 1024 * 1024


def _rms(x, w):
    return (x * lax.rsqrt(jnp.mean(x * x, axis=-1, keepdims=True) + EPS)) * w


def _silu(x):
    return x * jax.nn.sigmoid(x)


def _bdot(a, b):
    return jnp.dot(a.astype(BF16), b, preferred_element_type=F32)


def _ada_kernel(c_ref, w_ref, b_ref, o_ref):
    o_ref[...] = _bdot(_silu(c_ref[...]), w_ref[...]) + b_ref[...]


def _ada(c_all, ada_w, ada_b):
    rows, d = c_all.shape
    n = ada_w.shape[1]
    bn = 1024
    return pl.pallas_call(
        _ada_kernel,
        grid=(n // bn,),
        in_specs=[pl.BlockSpec((rows, d), lambda j: (0, 0)),
                  pl.BlockSpec((d, bn), lambda j: (0, j)),
                  pl.BlockSpec((1, bn), lambda j: (0, j))],
        out_specs=pl.BlockSpec((rows, bn), lambda j: (0, j)),
        out_shape=jax.ShapeDtypeStruct((rows, n), F32),
        compiler_params=pltpu.CompilerParams(dimension_semantics=("arbitrary",)),
        name="ada",
    )(c_all, ada_w, ada_b)


def _project(x, shift, scale, nw, win, qnw, kvnw, wq, wqr, cosk, sink):
    h = _rms(x, nw) * (1.0 + scale) + shift
    z = _bdot(h, win)
    q_lat = _rms(z[:, C_Q:C_KV], qnw)
    ckv = _rms(z[:, C_KV:C_KR], kvnw)
    krc = z[:, C_KR:C_GA]
    kr128 = krc * cosk + pltpu.roll(krc, HEAD_PAD - ROPE_DIM, axis=1) * sink
    ql = q_lat.astype(BF16)
    qa = jnp.dot(ql, wq, preferred_element_type=F32)
    qb = jnp.dot(ql, wqr, preferred_element_type=F32)
    return qa, qb, ckv, kr128, z[:, C_GA:C_U], z[:, C_U:C_GP], z[:, C_GP:C_END]


def _pool_out(pooled_g, g, wpool_ref, bpool_ref, pscale_ref, g_pool):
    lo, hi = g * POOL_GROUP, (g + 1) * POOL_GROUP
    y = _bdot(pooled_g, wpool_ref[g]) + bpool_ref[:, lo:hi]
    y = y * pscale_ref[:, lo:hi]
    return y * _silu(g_pool[:, lo:hi])


def _inproj_p_kernel(x_ref, mod_ref, nw_ref, win_ref, qnw_ref, kvnw_ref, wq_ref, wqr_ref,
                     wpool_ref, bpool_ref, pscale_ref, wk_ref, wvt_ref, cosk_ref, sink_ref, cosq_ref, sinq_ref,
                     q_ref, k_ref, vt_ref, ckv_ref, kr_ref, sg_ref, mp_ref, utail_ref, uext_ref, *, tm):
    i = pl.program_id(1)
    shift = mod_ref[0, 0:1, :]
    scale = mod_ref[0, 1:2, :]
    qa, qb, ckv, kr128, g_att, u, g_pool = _project(
        x_ref[0], shift, scale, nw_ref[...], win_ref[...], qnw_ref[...], kvnw_ref[...],
        wq_ref[...], wqr_ref[...], cosk_ref[...], sink_ref[...])
    cosq, sinq = cosq_ref[...] * QK_PRESCALE, sinq_ref[...] * QK_PRESCALE
    for h in range(N_HEADS):
        sl = slice(h * HEAD_PAD, (h + 1) * HEAD_PAD)
        q_ref[0, :, sl] = (qa[:, sl] * cosq + qb[:, sl] * sinq).astype(BF16)
    ckv_b = ckv.astype(BF16)
    k = jnp.dot(jnp.concatenate([ckv_b, kr128.astype(BF16)], axis=1), wk_ref[...], preferred_element_type=F32)
    k_ref[0] = k.astype(BF16)
    vt_ref[0] = lax.dot_general(wvt_ref[...], ckv_b, _NT, preferred_element_type=F32).astype(BF16)
    ckv_ref[0] = ckv
    kr_ref[0] = kr128[:, :ROPE_DIM]
    sg_ref[0] = _silu(g_att)

    @pl.when(i == 0)
    def _():
        uext_ref[0:HALO, :] = jnp.zeros((HALO, D_POOL), F32)

    uext_ref[HALO:HALO + tm, :] = u
    pos = (i * tm + lax.broadcasted_iota(jnp.int32, (tm, 1), 0)).astype(F32)
    for g, w in enumerate(POOL_WINDOWS):
        lo, hi = g * POOL_GROUP, (g + 1) * POOL_GROUP
        wsum = uext_ref[HALO:HALO + tm, lo:hi]
        for d in range(1, w):
            wsum = wsum + uext_ref[HALO - d:HALO - d + tm, lo:hi]
        cnt = jnp.minimum(pos + 1.0, float(w))
        pooled = wsum / cnt - u[:, lo:hi]
        mp_ref[0, :, lo:hi] = _pool_out(pooled, g, wpool_ref, bpool_ref, pscale_ref, g_pool).astype(BF16)
    uext_ref[0:HALO, :] = u[tm - HALO:tm, :]

    @pl.when(i == pl.num_programs(1) - 1)
    def _():
        utail_ref[0] = u[tm - HALO:tm, :]


def _full(shape):
    nd = len(shape)
    return pl.BlockSpec(shape, lambda *_: (0,) * nd)


def _inproj_p(x, mod3, wts, wk, wvt, tabs, tm=512):
    b, s, d = x.shape
    nt = s // tm
    row = lambda w: pl.BlockSpec((1, tm, w), lambda bi, i: (bi, i, 0))
    tab = pl.BlockSpec((tm, HEAD_PAD), lambda bi, i: (i, 0))
    in_specs = [row(d), pl.BlockSpec((1, 3, d), lambda bi, i: (bi, 0, 0))]
    in_specs += [_full(w.shape) for w in (*wts, wk, wvt)]
    in_specs += [tab, tab, tab, tab]
    out_shape = [
        jax.ShapeDtypeStruct((b, s, N_HEADS * HEAD_PAD), BF16),
        jax.ShapeDtypeStruct((b, s, N_HEADS * HEAD_PAD), BF16),
        jax.ShapeDtypeStruct((b, D_ATT, s), BF16),
        jax.ShapeDtypeStruct((b, s, KV_LORA), F32),
        jax.ShapeDtypeStruct((b, s, ROPE_DIM), F32),
        jax.ShapeDtypeStruct((b, s, D_ATT), F32),
        jax.ShapeDtypeStruct((b, s, D_POOL), BF16),
        jax.ShapeDtypeStruct((b, HALO, D_POOL), F32),
    ]
    out_specs = [row(N_HEADS * HEAD_PAD), row(N_HEADS * HEAD_PAD),
                 pl.BlockSpec((1, D_ATT, tm), lambda bi, i: (bi, 0, i)), row(KV_LORA), row(ROPE_DIM),
                 row(D_ATT), row(D_POOL), pl.BlockSpec((1, HALO, D_POOL), lambda bi, i: (bi, 0, 0))]
    return pl.pallas_call(
        functools.partial(_inproj_p_kernel, tm=tm),
        grid=(b, nt),
        in_specs=in_specs,
        out_specs=out_specs,
        out_shape=out_shape,
        scratch_shapes=[pltpu.VMEM((tm + HALO, D_POOL), F32)],
        compiler_params=pltpu.CompilerParams(dimension_semantics=("arbitrary", "arbitrary"),
                                             vmem_limit_bytes=VMEM_LIMIT),
        name="inproj_p",
    )(x, mod3, *wts, wk, wvt, *tabs)


def _inproj_s_kernel(x_ref, mod_ref, nw_ref, win_ref, qnw_ref, kvnw_ref, wq_ref, wqr_ref,
                     wpool_ref, bpool_ref, pscale_ref, cosk_ref, sink_ref, cosq_ref, sinq_ref,
                     state_ref, wukt_ref,
                     qabs_ref, qrope_ref, ckv_ref, kr_ref, sg_ref, mp_ref, u_ref):
    d = x_ref.shape[1]
    shift = mod_ref[:, 0:d]
    scale = mod_ref[:, d:2 * d]
    cosq, sinq = cosq_ref[...], sinq_ref[...]
    qa, qb, ckv, kr128, g_att, u, g_pool = _project(
        x_ref[...], shift, scale, nw_ref[...], win_ref[...], qnw_ref[...], kvnw_ref[...],
        wq_ref[...], wqr_ref[...], cosk_ref[...], sink_ref[...])
    for h in range(N_HEADS):
        sl = slice(h * HEAD_PAD, (h + 1) * HEAD_PAD)
        qrope_ref[:, sl] = qa[:, sl] * cosq + qb[:, sl] * sinq
        qabs_ref[:, sl] = _bdot(qa[:, sl], wukt_ref[h])
    ckv_ref[...] = ckv
    kr_ref[...] = kr128[:, :ROPE_DIM]
    sg_ref[...] = _silu(g_att)
    u_ref[...] = u
    for g, w in enumerate(POOL_WINDOWS):
        lo, hi = g * POOL_GROUP, (g + 1) * POOL_GROUP
        wsum = u[:, lo:hi]
        for j in range(1, w):
            wsum = wsum + state_ref[POOL_PREV - j, :, lo:hi]
        pooled = wsum / float(w) - u[:, lo:hi]
        mp_ref[:, lo:hi] = _pool_out(pooled, g, wpool_ref, bpool_ref, pscale_ref, g_pool).astype(BF16)


def _inproj_s(x, mod, wts, tabs, state, wukt):
    n, d = x.shape
    args = (x, mod, *wts, *tabs, state, wukt)
    out_shape = [
        jax.ShapeDtypeStruct((n, N_HEADS * HEAD_PAD), F32),
        jax.ShapeDtypeStruct((n, N_HEADS * HEAD_PAD), F32),
        jax.ShapeDtypeStruct((n, KV_LORA), F32),
        jax.ShapeDtypeStruct((n, ROPE_DIM), F32),
        jax.ShapeDtypeStruct((n, D_ATT), F32),
        jax.ShapeDtypeStruct((n, D_POOL), BF16),
        jax.ShapeDtypeStruct((n, D_POOL), F32),
    ]
    return pl.pallas_call(
        _inproj_s_kernel,
        in_specs=[_full(a.shape) for a in args],
        out_specs=[_full(o.shape) for o in out_shape],
        out_shape=out_shape,
        grid=(1,),
        compiler_params=pltpu.CompilerParams(dimension_semantics=("arbitrary",), vmem_limit_bytes=VMEM_LIMIT),
        name="inproj_s",
    )(*args)


def _flash_kernel(q_ref, k_ref, vt_ref, sg_ref, o_ref, st_ref, *, tq, tk):
    assert tq == tk
    qi = pl.program_id(2)
    qs = [q_ref[0, :, hh * HEAD_PAD:(hh + 1) * HEAD_PAD] for hh in range(2)]

    def produce(j, slot):
        for hh in range(2):
            kb = k_ref[0, pl.ds(pl.multiple_of(j * tk, tk), tk), hh * HEAD_PAD:(hh + 1) * HEAD_PAD]
            st_ref[slot, hh] = lax.dot_general(kb, qs[hh], _NT, preferred_element_type=F32)

    def consume(j, slot, states, masked):
        out = []
        for hh in range(2):
            m, l, acc = states[hh]
            st = st_ref[slot, hh]
            if masked:
                kpos = lax.broadcasted_iota(jnp.int32, (tk, tq), 0)
                qpos = lax.broadcasted_iota(jnp.int32, (tk, tq), 1)
                st = jnp.where(kpos <= qpos, st, -jnp.inf)
            vtb = vt_ref[0, hh * V_DIM:(hh + 1) * V_DIM, pl.ds(pl.multiple_of(j * tk, tk), tk)]
            m_new = jnp.maximum(m, jnp.max(st, axis=0, keepdims=True))
            alpha = jnp.exp2(m - m_new)
            pt = jnp.exp2(st - m_new)
            l_new = alpha * l + jnp.sum(pt, axis=0, keepdims=True)
            acc_new = alpha * acc + jnp.dot(vtb, pt.astype(BF16), preferred_element_type=F32)
            out.append((m_new, l_new, acc_new))
        return tuple(out)

    def finish(states):
        att_t = jnp.concatenate([acc / l for _, l, acc in states], axis=0)
        o_ref[0] = (att_t.T * sg_ref[0]).astype(BF16)

    def pair(t, states):
        j = 2 * t
        produce(j + 1, 1)
        states = consume(j, 0, states, False)
        produce(j + 2, 0)
        return consume(j + 1, 1, states, False)

    init = tuple((jnp.full((1, tq), -jnp.inf, F32), jnp.zeros((1, tq), F32), jnp.zeros((V_DIM, tq), F32))
                 for _ in range(2))
    produce(0, 0)
    states = lax.fori_loop(0, qi // 2, pair, init)

    @pl.when(qi % 2 == 0)
    def _():
        finish(consume(qi, 0, states, True))

    @pl.when(qi % 2 == 1)
    def _():
        produce(qi, 1)
        finish(consume(qi, 1, consume(qi - 1, 0, states, False), True))


def _flash(q, k, vt, sg, tq=512, tk=512):
    b, s, _ = q.shape
    n_pairs = N_HEADS // 2
    return pl.pallas_call(
        functools.partial(_flash_kernel, tq=tq, tk=tk),
        grid=(b, n_pairs, s // tq),
        in_specs=[pl.BlockSpec((1, tq, 2 * HEAD_PAD), lambda bi, hp, qi: (bi, qi, hp)),
                  pl.BlockSpec((1, s, 2 * HEAD_PAD), lambda bi, hp, qi: (bi, 0, hp)),
                  pl.BlockSpec((1, 2 * V_DIM, s), lambda bi, hp, qi: (bi, hp, 0)),
                  pl.BlockSpec((1, tq, 2 * V_DIM), lambda bi, hp, qi: (bi, qi, hp))],
        out_specs=pl.BlockSpec((1, tq, 2 * V_DIM), lambda bi, hp, qi: (bi, qi, hp)),
        out_shape=jax.ShapeDtypeStruct((b, s, D_ATT), BF16),
        scratch_shapes=[pltpu.VMEM((2, 2, tk, tq), F32)],
        compiler_params=pltpu.CompilerParams(dimension_semantics=("arbitrary", "arbitrary", "arbitrary"),
                                             vmem_limit_bytes=VMEM_LIMIT),
        name="flash",
    )(q, k, vt, sg)


def _decode_kernel(pt_ref, qa_ref, qr_ref, ckvn_ref, krn_ref, wv_ref, sg_ref, cckv_hbm, ckrt_hbm,
                   o_ref, ckv_buf, krt_buf, sem, *, n_pages):
    bi = pl.program_id(0)
    slot = bi % 2

    def page_copies(row, sl, p):
        page = pt_ref[row * n_pages + p]
        cols = pl.ds(pl.multiple_of(p * PAGE_SIZE, PAGE_SIZE), PAGE_SIZE)
        return (pltpu.make_async_copy(cckv_hbm.at[0, page], ckv_buf.at[sl, cols], sem.at[0, sl]),
                pltpu.make_async_copy(ckrt_hbm.at[0, page], krt_buf.at[sl, :, cols], sem.at[1, sl]))

    def start_row(row, sl):
        def body(p, _):
            for cp in page_copies(row, sl, p):
                cp.start()
            return 0
        lax.fori_loop(0, n_pages, body, 0, unroll=8)

    @pl.when(bi == 0)
    def _():
        start_row(0, 0)

    @pl.when(bi + 1 < pl.num_programs(0))
    def _():
        start_row(bi + 1, 1 - slot)

    def wait_body(p, _):
        for cp in page_copies(bi, slot, p):
            cp.wait()
        return 0
    lax.fori_loop(0, n_pages, wait_body, 0, unroll=8)

    qa = qa_ref[0].astype(BF16)
    qr = qr_ref[0].astype(BF16)
    ckv = ckv_buf[slot].astype(BF16)
    krt = krt_buf[slot].astype(BF16)
    s = (lax.dot_general(qa, ckv, _NT, preferred_element_type=F32)
         + jnp.dot(qr, krt, preferred_element_type=F32))
    ckvn = ckvn_ref[0].astype(BF16).astype(F32)
    krn = krn_ref[0].astype(BF16).astype(F32)
    s_n = (jnp.sum(qa.astype(F32) * ckvn, axis=1, keepdims=True)
           + jnp.sum(qr.astype(F32) * krn, axis=1, keepdims=True))
    m = jnp.maximum(jnp.max(s, axis=1, keepdims=True), s_n)
    p = jnp.exp2((s - m) * QK_PRESCALE)
    p_n = jnp.exp2((s_n - m) * QK_PRESCALE)
    l = jnp.sum(p, axis=1, keepdims=True) + p_n
    o_lat = (jnp.dot(p.astype(BF16), ckv, preferred_element_type=F32)
             + p_n.astype(BF16).astype(F32) * ckvn) / l
    o_all = _bdot(o_lat, wv_ref[...])
    hrow = lax.broadcasted_iota(jnp.int32, o_all.shape, 0)
    hcol = lax.broadcasted_iota(jnp.int32, o_all.shape, 1) // V_DIM
    att = jnp.sum(jnp.where(hrow == hcol, o_all, 0.0), axis=0, keepdims=True)
    o_ref[0] = (att * sg_ref[0]).astype(BF16)


def _decode(page_table, qabs, qrope, ckv_new, kr_new, wv, sg, cache_ckv, cache_krt):
    n, n_pages = page_table.shape
    past = n_pages * PAGE_SIZE
    per_b = lambda w: pl.BlockSpec((1, 1, w), lambda bi, pt: (bi, 0, 0))
    grid_spec = pltpu.PrefetchScalarGridSpec(
        num_scalar_prefetch=1,
        grid=(n,),
        in_specs=[pl.BlockSpec((1, N_HEADS, KV_LORA), lambda bi, pt: (bi, 0, 0)),
                  pl.BlockSpec((1, N_HEADS, ROPE_DIM), lambda bi, pt: (bi, 0, 0)),
                  per_b(KV_LORA), per_b(ROPE_DIM),
                  pl.BlockSpec(wv.shape, lambda bi, pt: (0, 0)),
                  per_b(D_ATT),
                  pl.BlockSpec(memory_space=pl.ANY),
                  pl.BlockSpec(memory_space=pl.ANY)],
        out_specs=per_b(D_ATT),
        scratch_shapes=[pltpu.VMEM((2, past, KV_LORA), F32),
                        pltpu.VMEM((2, ROPE_DIM, past), F32),
                        pltpu.SemaphoreType.DMA((2, 2))],
    )
    return pl.pallas_call(
        functools.partial(_decode_kernel, n_pages=n_pages),
        grid_spec=grid_spec,
        out_shape=jax.ShapeDtypeStruct((n, 1, D_ATT), BF16),
        compiler_params=pltpu.CompilerParams(dimension_semantics=("arbitrary",), vmem_limit_bytes=VMEM_LIMIT),
        name="decode",
    )(page_table.reshape(-1), qabs, qrope, ckv_new, kr_new, wv, sg, cache_ckv, cache_krt)


def _combine_kernel(x_ref, gate_ref, ag_ref, mp_ref, wo_ref, fnw_ref, y_ref):
    proj = (jnp.dot(ag_ref[0], wo_ref[0:D_ATT, :], preferred_element_type=F32)
            + jnp.dot(mp_ref[0], wo_ref[D_ATT:D_ATT + D_POOL, :], preferred_element_type=F32))
    y = x_ref[0] + gate_ref[0] * proj
    y_ref[0] = _rms(y, fnw_ref[...])


def _combine(x, gate, ag, mp, wo, fnw, tm):
    b, s, d = x.shape
    gr = gate.shape[1]
    gate_spec = (pl.BlockSpec((1, 1, d), lambda bi, i: (bi, 0, 0)) if gr == 1
                 else pl.BlockSpec((1, tm, d), lambda bi, i: (bi, i, 0)))
    row = lambda w: pl.BlockSpec((1, tm, w), lambda bi, i: (bi, i, 0))
    return pl.pallas_call(
        _combine_kernel,
        grid=(b, s // tm),
        in_specs=[row(d), gate_spec, row(D_ATT), row(D_POOL), _full(wo.shape), _full(fnw.shape)],
        out_specs=row(d),
        out_shape=jax.ShapeDtypeStruct((b, s, d), F32),
        compiler_params=pltpu.CompilerParams(dimension_semantics=("arbitrary", "arbitrary"),
                                             vmem_limit_bytes=VMEM_LIMIT),
        name="combine",
    )(x, gate, ag, mp, wo, fnw)


def _rope_tables(pos):
    inv = ROPE_BASE ** (-jnp.arange(0, ROPE_DIM, 2, dtype=F32) / ROPE_DIM)
    ang = pos.astype(F32)[:, None] * inv[None, :]
    cos, sin = jnp.cos(ang), jnp.sin(ang)
    t = pos.shape[0]
    cc, ss = jnp.concatenate([cos, cos], axis=1), jnp.concatenate([sin, sin], axis=1)
    zk = jnp.zeros((t, HEAD_PAD - ROPE_DIM), F32)
    zq = jnp.zeros((t, HEAD_PAD - QK_DIM), F32)
    cosk = jnp.concatenate([cc, zk], axis=1)
    sink = jnp.concatenate([ss, zk], axis=1)
    cosq = jnp.concatenate([jnp.ones((t, NOPE_DIM), F32), cc, zq], axis=1)
    sinq = jnp.concatenate([jnp.zeros((t, NOPE_DIM), F32), ss, zq], axis=1)
    return cosk, sink, cosq, sinq


def _rot_half(w):
    half = w.shape[-1] // 2
    return jnp.concatenate([-w[..., half:], w[..., :half]], axis=-1)


def _pack_weights(norm_w, w_in, q_norm_w, w_uq, kv_norm_w, w_uk, w_uv, w_pool, b_pool, pool_scale):
    d = w_in.shape[0]
    off_kr = Q_LORA + KV_LORA
    krc = w_in[:, off_kr:off_kr + ROPE_DIM]
    win = jnp.concatenate([w_in[:, :off_kr], krc, _rot_half(krc),
                           jnp.zeros((d, HEAD_PAD - 2 * ROPE_DIM), F32), w_in[:, off_kr + ROPE_DIM:]], axis=1)
    pad_q = HEAD_PAD - QK_DIM
    wq = jnp.pad(w_uq, ((0, 0), (0, 0), (0, pad_q))).reshape(Q_LORA, N_HEADS * HEAD_PAD)
    wqr = jnp.pad(_rot_half(w_uq[..., NOPE_DIM:]), ((0, 0), (0, 0), (NOPE_DIM, pad_q)))
    wqr = wqr.reshape(Q_LORA, N_HEADS * HEAD_PAD)
    wk_nope = jnp.pad(w_uk, ((0, 0), (0, 0), (0, HEAD_PAD - NOPE_DIM))).reshape(KV_LORA, N_HEADS * HEAD_PAD)
    place = jnp.pad(jnp.eye(ROPE_DIM, dtype=F32), ((0, HEAD_PAD - ROPE_DIM), (NOPE_DIM, pad_q)))
    wk = jnp.concatenate([wk_nope, jnp.tile(place, (1, N_HEADS))], axis=0)
    wv = w_uv.reshape(KV_LORA, N_HEADS * V_DIM).astype(BF16)
    shared = (norm_w.reshape(1, -1), win.astype(BF16), q_norm_w.reshape(1, -1), kv_norm_w.reshape(1, -1),
              wq.astype(BF16), wqr.astype(BF16), w_pool.astype(BF16), b_pool.reshape(1, -1),
              pool_scale.reshape(1, -1))
    return shared, wk.astype(BF16), wv


def kernel(x_prompt, x_sample, c_prompt, c_sample, cache_ckv, cache_krope, state_pool, page_table, ada_w, ada_b, norm_w, w_in, q_norm_w, w_uq, kv_norm_w, w_uk, w_uv, w_pool, b_pool, pool_scale, w_out, final_norm_w):
    assert ada_w.shape[0] == 1, "single-layer trunk only"
    b, s, d = x_prompt.shape
    n = x_sample.shape[0]
    assert x_sample.shape[1] == 1
    past_len = page_table.shape[1] * PAGE_SIZE

    c_all = jnp.concatenate([c_prompt, c_sample], axis=0)
    rows = -(-c_all.shape[0] // 8) * 8
    c_all = jnp.pad(c_all, ((0, rows - c_all.shape[0]), (0, 0)))
    mod = _ada(c_all, ada_w[0].astype(BF16), ada_b[0].reshape(1, -1))
    mod_p = mod[:b].reshape(b, 3, d)
    mod_s = mod[b:b + n]

    wts, wk, wv = _pack_weights(norm_w[0], w_in[0], q_norm_w[0], w_uq[0], kv_norm_w[0], w_uk[0], w_uv[0],
                                w_pool[0], b_pool[0], pool_scale[0])
    wo = w_out[0].astype(BF16)
    fnw = final_norm_w.reshape(1, -1)

    tabs_p = _rope_tables(jnp.arange(s))
    q, k, vt, ckv_p, kr_p, sg_p, mp_p, utail = _inproj_p(x_prompt, mod_p, wts, wk, wv.T, tabs_p)
    ag_p = _flash(q, k, vt, sg_p)
    y_prompt = _combine(x_prompt, mod_p[:, 2:3, :], ag_p, mp_p, wo, fnw, tm=512)

    tabs_s = tuple(jnp.broadcast_to(t, (n, HEAD_PAD)) for t in _rope_tables(jnp.full((1,), past_len)))
    wukt = jnp.pad(jnp.transpose(w_uk[0], (1, 2, 0)), ((0, 0), (0, HEAD_PAD - NOPE_DIM), (0, 0))).astype(BF16)
    state_t = jnp.transpose(state_pool[0], (1, 0, 2))
    qabs, qrope, ckv_s, kr_s, sg_s, mp_s, u_s = _inproj_s(x_sample[:, 0, :], mod_s, wts, tabs_s, state_t, wukt)
    qrope = qrope.reshape(n, N_HEADS, HEAD_PAD)[:, :, NOPE_DIM:QK_DIM]
    ag_s = _decode(page_table, qabs.reshape(n, N_HEADS, KV_LORA), qrope, ckv_s[:, None, :], kr_s[:, None, :],
                   wv, sg_s[:, None, :], cache_ckv, jnp.swapaxes(cache_krope, 2, 3))
    y_sample = _combine(x_sample.reshape(1, n, d), mod_s[None, :, 2 * d:], ag_s.reshape(1, n, D_ATT),
                        mp_s[None], wo, fnw, tm=n).reshape(n, 1, d)

    new_pool_p = utail[:, HALO - POOL_PREV:, :]
    new_pool_s = jnp.transpose(jnp.concatenate([state_t[1:], u_s[None]], axis=0), (1, 0, 2))
    return (y_prompt, y_sample,
            ckv_p[None], kr_p[None], new_pool_p[None],
            ckv_s[:, None, :][None], kr_s[:, None, :][None], new_pool_s[None])
```

```python
import functools
import math

import jax
import jax.numpy as jnp
from jax import lax
from jax.experimental import pallas as pl
from jax.experimental.pallas import tpu as pltpu

F32 = jnp.float32
BF16 = jnp.bfloat16

N_HEADS = 8
NOPE_DIM = 64
ROPE_DIM = 32
QK_DIM = NOPE_DIM + ROPE_DIM
V_DIM = 64
Q_LORA = 256
KV_LORA = 128
D_POOL = 512
D_ATT = 512
POOL_WINDOWS = (2, 4, 8, 16)
POOL_GROUP = 128
POOL_PREV = 15
PAGE_SIZE = 128
ROPE_BASE = 10000.0
EPS = 1e-6
SM_SCALE = QK_DIM ** -0.5
LOG2E = math.log2(math.e)
QK_PRESCALE = SM_SCALE * LOG2E
_NT = (((1,), (1,)), ((), ()))

HEAD_PAD = 128
SUM_ROWS = 16
HALO = 16
C_Q, C_KV, C_KR, C_GA, C_U, C_GP, C_END = 0, 256, 384, 512, 1024, 1536, 2048

VMEM_LIMIT = 56 * 1024 * 1024


def _rms(x, w):
    return (x * lax.rsqrt(jnp.mean(x * x, axis=-1, keepdims=True) + EPS)) * w


def _silu(x):
    return x * jax.nn.sigmoid(x)


def _bdot(a, b):
    return jnp.dot(a.astype(BF16), b, preferred_element_type=F32)


def _ada_kernel(c_ref, w_ref, b_ref, o_ref):
    o_ref[...] = _bdot(_silu(c_ref[...]), w_ref[...]) + b_ref[...]


def _ada(c_all, ada_w, ada_b):
    rows, d = c_all.shape
    n = ada_w.shape[1]
    bn = 1024
    return pl.pallas_call(
        _ada_kernel,
        grid=(n // bn,),
        in_specs=[pl.BlockSpec((rows, d), lambda j: (0, 0)),
                  pl.BlockSpec((d, bn), lambda j: (0, j)),
                  pl.BlockSpec((1, bn), lambda j: (0, j))],
        out_specs=pl.BlockSpec((rows, bn), lambda j: (0, j)),
        out_shape=jax.ShapeDtypeStruct((rows, n), F32),
        compiler_params=pltpu.CompilerParams(dimension_semantics=("arbitrary",)),
        name="ada",
    )(c_all, ada_w, ada_b)


def _project(x, shift, scale, nw, win, qnw, kvnw, wq, wqr, cosk, sink):
    h = _rms(x, nw) * (1.0 + scale) + shift
    z = _bdot(h, win)
    q_lat = _rms(z[:, C_Q:C_KV], qnw)
    ckv = _rms(z[:, C_KV:C_KR], kvnw)
    krc = z[:, C_KR:C_GA]
    kr128 = krc * cosk + pltpu.roll(krc, HEAD_PAD - ROPE_DIM, axis=1) * sink
    ql = q_lat.astype(BF16)
    qa = jnp.dot(ql, wq, preferred_element_type=F32)
    qb = jnp.dot(ql, wqr, preferred_element_type=F32)
    return qa, qb, ckv, kr128, z[:, C_GA:C_U], z[:, C_U:C_GP], z[:, C_GP:C_END]


def _pool_out(pooled_g, g, wpool_ref, bpool_ref, pscale_ref, g_pool):
    lo, hi = g * POOL_GROUP, (g + 1) * POOL_GROUP
    y = _bdot(pooled_g, wpool_ref[g]) + bpool_ref[:, lo:hi]
    y = y * pscale_ref[:, lo:hi]
    return y * _silu(g_pool[:, lo:hi])


def _inproj_p_kernel(x_ref, mod_ref, nw_ref, win_ref, qnw_ref, kvnw_ref, wq_ref, wqr_ref,
                     wpool_ref, bpool_ref, pscale_ref, wk_ref, wvt_ref, cosk_ref, sink_ref, cosq_ref, sinq_ref,
                     q_ref, k_ref, vt_ref, ckv_ref, kr_ref, sg_ref, mp_ref, utail_ref, uext_ref, *, tm):
    i = pl.program_id(1)
    shift = mod_ref[0, 0:1, :]
    scale = mod_ref[0, 1:2, :]
    qa, qb, ckv, kr128, g_att, u, g_pool = _project(
        x_ref[0], shift, scale, nw_ref[...], win_ref[...], qnw_ref[...], kvnw_ref[...],
        wq_ref[...], wqr_ref[...], cosk_ref[...], sink_ref[...])
    cosq, sinq = cosq_ref[...] * QK_PRESCALE, sinq_ref[...] * QK_PRESCALE
    for h in range(N_HEADS):
        sl = slice(h * HEAD_PAD, (h + 1) * HEAD_PAD)
        q_ref[0, :, sl] = (qa[:, sl] * cosq + qb[:, sl] * sinq).astype(BF16)
    ckv_b = ckv.astype(BF16)
    k = jnp.dot(jnp.concatenate([ckv_b, kr128.astype(BF16)], axis=1), wk_ref[...], preferred_element_type=F32)
    k_ref[0] = k.astype(BF16)
    vt_ref[0] = lax.dot_general(wvt_ref[...], ckv_b, _NT, preferred_element_type=F32).astype(BF16)
    ckv_ref[0] = ckv
    kr_ref[0] = kr128[:, :ROPE_DIM]
    sg_ref[0] = _silu(g_att)

    @pl.when(i == 0)
    def _():
        uext_ref[0:HALO, :] = jnp.zeros((HALO, D_POOL), F32)

    uext_ref[HALO:HALO + tm, :] = u
    pos = (i * tm + lax.broadcasted_iota(jnp.int32, (tm, 1), 0)).astype(F32)
    for g, w in enumerate(POOL_WINDOWS):
        lo, hi = g * POOL_GROUP, (g + 1) * POOL_GROUP
        wsum = uext_ref[HALO:HALO + tm, lo:hi]
        for d in range(1, w):
            wsum = wsum + uext_ref[HALO - d:HALO - d + tm, lo:hi]
        cnt = jnp.minimum(pos + 1.0, float(w))
        pooled = wsum / cnt - u[:, lo:hi]
        mp_ref[0, :, lo:hi] = _pool_out(pooled, g, wpool_ref, bpool_ref, pscale_ref, g_pool).astype(BF16)
    uext_ref[0:HALO, :] = u[tm - HALO:tm, :]

    @pl.when(i == pl.num_programs(1) - 1)
    def _():
        utail_ref[0] = u[tm - HALO:tm, :]


def _full(shape):
    nd = len(shape)
    return pl.BlockSpec(shape, lambda *_: (0,) * nd)


def _inproj_p(x, mod3, wts, wk, wvt, tabs, tm=512):
    b, s, d = x.shape
    nt = s // tm
    row = lambda w: pl.BlockSpec((1, tm, w), lambda bi, i: (bi, i, 0))
    tab = pl.BlockSpec((tm, HEAD_PAD), lambda bi, i: (i, 0))
    in_specs = [row(d), pl.BlockSpec((1, 3, d), lambda bi, i: (bi, 0, 0))]
    in_specs += [_full(w.shape) for w in (*wts, wk, wvt)]
    in_specs += [tab, tab, tab, tab]
    out_shape = [
        jax.ShapeDtypeStruct((b, s, N_HEADS * HEAD_PAD), BF16),
        jax.ShapeDtypeStruct((b, s, N_HEADS * HEAD_PAD), BF16),
        jax.ShapeDtypeStruct((b, D_ATT, s), BF16),
        jax.ShapeDtypeStruct((b, s, KV_LORA), F32),
        jax.ShapeDtypeStruct((b, s, ROPE_DIM), F32),
        jax.ShapeDtypeStruct((b, s, D_ATT), F32),
        jax.ShapeDtypeStruct((b, s, D_POOL), BF16),
        jax.ShapeDtypeStruct((b, HALO, D_POOL), F32),
    ]
    out_specs = [row(N_HEADS * HEAD_PAD), row(N_HEADS * HEAD_PAD),
                 pl.BlockSpec((1, D_ATT, tm), lambda bi, i: (bi, 0, i)), row(KV_LORA), row(ROPE_DIM),
                 row(D_ATT), row(D_POOL), pl.BlockSpec((1, HALO, D_POOL), lambda bi, i: (bi, 0, 0))]
    return pl.pallas_call(
        functools.partial(_inproj_p_kernel, tm=tm),
        grid=(b, nt),
        in_specs=in_specs,
        out_specs=out_specs,
        out_shape=out_shape,
        scratch_shapes=[pltpu.VMEM((tm + HALO, D_POOL), F32)],
        compiler_params=pltpu.CompilerParams(dimension_semantics=("arbitrary", "arbitrary"),
                                             vmem_limit_bytes=VMEM_LIMIT),
        name="inproj_p",
    )(x, mod3, *wts, wk, wvt, *tabs)


def _inproj_s_kernel(x_ref, mod_ref, nw_ref, win_ref, qnw_ref, kvnw_ref, wq_ref, wqr_ref,
                     wpool_ref, bpool_ref, pscale_ref, cosk_ref, sink_ref, cosq_ref, sinq_ref,
                     state_ref, wukt_ref,
                     qabs_ref, qrope_ref, ckv_ref, kr_ref, sg_ref, mp_ref, u_ref):
    d = x_ref.shape[1]
    shift = mod_ref[:, 0:d]
    scale = mod_ref[:, d:2 * d]
    cosq, sinq = cosq_ref[...], sinq_ref[...]
    qa, qb, ckv, kr128, g_att, u, g_pool = _project(
        x_ref[...], shift, scale, nw_ref[...], win_ref[...], qnw_ref[...], kvnw_ref[...],
        wq_ref[...], wqr_ref[...], cosk_ref[...], sink_ref[...])
    for h in range(N_HEADS):
        sl = slice(h * HEAD_PAD, (h + 1) * HEAD_PAD)
        qrope_ref[:, sl] = qa[:, sl] * cosq + qb[:, sl] * sinq
        qabs_ref[:, sl] = _bdot(qa[:, sl], wukt_ref[h])
    ckv_ref[...] = ckv
    kr_ref[...] = kr128[:, :ROPE_DIM]
    sg_ref[...] = _silu(g_att)
    u_ref[...] = u
    for g, w in enumerate(POOL_WINDOWS):
        lo, hi = g * POOL_GROUP, (g + 1) * POOL_GROUP
        wsum = u[:, lo:hi]
        for j in range(1, w):
            wsum = wsum + state_ref[POOL_PREV - j, :, lo:hi]
        pooled = wsum / float(w) - u[:, lo:hi]
        mp_ref[:, lo:hi] = _pool_out(pooled, g, wpool_ref, bpool_ref, pscale_ref, g_pool).astype(BF16)


def _inproj_s(x, mod, wts, tabs, state, wukt):
    n, d = x.shape
    args = (x, mod, *wts, *tabs, state, wukt)
    out_shape = [
        jax.ShapeDtypeStruct((n, N_HEADS * HEAD_PAD), F32),
        jax.ShapeDtypeStruct((n, N_HEADS * HEAD_PAD), F32),
        jax.ShapeDtypeStruct((n, KV_LORA), F32),
        jax.ShapeDtypeStruct((n, ROPE_DIM), F32),
        jax.ShapeDtypeStruct((n, D_ATT), F32),
        jax.ShapeDtypeStruct((n, D_POOL), BF16),
        jax.ShapeDtypeStruct((n, D_POOL), F32),
    ]
    return pl.pallas_call(
        _inproj_s_kernel,
        in_specs=[_full(a.shape) for a in args],
        out_specs=[_full(o.shape) for o in out_shape],
        out_shape=out_shape,
        grid=(1,),
        compiler_params=pltpu.CompilerParams(dimension_semantics=("arbitrary",), vmem_limit_bytes=VMEM_LIMIT),
        name="inproj_s",
    )(*args)


def _flash_tile(qi, q_ref, k_ref, vt_ref, sg_ref, o_ref, st_ref, *, tq, tk):
    assert tq == tk
    qs = [q_ref[0, :, hh * HEAD_PAD:(hh + 1) * HEAD_PAD] for hh in range(2)]

    def produce(j, slot):
        for hh in range(2):
            kb = k_ref[0, pl.ds(pl.multiple_of(j * tk, tk), tk), hh * HEAD_PAD:(hh + 1) * HEAD_PAD]
            st_ref[slot][hh][...] = lax.dot_general(kb, qs[hh], _NT, preferred_element_type=F32)

    def consume(j, slot, states, masked):
        out = []
        for hh in range(2):
            m, acc = states[hh]
            st = st_ref[slot][hh][...]
            if masked:
                kpos = lax.broadcasted_iota(jnp.int32, (tk, tq), 0)
                qpos = lax.broadcasted_iota(jnp.int32, (tk, tq), 1)
                st = jnp.where(kpos <= qpos, st, -jnp.inf)
            vtb = vt_ref[0, hh * V_DIM:(hh + 1) * V_DIM, pl.ds(pl.multiple_of(j * tk, tk), tk)]
            vtb = jnp.concatenate([vtb, jnp.ones((SUM_ROWS, tk), BF16)], axis=0)
            m_new = jnp.maximum(m, jnp.max(st, axis=0, keepdims=True))
            alpha = jnp.exp2(m - m_new)
            pt = jnp.exp2(st - m_new).astype(BF16)
            acc_new = alpha * acc + jnp.dot(vtb, pt, preferred_element_type=F32)
            out.append((m_new, acc_new))
        return tuple(out)

    def finish(states):
        att_t = jnp.concatenate([acc[:V_DIM] / acc[V_DIM:V_DIM + 1] for _, acc in states], axis=0)
        o_ref[0] = (att_t.T * sg_ref[0]).astype(BF16)

    def pair(t, states):
        j = 2 * t
        produce(j + 1, 1)
        states = consume(j, 0, states, False)
        produce(j + 2, 0)
        return consume(j + 1, 1, states, False)

    init = tuple((jnp.full((1, tq), -jnp.inf, F32), jnp.zeros((V_DIM + SUM_ROWS, tq), F32)) for _ in range(2))
    produce(0, 0)
    states = lax.fori_loop(0, qi // 2, pair, init)

    @pl.when(qi % 2 == 0)
    def _():
        finish(consume(qi, 0, states, True))

    @pl.when(qi % 2 == 1)
    def _():
        produce(qi, 1)
        finish(consume(qi, 1, consume(qi - 1, 0, states, False), True))


def _page_copies(pt_ref, cckv_hbm, ckrt_hbm, ckv_buf, krt_buf, sem, n_pages, row, sl, p):
    page = pt_ref[row * n_pages + p]
    cols = pl.ds(pl.multiple_of(p * PAGE_SIZE, PAGE_SIZE), PAGE_SIZE)
    return (pltpu.make_async_copy(cckv_hbm.at[0, page], ckv_buf.at[sl, cols], sem.at[0, sl]),
            pltpu.make_async_copy(ckrt_hbm.at[0, page], krt_buf.at[sl, :, cols], sem.at[1, sl]))


def _decode_row(pt_ref, qa_ref, qr_ref, ckvn_ref, krn_ref, wv_ref, sg_ref, cckv_hbm, ckrt_hbm,
                o_ref, ckv_buf, krt_buf, sem, *, row, slot, n_pages):
    def wait_body(p, _):
        for cp in _page_copies(pt_ref, cckv_hbm, ckrt_hbm, ckv_buf, krt_buf, sem, n_pages, row, slot, p):
            cp.wait()
        return 0
    lax.fori_loop(0, n_pages, wait_body, 0, unroll=8)

    qa = qa_ref[0].astype(BF16)
    qr = qr_ref[0].astype(BF16)
    ckv = ckv_buf[slot].astype(BF16)
    krt = krt_buf[slot].astype(BF16)
    s = (lax.dot_general(qa, ckv, _NT, preferred_element_type=F32)
         + jnp.dot(qr, krt, preferred_element_type=F32))
    ckvn = ckvn_ref[0].astype(BF16).astype(F32)
    krn = krn_ref[0].astype(BF16).astype(F32)
    s_n = (jnp.sum(qa.astype(F32) * ckvn, axis=1, keepdims=True)
           + jnp.sum(qr.astype(F32) * krn, axis=1, keepdims=True))
    m = jnp.maximum(jnp.max(s, axis=1, keepdims=True), s_n)
    p = jnp.exp2((s - m) * QK_PRESCALE)
    p_n = jnp.exp2((s_n - m) * QK_PRESCALE)
    l = jnp.sum(p, axis=1, keepdims=True) + p_n
    o_lat = (jnp.dot(p.astype(BF16), ckv, preferred_element_type=F32)
             + p_n.astype(BF16).astype(F32) * ckvn) / l
    o_all = _bdot(o_lat, wv_ref[...])
    hrow = lax.broadcasted_iota(jnp.int32, o_all.shape, 0)
    hcol = lax.broadcasted_iota(jnp.int32, o_all.shape, 1) // V_DIM
    att = jnp.sum(jnp.where(hrow == hcol, o_all, 0.0), axis=0, keepdims=True)
    o_ref[0] = (att * sg_ref[0]).astype(BF16)


def _attn_kernel(pt_ref, q_ref, k_ref, vt_ref, sgp_ref, qa_ref, qr_ref, ckvn_ref, krn_ref, wv_ref, sgs_ref,
                 cckv_hbm, ckrt_hbm, op_ref, os_ref, st00, st01, st10, st11, ckv_buf, krt_buf, sem,
                 *, tq, tk, n_pages):
    qi = pl.program_id(2)
    r = (pl.program_id(0) * pl.num_programs(1) + pl.program_id(1)) * pl.num_programs(2) + qi
    n_rows = pl.num_programs(0) * pl.num_programs(1) * pl.num_programs(2)
    slot = r % 2
    dma = (pt_ref, cckv_hbm, ckrt_hbm, ckv_buf, krt_buf, sem, n_pages)

    def start_row(row, sl):
        def body(p, _):
            for cp in _page_copies(*dma, row, sl, p):
                cp.start()
            return 0
        lax.fori_loop(0, n_pages, body, 0, unroll=8)

    @pl.when(r == 0)
    def _():
        start_row(0, 0)

    @pl.when(r + 1 < n_rows)
    def _():
        start_row(r + 1, 1 - slot)

    _flash_tile(qi, q_ref, k_ref, vt_ref, sgp_ref, op_ref, ((st00, st01), (st10, st11)), tq=tq, tk=tk)
    _decode_row(pt_ref, qa_ref, qr_ref, ckvn_ref, krn_ref, wv_ref, sgs_ref, cckv_hbm, ckrt_hbm,
                os_ref, ckv_buf, krt_buf, sem, row=r, slot=slot, n_pages=n_pages)


def _attention(q, k, vt, sg_p, page_table, qabs, qrope, ckv_new, kr_new, wv, sg_s, cache_ckv, cache_krt,
               tq=512, tk=512):
    b, s, _ = q.shape
    n_pairs = N_HEADS // 2
    nq = s // tq
    n, n_pages = page_table.shape
    assert n == b * n_pairs * nq, "one sample row per prompt attention step"
    past = n_pages * PAGE_SIZE
    row = lambda bi, hp, qi: (bi * n_pairs + hp) * nq + qi
    per_r = lambda w: pl.BlockSpec((1, 1, w), lambda bi, hp, qi, pt: (row(bi, hp, qi), 0, 0))
    per_h = lambda w: pl.BlockSpec((1, N_HEADS, w), lambda bi, hp, qi, pt: (row(bi, hp, qi), 0, 0))
    grid_spec = pltpu.PrefetchScalarGridSpec(
        num_scalar_prefetch=1,
        grid=(b, n_pairs, nq),
        in_specs=[pl.BlockSpec((1, tq, 2 * HEAD_PAD), lambda bi, hp, qi, pt: (bi, qi, hp)),
                  pl.BlockSpec((1, s, 2 * HEAD_PAD), lambda bi, hp, qi, pt: (bi, 0, hp)),
                  pl.BlockSpec((1, 2 * V_DIM, s), lambda bi, hp, qi, pt: (bi, hp, 0)),
                  pl.BlockSpec((1, tq, 2 * V_DIM), lambda bi, hp, qi, pt: (bi, qi, hp)),
                  per_h(KV_LORA), per_h(ROPE_DIM), per_r(KV_LORA), per_r(ROPE_DIM),
                  pl.BlockSpec(wv.shape, lambda bi, hp, qi, pt: (0, 0)),
                  per_r(D_ATT),
                  pl.BlockSpec(memory_space=pl.ANY),
                  pl.BlockSpec(memory_space=pl.ANY)],
        out_specs=[pl.BlockSpec((1, tq, 2 * V_DIM), lambda bi, hp, qi, pt: (bi, qi, hp)), per_r(D_ATT)],
        scratch_shapes=[pltpu.VMEM((tk, tq), F32)] * 4 + [
            pltpu.VMEM((2, past, KV_LORA), F32),
            pltpu.VMEM((2, ROPE_DIM, past), F32),
            pltpu.SemaphoreType.DMA((2, 2))],
    )
    return pl.pallas_call(
        functools.partial(_attn_kernel, tq=tq, tk=tk, n_pages=n_pages),
        grid_spec=grid_spec,
        out_shape=[jax.ShapeDtypeStruct((b, s, D_ATT), BF16), jax.ShapeDtypeStruct((n, 1, D_ATT), BF16)],
        compiler_params=pltpu.CompilerParams(dimension_semantics=("arbitrary", "arbitrary", "arbitrary"),
                                             vmem_limit_bytes=VMEM_LIMIT),
        name="attention",
    )(page_table.reshape(-1), q, k, vt, sg_p, qabs, qrope, ckv_new, kr_new, wv, sg_s, cache_ckv, cache_krt)


def _combine_kernel(x_ref, gate_ref, ag_ref, mp_ref, wo_ref, fnw_ref, y_ref):
    proj = (jnp.dot(ag_ref[0], wo_ref[0:D_ATT, :], preferred_element_type=F32)
            + jnp.dot(mp_ref[0], wo_ref[D_ATT:D_ATT + D_POOL, :], preferred_element_type=F32))
    y = x_ref[0] + gate_ref[0] * proj
    y_ref[0] = _rms(y, fnw_ref[...])


def _combine(x, gate, ag, mp, wo, fnw, tm):
    b, s, d = x.shape
    gr = gate.shape[1]
    gate_spec = (pl.BlockSpec((1, 1, d), lambda bi, i: (bi, 0, 0)) if gr == 1
                 else pl.BlockSpec((1, tm, d), lambda bi, i: (bi, i, 0)))
    row = lambda w: pl.BlockSpec((1, tm, w), lambda bi, i: (bi, i, 0))
    return pl.pallas_call(
        _combine_kernel,
        grid=(b, s // tm),
        in_specs=[row(d), gate_spec, row(D_ATT), row(D_POOL), _full(wo.shape), _full(fnw.shape)],
        out_specs=row(d),
        out_shape=jax.ShapeDtypeStruct((b, s, d), F32),
        compiler_params=pltpu.CompilerParams(dimension_semantics=("arbitrary", "arbitrary"),
                                             vmem_limit_bytes=VMEM_LIMIT),
        name="combine",
    )(x, gate, ag, mp, wo, fnw)


def _rope_tables(pos):
    inv = ROPE_BASE ** (-jnp.arange(0, ROPE_DIM, 2, dtype=F32) / ROPE_DIM)
    ang = pos.astype(F32)[:, None] * inv[None, :]
    cos, sin = jnp.cos(ang), jnp.sin(ang)
    t = pos.shape[0]
    cc, ss = jnp.concatenate([cos, cos], axis=1), jnp.concatenate([sin, sin], axis=1)
    zk = jnp.zeros((t, HEAD_PAD - ROPE_DIM), F32)
    zq = jnp.zeros((t, HEAD_PAD - QK_DIM), F32)
    cosk = jnp.concatenate([cc, zk], axis=1)
    sink = jnp.concatenate([ss, zk], axis=1)
    cosq = jnp.concatenate([jnp.ones((t, NOPE_DIM), F32), cc, zq], axis=1)
    sinq = jnp.concatenate([jnp.zeros((t, NOPE_DIM), F32), ss, zq], axis=1)
    return cosk, sink, cosq, sinq


def _rot_half(w):
    half = w.shape[-1] // 2
    return jnp.concatenate([-w[..., half:], w[..., :half]], axis=-1)


def _pack_weights(norm_w, w_in, q_norm_w, w_uq, kv_norm_w, w_uk, w_uv, w_pool, b_pool, pool_scale):
    d = w_in.shape[0]
    off_kr = Q_LORA + KV_LORA
    krc = w_in[:, off_kr:off_kr + ROPE_DIM]
    win = jnp.concatenate([w_in[:, :off_kr], krc, _rot_half(krc),
                           jnp.zeros((d, HEAD_PAD - 2 * ROPE_DIM), F32), w_in[:, off_kr + ROPE_DIM:]], axis=1)
    pad_q = HEAD_PAD - QK_DIM
    wq = jnp.pad(w_uq, ((0, 0), (0, 0), (0, pad_q))).reshape(Q_LORA, N_HEADS * HEAD_PAD)
    wqr = jnp.pad(_rot_half(w_uq[..., NOPE_DIM:]), ((0, 0), (0, 0), (NOPE_DIM, pad_q)))
    wqr = wqr.reshape(Q_LORA, N_HEADS * HEAD_PAD)
    wk_nope = jnp.pad(w_uk, ((0, 0), (0, 0), (0, HEAD_PAD - NOPE_DIM))).reshape(KV_LORA, N_HEADS * HEAD_PAD)
    place = jnp.pad(jnp.eye(ROPE_DIM, dtype=F32), ((0, HEAD_PAD - ROPE_DIM), (NOPE_DIM, pad_q)))
    wk = jnp.concatenate([wk_nope, jnp.tile(place, (1, N_HEADS))], axis=0)
    wv = w_uv.reshape(KV_LORA, N_HEADS * V_DIM).astype(BF16)
    shared = (norm_w.reshape(1, -1), win.astype(BF16), q_norm_w.reshape(1, -1), kv_norm_w.reshape(1, -1),
              wq.astype(BF16), wqr.astype(BF16), w_pool.astype(BF16), b_pool.reshape(1, -1),
              pool_scale.reshape(1, -1))
    return shared, wk.astype(BF16), wv


def kernel(x_prompt, x_sample, c_prompt, c_sample, cache_ckv, cache_krope, state_pool, page_table, ada_w, ada_b, norm_w, w_in, q_norm_w, w_uq, kv_norm_w, w_uk, w_uv, w_pool, b_pool, pool_scale, w_out, final_norm_w):
    assert ada_w.shape[0] == 1, "single-layer trunk only"
    b, s, d = x_prompt.shape
    n = x_sample.shape[0]
    assert x_sample.shape[1] == 1
    past_len = page_table.shape[1] * PAGE_SIZE

    c_all = jnp.concatenate([c_prompt, c_sample], axis=0)
    rows = -(-c_all.shape[0] // 8) * 8
    c_all = jnp.pad(c_all, ((0, rows - c_all.shape[0]), (0, 0)))
    mod = _ada(c_all, ada_w[0].astype(BF16), ada_b[0].reshape(1, -1))
    mod_p = mod[:b].reshape(b, 3, d)
    mod_s = mod[b:b + n]

    wts, wk, wv = _pack_weights(norm_w[0], w_in[0], q_norm_w[0], w_uq[0], kv_norm_w[0], w_uk[0], w_uv[0],
                                w_pool[0], b_pool[0], pool_scale[0])
    wo = w_out[0].astype(BF16)
    fnw = final_norm_w.reshape(1, -1)

    tabs_p = _rope_tables(jnp.arange(s))
    q, k, vt, ckv_p, kr_p, sg_p, mp_p, utail = _inproj_p(x_prompt, mod_p, wts, wk, wv.T, tabs_p)

    tabs_s = tuple(jnp.broadcast_to(t, (n, HEAD_PAD)) for t in _rope_tables(jnp.full((1,), past_len)))
    wukt = jnp.pad(jnp.transpose(w_uk[0], (1, 2, 0)), ((0, 0), (0, HEAD_PAD - NOPE_DIM), (0, 0))).astype(BF16)
    state_t = jnp.transpose(state_pool[0], (1, 0, 2))
    qabs, qrope, ckv_s, kr_s, sg_s, mp_s, u_s = _inproj_s(x_sample[:, 0, :], mod_s, wts, tabs_s, state_t, wukt)
    qrope = qrope.reshape(n, N_HEADS, HEAD_PAD)[:, :, NOPE_DIM:QK_DIM]
    ag_p, ag_s = _attention(q, k, vt, sg_p, page_table, qabs.reshape(n, N_HEADS, KV_LORA), qrope,
                            ckv_s[:, None, :], kr_s[:, None, :], wv, sg_s[:, None, :],
                            cache_ckv, jnp.swapaxes(cache_krope, 2, 3))
    y_prompt = _combine(x_prompt, mod_p[:, 2:3, :], ag_p, mp_p, wo, fnw, tm=512)
    y_sample = _combine(x_sample.reshape(1, n, d), mod_s[None, :, 2 * d:], ag_s.reshape(1, n, D_ATT),
                        mp_s[None], wo, fnw, tm=n).reshape(n, 1, d)

    new_pool_p = utail[:, HALO - POOL_PREV:, :]
    new_pool_s = jnp.transpose(jnp.concatenate([state_t[1:], u_s[None]], axis=0), (1, 0, 2))
    return (y_prompt, y_sample,
            ckv_p[None], kr_p[None], new_pool_p[None],
            ckv_s[:, None, :][None], kr_s[:, None, :][None], new_pool_s[None])
```

```python
import functools
import math

import jax
import jax.numpy as jnp
from jax import lax
from jax.experimental import pallas as pl
from jax.experimental.pallas import tpu as pltpu

F32 = jnp.float32
BF16 = jnp.bfloat16

N_HEADS = 8
NOPE_DIM = 64
ROPE_DIM = 32
QK_DIM = NOPE_DIM + ROPE_DIM
V_DIM = 64
Q_LORA = 256
KV_LORA = 128
D_POOL = 512
D_ATT = 512
POOL_WINDOWS = (2, 4, 8, 16)
POOL_GROUP = 128
POOL_PREV = 15
PAGE_SIZE = 128
ROPE_BASE = 10000.0
EPS = 1e-6
SM_SCALE = QK_DIM ** -0.5
LOG2E = math.log2(math.e)
QK_PRESCALE = SM_SCALE * LOG2E
_NT = (((1,), (1,)), ((), ()))

HEAD_PAD = 128
FIRST_CHUNKS = 8
SUM_ROWS = 16
HALO = 16
C_Q, C_KV, C_KR, C_GA, C_U, C_GP, C_END = 0, 256, 384, 512, 1024, 1536, 2048

ROW_TILE = 512
VMEM_LIMIT = 56 * 1024 * 1024


def _rms(x, w):
    return (x * lax.rsqrt(jnp.mean(x * x, axis=-1, keepdims=True) + EPS)) * w


def _silu(x):
    return x * jax.nn.sigmoid(x)


def _bdot(a, b):
    return jnp.dot(a.astype(BF16), b, preferred_element_type=F32)


def _ada_kernel(c_ref, w_ref, b_ref, o_ref):
    o_ref[...] = _bdot(_silu(c_ref[...]), w_ref[...]) + b_ref[...]


def _ada(c_all, ada_w, ada_b):
    rows, d = c_all.shape
    n = ada_w.shape[1]
    bn = 1024
    return pl.pallas_call(
        _ada_kernel,
        grid=(n // bn,),
        in_specs=[pl.BlockSpec((rows, d), lambda j: (0, 0)),
                  pl.BlockSpec((d, bn), lambda j: (0, j)),
                  pl.BlockSpec((1, bn), lambda j: (0, j))],
        out_specs=pl.BlockSpec((rows, bn), lambda j: (0, j)),
        out_shape=jax.ShapeDtypeStruct((rows, n), F32),
        compiler_params=pltpu.CompilerParams(dimension_semantics=("arbitrary",)),
        name="ada",
    )(c_all, ada_w, ada_b)


def _project(x, shift, scale, nw, win, qnw, kvnw, wq, wqr, cr, sr, cb, sb):
    cosk = cb * cr - sb * sr
    sink = sb * cr + cb * sr
    nope = (lax.broadcasted_iota(jnp.int32, (1, HEAD_PAD), 1) < NOPE_DIM).astype(F32)
    cosq = pltpu.roll(cosk, NOPE_DIM, axis=1) + nope
    sinq = pltpu.roll(sink, NOPE_DIM, axis=1)
    h = _rms(x, nw) * (1.0 + scale) + shift
    z = _bdot(h, win)
    q_lat = _rms(z[:, C_Q:C_KV], qnw)
    ckv = _rms(z[:, C_KV:C_KR], kvnw)
    krc = z[:, C_KR:C_GA]
    kr128 = krc * cosk + pltpu.roll(krc, HEAD_PAD - ROPE_DIM, axis=1) * sink
    ql = q_lat.astype(BF16)
    qa = jnp.dot(ql, wq, preferred_element_type=F32)
    qb = jnp.dot(ql, wqr, preferred_element_type=F32)
    return qa, qb, ckv, kr128, z[:, C_GA:C_U], z[:, C_U:C_GP], z[:, C_GP:C_END], cosq, sinq


def _pool_out(pooled_g, g, wpool_ref, bpool_ref, pscale_ref, g_pool):
    lo, hi = g * POOL_GROUP, (g + 1) * POOL_GROUP
    y = _bdot(pooled_g, wpool_ref[g]) + bpool_ref[:, lo:hi]
    y = y * pscale_ref[:, lo:hi]
    return y * _silu(g_pool[:, lo:hi])


def _inproj_p_kernel(x_ref, mod_ref, nw_ref, win_ref, qnw_ref, kvnw_ref, wq_ref, wqr_ref,
                     wpool_ref, bpool_ref, pscale_ref, wk_ref, wvt_ref, cr_ref, sr_ref, cb_ref, sb_ref,
                     q_ref, k_ref, vt_ref, ckv_ref, kr_ref, sg_ref, mp_ref, utail_ref, uext_ref, *, tm):
    i = pl.program_id(1)
    shift = mod_ref[0, 0:1, :]
    scale = mod_ref[0, 1:2, :]
    qa, qb, ckv, kr128, g_att, u, g_pool, cosq, sinq = _project(
        x_ref[0], shift, scale, nw_ref[...], win_ref[...], qnw_ref[...], kvnw_ref[...],
        wq_ref[...], wqr_ref[...], cr_ref[...], sr_ref[...], cb_ref[0], sb_ref[0])
    cosq, sinq = cosq * QK_PRESCALE, sinq * QK_PRESCALE
    for h in range(N_HEADS):
        sl = slice(h * HEAD_PAD, (h + 1) * HEAD_PAD)
        q_ref[0, :, sl] = (qa[:, sl] * cosq + qb[:, sl] * sinq).astype(BF16)
    ckv_b = ckv.astype(BF16)
    k = jnp.dot(jnp.concatenate([ckv_b, kr128.astype(BF16)], axis=1), wk_ref[...], preferred_element_type=F32)
    k_ref[0] = k.astype(BF16)
    vt_ref[0] = lax.dot_general(wvt_ref[...], ckv_b, _NT, preferred_element_type=F32).astype(BF16)
    ckv_ref[0] = ckv
    kr_ref[0] = kr128[:, :ROPE_DIM]
    sg_ref[0] = _silu(g_att)

    @pl.when(i == 0)
    def _():
        uext_ref[0:HALO, :] = jnp.zeros((HALO, D_POOL), F32)

    uext_ref[HALO:HALO + tm, :] = u
    pos = (i * tm + lax.broadcasted_iota(jnp.int32, (tm, 1), 0)).astype(F32)
    for g, w in enumerate(POOL_WINDOWS):
        lo, hi = g * POOL_GROUP, (g + 1) * POOL_GROUP
        wsum = uext_ref[HALO:HALO + tm, lo:hi]
        for d in range(1, w):
            wsum = wsum + uext_ref[HALO - d:HALO - d + tm, lo:hi]
        cnt = jnp.minimum(pos + 1.0, float(w))
        pooled = wsum / cnt - u[:, lo:hi]
        mp_ref[0, :, lo:hi] = _pool_out(pooled, g, wpool_ref, bpool_ref, pscale_ref, g_pool).astype(BF16)
    uext_ref[0:HALO, :] = u[tm - HALO:tm, :]

    @pl.when(i == pl.num_programs(1) - 1)
    def _():
        utail_ref[0] = u[tm - HALO:tm, :]


def _full(shape):
    nd = len(shape)
    return pl.BlockSpec(shape, lambda *_: (0,) * nd)


def _inproj_p(x, mod3, wts, wk, wvt, tabs, tm=512):
    b, s, d = x.shape
    nt = s // tm
    row = lambda w: pl.BlockSpec((1, tm, w), lambda bi, i: (bi, i, 0))
    rtab = _full((tm, HEAD_PAD))
    btab = pl.BlockSpec((1, 1, HEAD_PAD), lambda bi, i: (i, 0, 0))
    in_specs = [row(d), pl.BlockSpec((1, 3, d), lambda bi, i: (bi, 0, 0))]
    in_specs += [_full(w.shape) for w in (*wts, wk, wvt)]
    in_specs += [rtab, rtab, btab, btab]
    out_shape = [
        jax.ShapeDtypeStruct((b, s, N_HEADS * HEAD_PAD), BF16),
        jax.ShapeDtypeStruct((b, s, N_HEADS * HEAD_PAD), BF16),
        jax.ShapeDtypeStruct((b, D_ATT, s), BF16),
        jax.ShapeDtypeStruct((b, s, KV_LORA), F32),
        jax.ShapeDtypeStruct((b, s, ROPE_DIM), F32),
        jax.ShapeDtypeStruct((b, s, D_ATT), F32),
        jax.ShapeDtypeStruct((b, s, D_POOL), BF16),
        jax.ShapeDtypeStruct((b, HALO, D_POOL), F32),
    ]
    out_specs = [row(N_HEADS * HEAD_PAD), row(N_HEADS * HEAD_PAD),
                 pl.BlockSpec((1, D_ATT, tm), lambda bi, i: (bi, 0, i)), row(KV_LORA), row(ROPE_DIM),
                 row(D_ATT), row(D_POOL), pl.BlockSpec((1, HALO, D_POOL), lambda bi, i: (bi, 0, 0))]
    return pl.pallas_call(
        functools.partial(_inproj_p_kernel, tm=tm),
        grid=(b, nt),
        in_specs=in_specs,
        out_specs=out_specs,
        out_shape=out_shape,
        scratch_shapes=[pltpu.VMEM((tm + HALO, D_POOL), F32)],
        compiler_params=pltpu.CompilerParams(dimension_semantics=("arbitrary", "arbitrary"),
                                             vmem_limit_bytes=VMEM_LIMIT),
        name="inproj_p",
    )(x, mod3, *wts, wk, wvt, *tabs)


def _inproj_s_kernel(x_ref, mod_ref, nw_ref, win_ref, qnw_ref, kvnw_ref, wq_ref, wqr_ref,
                     wpool_ref, bpool_ref, pscale_ref, cr_ref, sr_ref, cb_ref, sb_ref,
                     state_ref, wukt_ref,
                     qabs_ref, qrope_ref, ckv_ref, kr_ref, sg_ref, mp_ref, u_ref):
    d = x_ref.shape[1]
    shift = mod_ref[:, 0:d]
    scale = mod_ref[:, d:2 * d]
    qa, qb, ckv, kr128, g_att, u, g_pool, cosq, sinq = _project(
        x_ref[...], shift, scale, nw_ref[...], win_ref[...], qnw_ref[...], kvnw_ref[...],
        wq_ref[...], wqr_ref[...], cr_ref[...], sr_ref[...], cb_ref[0], sb_ref[0])
    for h in range(N_HEADS):
        sl = slice(h * HEAD_PAD, (h + 1) * HEAD_PAD)
        qrope_ref[:, sl] = qa[:, sl] * cosq + qb[:, sl] * sinq
        qabs_ref[:, sl] = _bdot(qa[:, sl], wukt_ref[h])
    ckv_ref[...] = ckv
    kr_ref[...] = kr128[:, :ROPE_DIM]
    sg_ref[...] = _silu(g_att)
    u_ref[...] = u
    for g, w in enumerate(POOL_WINDOWS):
        lo, hi = g * POOL_GROUP, (g + 1) * POOL_GROUP
        wsum = u[:, lo:hi]
        for j in range(1, w):
            wsum = wsum + state_ref[POOL_PREV - j, :, lo:hi]
        pooled = wsum / float(w) - u[:, lo:hi]
        mp_ref[:, lo:hi] = _pool_out(pooled, g, wpool_ref, bpool_ref, pscale_ref, g_pool).astype(BF16)


def _inproj_s(x, mod, wts, tabs, state, wukt):
    n, d = x.shape
    args = (x, mod, *wts, *tabs, state, wukt)
    out_shape = [
        jax.ShapeDtypeStruct((n, N_HEADS * HEAD_PAD), F32),
        jax.ShapeDtypeStruct((n, N_HEADS * HEAD_PAD), F32),
        jax.ShapeDtypeStruct((n, KV_LORA), F32),
        jax.ShapeDtypeStruct((n, ROPE_DIM), F32),
        jax.ShapeDtypeStruct((n, D_ATT), F32),
        jax.ShapeDtypeStruct((n, D_POOL), BF16),
        jax.ShapeDtypeStruct((n, D_POOL), F32),
    ]
    return pl.pallas_call(
        _inproj_s_kernel,
        in_specs=[_full(a.shape) for a in args],
        out_specs=[_full(o.shape) for o in out_shape],
        out_shape=out_shape,
        grid=(1,),
        compiler_params=pltpu.CompilerParams(dimension_semantics=("arbitrary",), vmem_limit_bytes=VMEM_LIMIT),
        name="inproj_s",
    )(*args)


def _flash_tile(qi, q_ref, k_ref, vt_ref, sg_ref, o_ref, st_ref, between, *, tq, tk):
    assert tq == tk
    qs = [q_ref[0, :, hh * HEAD_PAD:(hh + 1) * HEAD_PAD] for hh in range(2)]

    def produce(j, slot):
        for hh in range(2):
            kb = k_ref[0, pl.ds(pl.multiple_of(j * tk, tk), tk), hh * HEAD_PAD:(hh + 1) * HEAD_PAD]
            st_ref[slot][hh][...] = lax.dot_general(kb, qs[hh], _NT, preferred_element_type=F32)

    def consume(j, slot, states, masked):
        out = []
        for hh in range(2):
            m, acc = states[hh]
            st = st_ref[slot][hh][...]
            if masked:
                kpos = lax.broadcasted_iota(jnp.int32, (tk, tq), 0)
                qpos = lax.broadcasted_iota(jnp.int32, (tk, tq), 1)
                st = jnp.where(kpos <= qpos, st, -jnp.inf)
            vtb = vt_ref[0, hh * V_DIM:(hh + 1) * V_DIM, pl.ds(pl.multiple_of(j * tk, tk), tk)]
            vtb = jnp.concatenate([vtb, jnp.ones((SUM_ROWS, tk), BF16)], axis=0)
            m_new = jnp.maximum(m, jnp.max(st, axis=0, keepdims=True))
            alpha = jnp.exp2(m - m_new)
            pt = jnp.exp2(st - m_new).astype(BF16)
            acc_new = alpha * acc + jnp.dot(vtb, pt, preferred_element_type=F32)
            out.append((m_new, acc_new))
        return tuple(out)

    def finish(states):
        att_t = jnp.concatenate([acc[:V_DIM] / acc[V_DIM:V_DIM + 1] for _, acc in states], axis=0)
        o_ref[0] = (att_t.T * sg_ref[0]).astype(BF16)

    def pair(t, states):
        j = 2 * t
        produce(j + 1, 1)
        states = consume(j, 0, states, False)
        produce(j + 2, 0)
        return consume(j + 1, 1, states, False)

    init = tuple((jnp.full((1, tq), -jnp.inf, F32), jnp.zeros((V_DIM + SUM_ROWS, tq), F32)) for _ in range(2))
    rows = tk // (FIRST_CHUNKS // 2)
    for hh in range(2):
        for c in range(FIRST_CHUNKS // 2):
            kb = k_ref[0, c * rows:(c + 1) * rows, hh * HEAD_PAD:(hh + 1) * HEAD_PAD]
            st_ref[0][hh][c * rows:(c + 1) * rows, :] = lax.dot_general(kb, qs[hh], _NT,
                                                                        preferred_element_type=F32)
            between(hh * (FIRST_CHUNKS // 2) + c)
    states = lax.fori_loop(0, qi // 2, pair, init)

    @pl.when(qi % 2 == 0)
    def _():
        finish(consume(qi, 0, states, True))

    @pl.when(qi % 2 == 1)
    def _():
        produce(qi, 1)
        finish(consume(qi, 1, consume(qi - 1, 0, states, False), True))


def _page_copies(pt_ref, cckv_hbm, ckrt_hbm, ckv_buf, krt_buf, sem, n_pages, row, sl, p):
    page = pt_ref[row * n_pages + p]
    cols = pl.ds(p * PAGE_SIZE if isinstance(p, int) else pl.multiple_of(p * PAGE_SIZE, PAGE_SIZE), PAGE_SIZE)
    return (pltpu.make_async_copy(cckv_hbm.at[0, page], ckv_buf.at[sl, cols], sem.at[0, sl]),
            pltpu.make_async_copy(ckrt_hbm.at[0, page], krt_buf.at[sl, :, cols], sem.at[1, sl]))


def _decode_row(pt_ref, qa_ref, qr_ref, ckvn_ref, krn_ref, wv_ref, sg_ref, cckv_hbm, ckrt_hbm,
                o_ref, ckv_buf, krt_buf, sem, *, row, slot, n_pages):
    def wait_body(p, _):
        for cp in _page_copies(pt_ref, cckv_hbm, ckrt_hbm, ckv_buf, krt_buf, sem, n_pages, row, slot, p):
            cp.wait()
        return 0
    lax.fori_loop(0, n_pages, wait_body, 0, unroll=8)

    qa = qa_ref[0].astype(BF16)
    qr = qr_ref[0].astype(BF16)
    ckv = ckv_buf[slot].astype(BF16)
    krt = krt_buf[slot].astype(BF16)
    s = (lax.dot_general(qa, ckv, _NT, preferred_element_type=F32)
         + jnp.dot(qr, krt, preferred_element_type=F32))
    ckvn = ckvn_ref[0].astype(BF16).astype(F32)
    krn = krn_ref[0].astype(BF16).astype(F32)
    s_n = (jnp.sum(qa.astype(F32) * ckvn, axis=1, keepdims=True)
           + jnp.sum(qr.astype(F32) * krn, axis=1, keepdims=True))
    m = jnp.maximum(jnp.max(s, axis=1, keepdims=True), s_n)
    p = jnp.exp2((s - m) * QK_PRESCALE)
    p_n = jnp.exp2((s_n - m) * QK_PRESCALE)
    l = jnp.sum(p, axis=1, keepdims=True) + p_n
    o_lat = (jnp.dot(p.astype(BF16), ckv, preferred_element_type=F32)
             + p_n.astype(BF16).astype(F32) * ckvn) / l
    o_all = _bdot(o_lat, wv_ref[...])
    hrow = lax.broadcasted_iota(jnp.int32, o_all.shape, 0)
    hcol = lax.broadcasted_iota(jnp.int32, o_all.shape, 1) // V_DIM
    att = jnp.sum(jnp.where(hrow == hcol, o_all, 0.0), axis=0, keepdims=True)
    o_ref[0] = (att * sg_ref[0]).astype(BF16)


def _attn_kernel(pt_ref, q_ref, k_ref, vt_ref, sgp_ref, qa_ref, qr_ref, ckvn_ref, krn_ref, wv_ref, sgs_ref,
                 cckv_hbm, ckrt_hbm, op_ref, os_ref, st00, st01, st10, st11, ckv_buf, krt_buf, sem,
                 *, tq, tk, n_pages):
    qi = pl.program_id(2)
    r = (pl.program_id(0) * pl.num_programs(1) + pl.program_id(1)) * pl.num_programs(2) + qi
    n_rows = pl.num_programs(0) * pl.num_programs(1) * pl.num_programs(2)
    slot = r % 2
    dma = (pt_ref, cckv_hbm, ckrt_hbm, ckv_buf, krt_buf, sem, n_pages)

    @pl.when(r == 0)
    def _():
        def body(p, _):
            for cp in _page_copies(*dma, 0, 0, p):
                cp.start()
            return 0
        lax.fori_loop(0, n_pages, body, 0, unroll=8)

    nxt = jnp.minimum(r + 1, n_rows - 1)
    assert n_pages % FIRST_CHUNKS == 0
    per_chunk = n_pages // FIRST_CHUNKS

    def issue(c):
        for p in range(c * per_chunk, (c + 1) * per_chunk):
            for cp in _page_copies(*dma, nxt, 1 - slot, p):
                cp.start()

    _flash_tile(qi, q_ref, k_ref, vt_ref, sgp_ref, op_ref, ((st00, st01), (st10, st11)), issue, tq=tq, tk=tk)
    _decode_row(pt_ref, qa_ref, qr_ref, ckvn_ref, krn_ref, wv_ref, sgs_ref, cckv_hbm, ckrt_hbm,
                os_ref, ckv_buf, krt_buf, sem, row=r, slot=slot, n_pages=n_pages)

    @pl.when(r == n_rows - 1)
    def _():
        def body(p, _):
            for cp in _page_copies(*dma, nxt, 1 - slot, p):
                cp.wait()
            return 0
        lax.fori_loop(0, n_pages, body, 0, unroll=8)


def _attention(q, k, vt, sg_p, page_table, qabs, qrope, ckv_new, kr_new, wv, sg_s, cache_ckv, cache_krt,
               tq=512, tk=512):
    b, s, _ = q.shape
    n_pairs = N_HEADS // 2
    nq = s // tq
    n, n_pages = page_table.shape
    assert n == b * n_pairs * nq, "one sample row per prompt attention step"
    past = n_pages * PAGE_SIZE
    row = lambda bi, hp, qi: (bi * n_pairs + hp) * nq + qi
    per_r = lambda w: pl.BlockSpec((1, 1, w), lambda bi, hp, qi, pt: (row(bi, hp, qi), 0, 0))
    per_h = lambda w: pl.BlockSpec((1, N_HEADS, w), lambda bi, hp, qi, pt: (row(bi, hp, qi), 0, 0))
    grid_spec = pltpu.PrefetchScalarGridSpec(
        num_scalar_prefetch=1,
        grid=(b, n_pairs, nq),
        in_specs=[pl.BlockSpec((1, tq, 2 * HEAD_PAD), lambda bi, hp, qi, pt: (bi, qi, hp)),
                  pl.BlockSpec((1, s, 2 * HEAD_PAD), lambda bi, hp, qi, pt: (bi, 0, hp)),
                  pl.BlockSpec((1, 2 * V_DIM, s), lambda bi, hp, qi, pt: (bi, hp, 0)),
                  pl.BlockSpec((1, tq, 2 * V_DIM), lambda bi, hp, qi, pt: (bi, qi, hp)),
                  per_h(KV_LORA), per_h(ROPE_DIM), per_r(KV_LORA), per_r(ROPE_DIM),
                  pl.BlockSpec(wv.shape, lambda bi, hp, qi, pt: (0, 0)),
                  per_r(D_ATT),
                  pl.BlockSpec(memory_space=pl.ANY),
                  pl.BlockSpec(memory_space=pl.ANY)],
        out_specs=[pl.BlockSpec((1, tq, 2 * V_DIM), lambda bi, hp, qi, pt: (bi, qi, hp)), per_r(D_ATT)],
        scratch_shapes=[pltpu.VMEM((tk, tq), F32)] * 4 + [
            pltpu.VMEM((2, past, KV_LORA), F32),
            pltpu.VMEM((2, ROPE_DIM, past), F32),
            pltpu.SemaphoreType.DMA((2, 2))],
    )
    return pl.pallas_call(
        functools.partial(_attn_kernel, tq=tq, tk=tk, n_pages=n_pages),
        grid_spec=grid_spec,
        out_shape=[jax.ShapeDtypeStruct((b, s, D_ATT), BF16), jax.ShapeDtypeStruct((n, 1, D_ATT), BF16)],
        compiler_params=pltpu.CompilerParams(dimension_semantics=("arbitrary", "arbitrary", "arbitrary"),
                                             vmem_limit_bytes=VMEM_LIMIT),
        name="attention",
    )(page_table.reshape(-1), q, k, vt, sg_p, qabs, qrope, ckv_new, kr_new, wv, sg_s, cache_ckv, cache_krt)


def _combine_kernel(x_ref, gate_ref, ag_ref, mp_ref, wo_ref, fnw_ref, y_ref):
    proj = (jnp.dot(ag_ref[0], wo_ref[0:D_ATT, :], preferred_element_type=F32)
            + jnp.dot(mp_ref[0], wo_ref[D_ATT:D_ATT + D_POOL, :], preferred_element_type=F32))
    y = x_ref[0] + gate_ref[0] * proj
    y_ref[0] = _rms(y, fnw_ref[...])


def _combine(x, gate, ag, mp, wo, fnw, tm):
    b, s, d = x.shape
    gr = gate.shape[1]
    gate_spec = (pl.BlockSpec((1, 1, d), lambda bi, i: (bi, 0, 0)) if gr == 1
                 else pl.BlockSpec((1, tm, d), lambda bi, i: (bi, i, 0)))
    row = lambda w: pl.BlockSpec((1, tm, w), lambda bi, i: (bi, i, 0))
    return pl.pallas_call(
        _combine_kernel,
        grid=(b, s // tm),
        in_specs=[row(d), gate_spec, row(D_ATT), row(D_POOL), _full(wo.shape), _full(fnw.shape)],
        out_specs=row(d),
        out_shape=jax.ShapeDtypeStruct((b, s, d), F32),
        compiler_params=pltpu.CompilerParams(dimension_semantics=("arbitrary", "arbitrary"),
                                             vmem_limit_bytes=VMEM_LIMIT),
        name="combine",
    )(x, gate, ag, mp, wo, fnw)


def _rope_tables(offsets, bases):
    inv = ROPE_BASE ** (-jnp.arange(0, ROPE_DIM, 2, dtype=F32) / ROPE_DIM)
    inv = jnp.pad(jnp.concatenate([inv, inv]), (0, HEAD_PAD - ROPE_DIM))
    lanes = (jnp.arange(HEAD_PAD) < ROPE_DIM).astype(F32)
    a_off = offsets.astype(F32)[:, None] * inv[None, :]
    a_base = bases.astype(F32)[:, None, None] * inv[None, None, :]
    return jnp.cos(a_off) * lanes, jnp.sin(a_off) * lanes, jnp.cos(a_base), jnp.sin(a_base)


def _rot_half(w):
    half = w.shape[-1] // 2
    return jnp.concatenate([-w[..., half:], w[..., :half]], axis=-1)


def _pack_weights(norm_w, w_in, q_norm_w, w_uq, kv_norm_w, w_uk, w_uv, w_pool, b_pool, pool_scale):
    d = w_in.shape[0]
    off_kr = Q_LORA + KV_LORA
    krc = w_in[:, off_kr:off_kr + ROPE_DIM]
    win = jnp.concatenate([w_in[:, :off_kr], krc, _rot_half(krc),
                           jnp.zeros((d, HEAD_PAD - 2 * ROPE_DIM), F32), w_in[:, off_kr + ROPE_DIM:]], axis=1)
    pad_q = HEAD_PAD - QK_DIM
    wq = jnp.pad(w_uq, ((0, 0), (0, 0), (0, pad_q))).reshape(Q_LORA, N_HEADS * HEAD_PAD)
    wqr = jnp.pad(_rot_half(w_uq[..., NOPE_DIM:]), ((0, 0), (0, 0), (NOPE_DIM, pad_q)))
    wqr = wqr.reshape(Q_LORA, N_HEADS * HEAD_PAD)
    wk_nope = jnp.pad(w_uk, ((0, 0), (0, 0), (0, HEAD_PAD - NOPE_DIM))).reshape(KV_LORA, N_HEADS * HEAD_PAD)
    place = jnp.pad(jnp.eye(ROPE_DIM, dtype=F32), ((0, HEAD_PAD - ROPE_DIM), (NOPE_DIM, pad_q)))
    wk = jnp.concatenate([wk_nope, jnp.tile(place, (1, N_HEADS))], axis=0)
    wv = w_uv.reshape(KV_LORA, N_HEADS * V_DIM).astype(BF16)
    shared = (norm_w.reshape(1, -1), win.astype(BF16), q_norm_w.reshape(1, -1), kv_norm_w.reshape(1, -1),
              wq.astype(BF16), wqr.astype(BF16), w_pool.astype(BF16), b_pool.reshape(1, -1),
              pool_scale.reshape(1, -1))
    return shared, wk.astype(BF16), wv


def kernel(x_prompt, x_sample, c_prompt, c_sample, cache_ckv, cache_krope, state_pool, page_table, ada_w, ada_b, norm_w, w_in, q_norm_w, w_uq, kv_norm_w, w_uk, w_uv, w_pool, b_pool, pool_scale, w_out, final_norm_w):
    assert ada_w.shape[0] == 1, "single-layer trunk only"
    b, s, d = x_prompt.shape
    n = x_sample.shape[0]
    assert x_sample.shape[1] == 1
    past_len = page_table.shape[1] * PAGE_SIZE

    c_all = jnp.concatenate([c_prompt, c_sample], axis=0)
    rows = -(-c_all.shape[0] // 8) * 8
    c_all = jnp.pad(c_all, ((0, rows - c_all.shape[0]), (0, 0)))
    mod = _ada(c_all, ada_w[0].astype(BF16), ada_b[0].reshape(1, -1))
    mod_p = mod[:b].reshape(b, 3, d)
    mod_s = mod[b:b + n]

    wts, wk, wv = _pack_weights(norm_w[0], w_in[0], q_norm_w[0], w_uq[0], kv_norm_w[0], w_uk[0], w_uv[0],
                                w_pool[0], b_pool[0], pool_scale[0])
    wo = w_out[0].astype(BF16)
    fnw = final_norm_w.reshape(1, -1)

    tabs_p = _rope_tables(jnp.arange(ROW_TILE), jnp.arange(0, s, ROW_TILE))
    q, k, vt, ckv_p, kr_p, sg_p, mp_p, utail = _inproj_p(x_prompt, mod_p, wts, wk, wv.T, tabs_p, tm=ROW_TILE)

    tabs_s = _rope_tables(jnp.zeros((n,)), jnp.full((1,), past_len))
    wukt = jnp.pad(jnp.transpose(w_uk[0], (1, 2, 0)), ((0, 0), (0, HEAD_PAD - NOPE_DIM), (0, 0))).astype(BF16)
    state_t = jnp.transpose(state_pool[0], (1, 0, 2))
    qabs, qrope, ckv_s, kr_s, sg_s, mp_s, u_s = _inproj_s(x_sample[:, 0, :], mod_s, wts, tabs_s, state_t, wukt)
    qrope = qrope.reshape(n, N_HEADS, HEAD_PAD)[:, :, NOPE_DIM:QK_DIM]
    ag_p, ag_s = _attention(q, k, vt, sg_p, page_table, qabs.reshape(n, N_HEADS, KV_LORA), qrope,
                            ckv_s[:, None, :], kr_s[:, None, :], wv, sg_s[:, None, :],
                            cache_ckv, jnp.swapaxes(cache_krope, 2, 3))
    y_prompt = _combine(x_prompt, mod_p[:, 2:3, :], ag_p, mp_p, wo, fnw, tm=512)
    y_sample = _combine(x_sample.reshape(1, n, d), mod_s[None, :, 2 * d:], ag_s.reshape(1, n, D_ATT),
                        mp_s[None], wo, fnw, tm=n).reshape(n, 1, d)

    new_pool_p = utail[:, HALO - POOL_PREV:, :]
    new_pool_s = jnp.transpose(jnp.concatenate([state_t[1:], u_s[None]], axis=0), (1, 0, 2))
    return (y_prompt, y_sample,
            ckv_p[None], kr_p[None], new_pool_p[None],
            ckv_s[:, None, :][None], kr_s[:, None, :][None], new_pool_s[None])
```

```python
import functools
import math

import jax
import jax.numpy as jnp
from jax import lax
from jax.experimental import pallas as pl
from jax.experimental.pallas import tpu as pltpu

F32 = jnp.float32
BF16 = jnp.bfloat16

N_HEADS = 8
NOPE_DIM = 64
ROPE_DIM = 32
QK_DIM = NOPE_DIM + ROPE_DIM
V_DIM = 64
Q_LORA = 256
KV_LORA = 128
D_POOL = 512
D_ATT = 512
POOL_WINDOWS = (2, 4, 8, 16)
POOL_GROUP = 128
POOL_PREV = 15
PAGE_SIZE = 128
ROPE_BASE = 10000.0
EPS = 1e-6
SM_SCALE = QK_DIM ** -0.5
LOG2E = math.log2(math.e)
QK_PRESCALE = SM_SCALE * LOG2E
_NT = (((1,), (1,)), ((), ()))

HEAD_PAD = 128
COL_SPLIT = 1
FIRST_CHUNKS = 8
SUM_ROWS = 16
HALO = 16
C_Q, C_KV, C_KR, C_GA, C_U, C_GP, C_END = 0, 256, 384, 512, 1024, 1536, 2048

ROW_TILE = 512
VMEM_LIMIT = 56 * 1024 * 1024


def _rms(x, w):
    return (x * lax.rsqrt(jnp.mean(x * x, axis=-1, keepdims=True) + EPS)) * w


def _silu(x):
    return x * jax.nn.sigmoid(x)


def _bdot(a, b):
    return jnp.dot(a.astype(BF16), b, preferred_element_type=F32)


def _ada_kernel(c_ref, w_ref, b_ref, o_ref):
    o_ref[...] = _bdot(_silu(c_ref[...]), w_ref[...].astype(BF16)) + b_ref[...]


def _ada(c_all, ada_w, ada_b):
    rows, d = c_all.shape
    n = ada_w.shape[1]
    bn = 1024
    return pl.pallas_call(
        _ada_kernel,
        grid=(n // bn,),
        in_specs=[pl.BlockSpec((rows, d), lambda j: (0, 0)),
                  pl.BlockSpec((d, bn), lambda j: (0, j)),
                  pl.BlockSpec((1, bn), lambda j: (0, j))],
        out_specs=pl.BlockSpec((rows, bn), lambda j: (0, j)),
        out_shape=jax.ShapeDtypeStruct((rows, n), F32),
        compiler_params=pltpu.CompilerParams(dimension_semantics=("arbitrary",)),
        name="ada",
    )(c_all, ada_w, ada_b)


def _project(x, shift, scale, nw, win, qnw, kvnw, wq, wqr, cr, sr, cb, sb):
    cosk = cb * cr - sb * sr
    sink = sb * cr + cb * sr
    nope = (lax.broadcasted_iota(jnp.int32, (1, HEAD_PAD), 1) < NOPE_DIM).astype(F32)
    cosq = pltpu.roll(cosk, NOPE_DIM, axis=1) + nope
    sinq = pltpu.roll(sink, NOPE_DIM, axis=1)
    h = _rms(x, nw) * (1.0 + scale) + shift
    z = _bdot(h, win)
    q_lat = _rms(z[:, C_Q:C_KV], qnw)
    ckv = _rms(z[:, C_KV:C_KR], kvnw)
    krc = z[:, C_KR:C_GA]
    kr128 = krc * cosk + pltpu.roll(krc, HEAD_PAD - ROPE_DIM, axis=1) * sink
    ql = q_lat.astype(BF16)
    qa = jnp.dot(ql, wq, preferred_element_type=F32)
    qb = jnp.dot(ql, wqr, preferred_element_type=F32)
    return qa, qb, ckv, kr128, z[:, C_GA:C_U], z[:, C_U:C_GP], z[:, C_GP:C_END], cosq, sinq


def _pool_out(pooled_g, g, wpool_ref, bpool_ref, pscale_ref, g_pool):
    lo, hi = g * POOL_GROUP, (g + 1) * POOL_GROUP
    y = _bdot(pooled_g, wpool_ref[g]) + bpool_ref[:, lo:hi]
    y = y * pscale_ref[:, lo:hi]
    return y * _silu(g_pool[:, lo:hi])


def _inproj_p_kernel(x_ref, mod_ref, nw_ref, win_ref, qnw_ref, kvnw_ref, wq_ref, wqr_ref,
                     wpool_ref, bpool_ref, pscale_ref, wk_ref, wvt_ref, cr_ref, sr_ref, cb_ref, sb_ref,
                     q_ref, k_ref, vt_ref, ckv_ref, krt_ref, sg_ref, mp_ref, utail_ref, uext_ref, *, tm):
    i = pl.program_id(1)
    shift = mod_ref[0, 0:1, :]
    scale = mod_ref[0, 1:2, :]
    qa, qb, ckv, kr128, g_att, u, g_pool, cosq, sinq = _project(
        x_ref[0], shift, scale, nw_ref[...], win_ref[...], qnw_ref[...], kvnw_ref[...],
        wq_ref[...], wqr_ref[...], cr_ref[...], sr_ref[...], cb_ref[0], sb_ref[0])
    cosq, sinq = cosq * QK_PRESCALE, sinq * QK_PRESCALE
    for h in range(N_HEADS):
        sl = slice(h * HEAD_PAD, (h + 1) * HEAD_PAD)
        q_ref[0, :, sl] = (qa[:, sl] * cosq + qb[:, sl] * sinq).astype(BF16)
    ckv_b = ckv.astype(BF16)
    k = jnp.dot(jnp.concatenate([ckv_b, kr128.astype(BF16)], axis=1), wk_ref[...], preferred_element_type=F32)
    k_ref[0] = k.astype(BF16)
    vt_ref[0] = lax.dot_general(wvt_ref[...], ckv_b, _NT, preferred_element_type=F32).astype(BF16)
    ckv_ref[0] = ckv
    krt_ref[0] = kr128.T[:ROPE_DIM, :]
    sg_ref[0] = _silu(g_att)

    @pl.when(i == 0)
    def _():
        uext_ref[0:HALO, :] = jnp.zeros((HALO, D_POOL), F32)

    uext_ref[HALO:HALO + tm, :] = u
    pos = (i * tm + lax.broadcasted_iota(jnp.int32, (tm, 1), 0)).astype(F32)
    for g, w in enumerate(POOL_WINDOWS):
        lo, hi = g * POOL_GROUP, (g + 1) * POOL_GROUP
        wsum = uext_ref[HALO:HALO + tm, lo:hi]
        for d in range(1, w):
            wsum = wsum + uext_ref[HALO - d:HALO - d + tm, lo:hi]
        cnt = jnp.minimum(pos + 1.0, float(w))
        pooled = wsum / cnt - u[:, lo:hi]
        mp_ref[0, :, lo:hi] = _pool_out(pooled, g, wpool_ref, bpool_ref, pscale_ref, g_pool).astype(BF16)
    uext_ref[0:HALO, :] = u[tm - HALO:tm, :]

    @pl.when(i == pl.num_programs(1) - 1)
    def _():
        utail_ref[0] = u[tm - HALO:tm, :]


def _full(shape):
    nd = len(shape)
    return pl.BlockSpec(shape, lambda *_: (0,) * nd)


def _inproj_p(x, mod3, wts, wk, wvt, tabs, tm=512):
    b, s, d = x.shape
    nt = s // tm
    row = lambda w: pl.BlockSpec((1, tm, w), lambda bi, i: (bi, i, 0))
    rtab = _full((tm, HEAD_PAD))
    btab = pl.BlockSpec((1, 1, HEAD_PAD), lambda bi, i: (i, 0, 0))
    in_specs = [row(d), pl.BlockSpec((1, 3, d), lambda bi, i: (bi, 0, 0))]
    in_specs += [_full(w.shape) for w in (*wts, wk, wvt)]
    in_specs += [rtab, rtab, btab, btab]
    out_shape = [
        jax.ShapeDtypeStruct((b, s, N_HEADS * HEAD_PAD), BF16),
        jax.ShapeDtypeStruct((b, s, N_HEADS * HEAD_PAD), BF16),
        jax.ShapeDtypeStruct((b, D_ATT, s), BF16),
        jax.ShapeDtypeStruct((b, s, KV_LORA), F32),
        jax.ShapeDtypeStruct((b, ROPE_DIM, s), F32),
        jax.ShapeDtypeStruct((b, s, D_ATT), F32),
        jax.ShapeDtypeStruct((b, s, D_POOL), BF16),
        jax.ShapeDtypeStruct((b, HALO, D_POOL), F32),
    ]
    out_specs = [row(N_HEADS * HEAD_PAD), row(N_HEADS * HEAD_PAD),
                 pl.BlockSpec((1, D_ATT, tm), lambda bi, i: (bi, 0, i)), row(KV_LORA),
                 pl.BlockSpec((1, ROPE_DIM, tm), lambda bi, i: (bi, 0, i)),
                 row(D_ATT), row(D_POOL), pl.BlockSpec((1, HALO, D_POOL), lambda bi, i: (bi, 0, 0))]
    return pl.pallas_call(
        functools.partial(_inproj_p_kernel, tm=tm),
        grid=(b, nt),
        in_specs=in_specs,
        out_specs=out_specs,
        out_shape=out_shape,
        scratch_shapes=[pltpu.VMEM((tm + HALO, D_POOL), F32)],
        compiler_params=pltpu.CompilerParams(dimension_semantics=("arbitrary", "arbitrary"),
                                             vmem_limit_bytes=VMEM_LIMIT),
        name="inproj_p",
    )(x, mod3, *wts, wk, wvt, *tabs)


def _inproj_s_kernel(x_ref, mod_ref, nw_ref, win_ref, qnw_ref, kvnw_ref, wq_ref, wqr_ref,
                     wpool_ref, bpool_ref, pscale_ref, cr_ref, sr_ref, cb_ref, sb_ref,
                     state_ref, wukt_ref,
                     qabs_ref, qrope_ref, ckv_ref, kr_ref, sg_ref, mp_ref, u_ref):
    d = x_ref.shape[1]
    shift = mod_ref[:, 0:d]
    scale = mod_ref[:, d:2 * d]
    qa, qb, ckv, kr128, g_att, u, g_pool, cosq, sinq = _project(
        x_ref[...], shift, scale, nw_ref[...], win_ref[...], qnw_ref[...], kvnw_ref[...],
        wq_ref[...], wqr_ref[...], cr_ref[...], sr_ref[...], cb_ref[0], sb_ref[0])
    for h in range(N_HEADS):
        sl = slice(h * HEAD_PAD, (h + 1) * HEAD_PAD)
        qrope_ref[:, sl] = qa[:, sl] * cosq + qb[:, sl] * sinq
        qabs_ref[:, sl] = _bdot(qa[:, sl], wukt_ref[h])
    ckv_ref[...] = ckv
    kr_ref[...] = kr128[:, :ROPE_DIM]
    sg_ref[...] = _silu(g_att)
    u_ref[...] = u
    for g, w in enumerate(POOL_WINDOWS):
        lo, hi = g * POOL_GROUP, (g + 1) * POOL_GROUP
        wsum = u[:, lo:hi]
        for j in range(1, w):
            wsum = wsum + state_ref[POOL_PREV - j, :, lo:hi]
        pooled = wsum / float(w) - u[:, lo:hi]
        mp_ref[:, lo:hi] = _pool_out(pooled, g, wpool_ref, bpool_ref, pscale_ref, g_pool).astype(BF16)


def _inproj_s(x, mod, wts, tabs, state, wukt):
    n, d = x.shape
    args = (x, mod, *wts, *tabs, state, wukt)
    out_shape = [
        jax.ShapeDtypeStruct((n, N_HEADS * HEAD_PAD), F32),
        jax.ShapeDtypeStruct((n, N_HEADS * HEAD_PAD), F32),
        jax.ShapeDtypeStruct((n, KV_LORA), F32),
        jax.ShapeDtypeStruct((n, ROPE_DIM), F32),
        jax.ShapeDtypeStruct((n, D_ATT), F32),
        jax.ShapeDtypeStruct((n, D_POOL), BF16),
        jax.ShapeDtypeStruct((n, D_POOL), F32),
    ]
    return pl.pallas_call(
        _inproj_s_kernel,
        in_specs=[_full(a.shape) for a in args],
        out_specs=[_full(o.shape) for o in out_shape],
        out_shape=out_shape,
        grid=(1,),
        compiler_params=pltpu.CompilerParams(dimension_semantics=("arbitrary",), vmem_limit_bytes=VMEM_LIMIT),
        name="inproj_s",
    )(*args)


def _flash_tile(qi, q_ref, k_ref, vt_ref, sg_ref, o_ref, st_ref, between, *, tq, tk):
    assert tq == tk
    cw = tq // COL_SPLIT
    units = [(hh, cs) for hh in range(2) for cs in range(COL_SPLIT)]
    qs = [q_ref[0, cs * cw:(cs + 1) * cw, hh * HEAD_PAD:(hh + 1) * HEAD_PAD] for hh, cs in units]

    def produce(j, slot, u, rows=None):
        hh, cs = units[u]
        r0, r1 = (0, tk) if rows is None else rows
        kb = k_ref[0, pl.ds(pl.multiple_of(j * tk, tk) + r0, r1 - r0), hh * HEAD_PAD:(hh + 1) * HEAD_PAD]
        st_ref[slot][hh][r0:r1, cs * cw:(cs + 1) * cw] = lax.dot_general(kb, qs[u], _NT,
                                                                         preferred_element_type=F32)

    def consume(j, slot, u, state, masked):
        hh, cs = units[u]
        m, acc = state
        st = st_ref[slot][hh][:, cs * cw:(cs + 1) * cw]
        if masked:
            kpos = lax.broadcasted_iota(jnp.int32, (tk, cw), 0)
            qpos = cs * cw + lax.broadcasted_iota(jnp.int32, (tk, cw), 1)
            st = jnp.where(kpos <= qpos, st, -jnp.inf)
        vtb = vt_ref[0, hh * V_DIM:(hh + 1) * V_DIM, pl.ds(pl.multiple_of(j * tk, tk), tk)]
        vtb = jnp.concatenate([vtb, jnp.ones((SUM_ROWS, tk), BF16)], axis=0)
        m_new = jnp.maximum(m, jnp.max(st, axis=0, keepdims=True))
        alpha = jnp.exp2(m - m_new)
        pt = jnp.exp2(st - m_new).astype(BF16)
        return m_new, alpha * acc + jnp.dot(vtb, pt, preferred_element_type=F32)

    def finish(states):
        heads = []
        for hh in range(2):
            acc = jnp.concatenate([states[u][1] for u in range(len(units)) if units[u][0] == hh], axis=1)
            heads.append(acc[:V_DIM] / acc[V_DIM:V_DIM + 1])
        o_ref[0] = (jnp.concatenate(heads, axis=0).T * sg_ref[0]).astype(BF16)

    def pair(t, states):
        j = 2 * t
        states = list(states)
        for u in range(len(units)):
            produce(j + 1, 1, u)
            states[u] = consume(j, 0, u, states[u], False)
        for u in range(len(units)):
            produce(j + 2, 0, u)
            states[u] = consume(j + 1, 1, u, states[u], False)
        return tuple(states)

    init = tuple((jnp.full((1, cw), -jnp.inf, F32), jnp.zeros((V_DIM + SUM_ROWS, cw), F32)) for _ in units)
    pieces = FIRST_CHUNKS // len(units)
    for u in range(len(units)):
        for c in range(pieces):
            produce(0, 0, u, (c * tk // pieces, (c + 1) * tk // pieces))
            between(u * pieces + c)
    states = lax.fori_loop(0, qi // 2, pair, init)

    @pl.when(qi % 2 == 0)
    def _():
        finish([consume(qi, 0, u, states[u], True) for u in range(len(units))])

    @pl.when(qi % 2 == 1)
    def _():
        out = []
        for u in range(len(units)):
            produce(qi, 1, u)
            out.append(consume(qi - 1, 0, u, states[u], False))
        finish([consume(qi, 1, u, out[u], True) for u in range(len(units))])


def _page_copies(pt_ref, cckv_hbm, ckrt_hbm, ckv_buf, krt_buf, sem, n_pages, row, sl, p):
    page = pt_ref[row * n_pages + p]
    cols = pl.ds(p * PAGE_SIZE if isinstance(p, int) else pl.multiple_of(p * PAGE_SIZE, PAGE_SIZE), PAGE_SIZE)
    return (pltpu.make_async_copy(cckv_hbm.at[0, page], ckv_buf.at[sl, cols], sem.at[0, sl]),
            pltpu.make_async_copy(ckrt_hbm.at[0, page], krt_buf.at[sl, :, cols], sem.at[1, sl]))


def _decode_row(pt_ref, qa_ref, qr_ref, ckvn_ref, krn_ref, wv_ref, sg_ref, cckv_hbm, ckrt_hbm,
                o_ref, ckv_buf, krt_buf, sem, *, row, slot, n_pages):
    def wait_body(p, _):
        for cp in _page_copies(pt_ref, cckv_hbm, ckrt_hbm, ckv_buf, krt_buf, sem, n_pages, row, slot, p):
            cp.wait()
        return 0
    lax.fori_loop(0, n_pages, wait_body, 0, unroll=8)

    qa = qa_ref[0].astype(BF16)
    qr = qr_ref[0].astype(BF16)
    ckv = ckv_buf[slot].astype(BF16)
    krt = krt_buf[slot].astype(BF16)
    s = (lax.dot_general(qa, ckv, _NT, preferred_element_type=F32)
         + jnp.dot(qr, krt, preferred_element_type=F32))
    ckvn = ckvn_ref[0].astype(BF16).astype(F32)
    krn = krn_ref[0].astype(BF16).astype(F32)
    s_n = (jnp.sum(qa.astype(F32) * ckvn, axis=1, keepdims=True)
           + jnp.sum(qr.astype(F32) * krn, axis=1, keepdims=True))
    m = jnp.maximum(jnp.max(s, axis=1, keepdims=True), s_n)
    p = jnp.exp2((s - m) * QK_PRESCALE)
    p_n = jnp.exp2((s_n - m) * QK_PRESCALE)
    l = jnp.sum(p, axis=1, keepdims=True) + p_n
    o_lat = (jnp.dot(p.astype(BF16), ckv, preferred_element_type=F32)
             + p_n.astype(BF16).astype(F32) * ckvn) / l
    o_all = _bdot(o_lat, wv_ref[...])
    hrow = lax.broadcasted_iota(jnp.int32, o_all.shape, 0)
    hcol = lax.broadcasted_iota(jnp.int32, o_all.shape, 1) // V_DIM
    att = jnp.sum(jnp.where(hrow == hcol, o_all, 0.0), axis=0, keepdims=True)
    o_ref[0] = (att * sg_ref[0]).astype(BF16)


def _attn_kernel(pt_ref, q_ref, k_ref, vt_ref, sgp_ref, qa_ref, qr_ref, ckvn_ref, krn_ref, wv_ref, sgs_ref,
                 cckv_hbm, ckrt_hbm, op_ref, os_ref, st00, st01, st10, st11, ckv_buf, krt_buf, sem,
                 *, tq, tk, n_pages):
    qi = pl.program_id(2)
    r = (pl.program_id(0) * pl.num_programs(1) + pl.program_id(1)) * pl.num_programs(2) + qi
    n_rows = pl.num_programs(0) * pl.num_programs(1) * pl.num_programs(2)
    slot = r % 2
    dma = (pt_ref, cckv_hbm, ckrt_hbm, ckv_buf, krt_buf, sem, n_pages)

    @pl.when(r == 0)
    def _():
        def body(p, _):
            for cp in _page_copies(*dma, 0, 0, p):
                cp.start()
            return 0
        lax.fori_loop(0, n_pages, body, 0, unroll=8)

    nxt = jnp.minimum(r + 1, n_rows - 1)
    assert n_pages % FIRST_CHUNKS == 0
    per_chunk = n_pages // FIRST_CHUNKS

    def issue(c):
        for p in range(c * per_chunk, (c + 1) * per_chunk):
            for cp in _page_copies(*dma, nxt, 1 - slot, p):
                cp.start()

    _flash_tile(qi, q_ref, k_ref, vt_ref, sgp_ref, op_ref, ((st00, st01), (st10, st11)), issue, tq=tq, tk=tk)
    _decode_row(pt_ref, qa_ref, qr_ref, ckvn_ref, krn_ref, wv_ref, sgs_ref, cckv_hbm, ckrt_hbm,
                os_ref, ckv_buf, krt_buf, sem, row=r, slot=slot, n_pages=n_pages)

    @pl.when(r == n_rows - 1)
    def _():
        def body(p, _):
            for cp in _page_copies(*dma, nxt, 1 - slot, p):
                cp.wait()
            return 0
        lax.fori_loop(0, n_pages, body, 0, unroll=8)


def _attention(q, k, vt, sg_p, page_table, qabs, qrope, ckv_new, kr_new, wv, sg_s, cache_ckv, cache_krt,
               tq=512, tk=512):
    b, s, _ = q.shape
    n_pairs = N_HEADS // 2
    nq = s // tq
    n, n_pages = page_table.shape
    assert n == b * n_pairs * nq, "one sample row per prompt attention step"
    past = n_pages * PAGE_SIZE
    row = lambda bi, hp, qi: (bi * n_pairs + hp) * nq + qi
    per_r = lambda w: pl.BlockSpec((1, 1, w), lambda bi, hp, qi, pt: (row(bi, hp, qi), 0, 0))
    per_h = lambda w: pl.BlockSpec((1, N_HEADS, w), lambda bi, hp, qi, pt: (row(bi, hp, qi), 0, 0))
    grid_spec = pltpu.PrefetchScalarGridSpec(
        num_scalar_prefetch=1,
        grid=(b, n_pairs, nq),
        in_specs=[pl.BlockSpec((1, tq, 2 * HEAD_PAD), lambda bi, hp, qi, pt: (bi, qi, hp)),
                  pl.BlockSpec((1, s, 2 * HEAD_PAD), lambda bi, hp, qi, pt: (bi, 0, hp)),
                  pl.BlockSpec((1, 2 * V_DIM, s), lambda bi, hp, qi, pt: (bi, hp, 0)),
                  pl.BlockSpec((1, tq, 2 * V_DIM), lambda bi, hp, qi, pt: (bi, qi, hp)),
                  per_h(KV_LORA), per_h(ROPE_DIM), per_r(KV_LORA), per_r(ROPE_DIM),
                  pl.BlockSpec(wv.shape, lambda bi, hp, qi, pt: (0, 0)),
                  per_r(D_ATT),
                  pl.BlockSpec(memory_space=pl.ANY),
                  pl.BlockSpec(memory_space=pl.ANY)],
        out_specs=[pl.BlockSpec((1, tq, 2 * V_DIM), lambda bi, hp, qi, pt: (bi, qi, hp)), per_r(D_ATT)],
        scratch_shapes=[pltpu.VMEM((tk, tq), F32)] * 4 + [
            pltpu.VMEM((2, past, KV_LORA), F32),
            pltpu.VMEM((2, ROPE_DIM, past), F32),
            pltpu.SemaphoreType.DMA((2, 2))],
    )
    return pl.pallas_call(
        functools.partial(_attn_kernel, tq=tq, tk=tk, n_pages=n_pages),
        grid_spec=grid_spec,
        out_shape=[jax.ShapeDtypeStruct((b, s, D_ATT), BF16), jax.ShapeDtypeStruct((n, 1, D_ATT), BF16)],
        compiler_params=pltpu.CompilerParams(dimension_semantics=("arbitrary", "arbitrary", "arbitrary"),
                                             vmem_limit_bytes=VMEM_LIMIT),
        name="attention",
    )(page_table.reshape(-1), q, k, vt, sg_p, qabs, qrope, ckv_new, kr_new, wv, sg_s, cache_ckv, cache_krt)


def _combine_kernel(x_ref, gate_ref, ag_ref, mp_ref, wo_ref, fnw_ref, y_ref):
    proj = (jnp.dot(ag_ref[0], wo_ref[0:D_ATT, :], preferred_element_type=F32)
            + jnp.dot(mp_ref[0], wo_ref[D_ATT:D_ATT + D_POOL, :], preferred_element_type=F32))
    y = x_ref[0] + gate_ref[0] * proj
    y_ref[0] = _rms(y, fnw_ref[...])


def _combine(x, gate, ag, mp, wo, fnw, tm):
    b, s, d = x.shape
    gr = gate.shape[1]
    gate_spec = (pl.BlockSpec((1, 1, d), lambda bi, i: (bi, 0, 0)) if gr == 1
                 else pl.BlockSpec((1, tm, d), lambda bi, i: (bi, i, 0)))
    row = lambda w: pl.BlockSpec((1, tm, w), lambda bi, i: (bi, i, 0))
    return pl.pallas_call(
        _combine_kernel,
        grid=(b, s // tm),
        in_specs=[row(d), gate_spec, row(D_ATT), row(D_POOL), _full(wo.shape), _full(fnw.shape)],
        out_specs=row(d),
        out_shape=jax.ShapeDtypeStruct((b, s, d), F32),
        compiler_params=pltpu.CompilerParams(dimension_semantics=("arbitrary", "arbitrary"),
                                             vmem_limit_bytes=VMEM_LIMIT),
        name="combine",
    )(x, gate, ag, mp, wo, fnw)


def _rope_tables(offsets, bases):
    inv = ROPE_BASE ** (-jnp.arange(0, ROPE_DIM, 2, dtype=F32) / ROPE_DIM)
    inv = jnp.pad(jnp.concatenate([inv, inv]), (0, HEAD_PAD - ROPE_DIM))
    lanes = (jnp.arange(HEAD_PAD) < ROPE_DIM).astype(F32)
    a_off = offsets.astype(F32)[:, None] * inv[None, :]
    a_base = bases.astype(F32)[:, None, None] * inv[None, None, :]
    return jnp.cos(a_off) * lanes, jnp.sin(a_off) * lanes, jnp.cos(a_base), jnp.sin(a_base)


def _rot_half(w):
    half = w.shape[-1] // 2
    return jnp.concatenate([-w[..., half:], w[..., :half]], axis=-1)


def _pack_weights(norm_w, w_in, q_norm_w, w_uq, kv_norm_w, w_uk, w_uv, w_pool, b_pool, pool_scale):
    d = w_in.shape[0]
    off_kr = Q_LORA + KV_LORA
    krc = w_in[:, off_kr:off_kr + ROPE_DIM]
    win = jnp.concatenate([w_in[:, :off_kr], krc, _rot_half(krc),
                           jnp.zeros((d, HEAD_PAD - 2 * ROPE_DIM), F32), w_in[:, off_kr + ROPE_DIM:]], axis=1)
    pad_q = HEAD_PAD - QK_DIM
    wq = jnp.pad(w_uq, ((0, 0), (0, 0), (0, pad_q))).reshape(Q_LORA, N_HEADS * HEAD_PAD)
    wqr = jnp.pad(_rot_half(w_uq[..., NOPE_DIM:]), ((0, 0), (0, 0), (NOPE_DIM, pad_q)))
    wqr = wqr.reshape(Q_LORA, N_HEADS * HEAD_PAD)
    wk_nope = jnp.pad(w_uk, ((0, 0), (0, 0), (0, HEAD_PAD - NOPE_DIM))).reshape(KV_LORA, N_HEADS * HEAD_PAD)
    place = jnp.pad(jnp.eye(ROPE_DIM, dtype=F32), ((0, HEAD_PAD - ROPE_DIM), (NOPE_DIM, pad_q)))
    wk = jnp.concatenate([wk_nope, jnp.tile(place, (1, N_HEADS))], axis=0)
    wv = w_uv.reshape(KV_LORA, N_HEADS * V_DIM).astype(BF16)
    shared = (norm_w.reshape(1, -1), win.astype(BF16), q_norm_w.reshape(1, -1), kv_norm_w.reshape(1, -1),
              wq.astype(BF16), wqr.astype(BF16), w_pool.astype(BF16), b_pool.reshape(1, -1),
              pool_scale.reshape(1, -1))
    return shared, wk.astype(BF16), wv


def kernel(x_prompt, x_sample, c_prompt, c_sample, cache_ckv, cache_krope, state_pool, page_table, ada_w, ada_b, norm_w, w_in, q_norm_w, w_uq, kv_norm_w, w_uk, w_uv, w_pool, b_pool, pool_scale, w_out, final_norm_w):
    assert ada_w.shape[0] == 1, "single-layer trunk only"
    b, s, d = x_prompt.shape
    n = x_sample.shape[0]
    assert x_sample.shape[1] == 1
    past_len = page_table.shape[1] * PAGE_SIZE

    c_all = jnp.concatenate([c_prompt, c_sample], axis=0)
    rows = -(-c_all.shape[0] // 8) * 8
    c_all = jnp.pad(c_all, ((0, rows - c_all.shape[0]), (0, 0)))
    mod = _ada(c_all, ada_w[0], ada_b[0].reshape(1, -1))
    mod_p = mod[:b].reshape(b, 3, d)
    mod_s = mod[b:b + n]

    wts, wk, wv = _pack_weights(norm_w[0], w_in[0], q_norm_w[0], w_uq[0], kv_norm_w[0], w_uk[0], w_uv[0],
                                w_pool[0], b_pool[0], pool_scale[0])
    wo = w_out[0].astype(BF16)
    fnw = final_norm_w.reshape(1, -1)

    tabs_p = _rope_tables(jnp.arange(ROW_TILE), jnp.arange(0, s, ROW_TILE))
    q, k, vt, ckv_p, krt_p, sg_p, mp_p, utail = _inproj_p(x_prompt, mod_p, wts, wk, wv.T, tabs_p, tm=ROW_TILE)

    tabs_s = _rope_tables(jnp.zeros((n,)), jnp.full((1,), past_len))
    wukt = jnp.pad(jnp.transpose(w_uk[0], (1, 2, 0)), ((0, 0), (0, HEAD_PAD - NOPE_DIM), (0, 0))).astype(BF16)
    state_t = jnp.transpose(state_pool[0], (1, 0, 2))
    qabs, qrope, ckv_s, kr_s, sg_s, mp_s, u_s = _inproj_s(x_sample[:, 0, :], mod_s, wts, tabs_s, state_t, wukt)
    qrope = qrope.reshape(n, N_HEADS, HEAD_PAD)[:, :, NOPE_DIM:QK_DIM]
    ag_p, ag_s = _attention(q, k, vt, sg_p, page_table, qabs.reshape(n, N_HEADS, KV_LORA), qrope,
                            ckv_s[:, None, :], kr_s[:, None, :], wv, sg_s[:, None, :],
                            cache_ckv, jnp.swapaxes(cache_krope, 2, 3))
    y_prompt = _combine(x_prompt, mod_p[:, 2:3, :], ag_p, mp_p, wo, fnw, tm=512)
    y_sample = _combine(x_sample.reshape(1, n, d), mod_s[None, :, 2 * d:], ag_s.reshape(1, n, D_ATT),
                        mp_s[None], wo, fnw, tm=n).reshape(n, 1, d)

    new_pool_p = utail[:, HALO - POOL_PREV:, :]
    new_pool_s = jnp.transpose(jnp.concatenate([state_t[1:], u_s[None]], axis=0), (1, 0, 2))
    return (y_prompt, y_sample,
            ckv_p[None], jnp.swapaxes(krt_p, 1, 2)[None], new_pool_p[None],
            ckv_s[:, None, :][None], kr_s[:, None, :][None], new_pool_s[None])
```

```python
import functools
import math

import jax
import jax.numpy as jnp
from jax import lax
from jax.experimental import pallas as pl
from jax.experimental.pallas import tpu as pltpu

F32 = jnp.float32
BF16 = jnp.bfloat16

N_HEADS = 8
NOPE_DIM = 64
ROPE_DIM = 32
QK_DIM = NOPE_DIM + ROPE_DIM
V_DIM = 64
Q_LORA = 256
KV_LORA = 128
D_POOL = 512
D_ATT = 512
POOL_WINDOWS = (2, 4, 8, 16)
POOL_GROUP = 128
POOL_PREV = 15
PAGE_SIZE = 128
ROPE_BASE = 10000.0
EPS = 1e-6
SM_SCALE = QK_DIM ** -0.5
LOG2E = math.log2(math.e)
QK_PRESCALE = SM_SCALE * LOG2E
_NT = (((1,), (1,)), ((), ()))

HEAD_PAD = 128
COL_SPLIT = 1
FIRST_CHUNKS = 8
SUM_ROWS = 16
HALO = 16
C_Q, C_KV, C_KR, C_GA, C_U, C_GP, C_END = 0, 256, 384, 512, 1024, 1536, 2048

ROW_TILE = 512
ROW_PARTS = 2
VMEM_LIMIT = 56 * 1024 * 1024


def _rms(x, w):
    return (x * lax.rsqrt(jnp.mean(x * x, axis=-1, keepdims=True) + EPS)) * w


def _silu(x):
    return x * jax.nn.sigmoid(x)


def _bdot(a, b):
    return jnp.dot(a.astype(BF16), b, preferred_element_type=F32)


def _ada_kernel(c_ref, w_ref, b_ref, o_ref):
    o_ref[...] = _bdot(_silu(c_ref[...]), w_ref[...].astype(BF16)) + b_ref[...]


def _ada(c_all, ada_w, ada_b):
    rows, d = c_all.shape
    n = ada_w.shape[1]
    bn = 1024
    return pl.pallas_call(
        _ada_kernel,
        grid=(n // bn,),
        in_specs=[pl.BlockSpec((rows, d), lambda j: (0, 0)),
                  pl.BlockSpec((d, bn), lambda j: (0, j)),
                  pl.BlockSpec((1, bn), lambda j: (0, j))],
        out_specs=pl.BlockSpec((rows, bn), lambda j: (0, j)),
        out_shape=jax.ShapeDtypeStruct((rows, n), F32),
        compiler_params=pltpu.CompilerParams(dimension_semantics=("arbitrary",)),
        name="ada",
    )(c_all, ada_w, ada_b)


def _project_z(x, shift, scale, nw, win):
    h = _rms(x, nw) * (1.0 + scale) + shift
    return lax.dot_general(h.astype(BF16), win, _NT, preferred_element_type=F32)


def _project_rest(z, qnw, kvnw, wq, wqr, cr, sr, cb, sb):
    cosk = cb * cr - sb * sr
    sink = sb * cr + cb * sr
    nope = (lax.broadcasted_iota(jnp.int32, (1, HEAD_PAD), 1) < NOPE_DIM).astype(F32)
    cosq = pltpu.roll(cosk, NOPE_DIM, axis=1) + nope
    sinq = pltpu.roll(sink, NOPE_DIM, axis=1)
    q_lat = _rms(z[:, C_Q:C_KV], qnw)
    ckv = _rms(z[:, C_KV:C_KR], kvnw)
    krc = z[:, C_KR:C_GA]
    kr128 = krc * cosk + pltpu.roll(krc, HEAD_PAD - ROPE_DIM, axis=1) * sink
    ql = q_lat.astype(BF16)
    qa = jnp.dot(ql, wq, preferred_element_type=F32)
    qb = jnp.dot(ql, wqr, preferred_element_type=F32)
    return qa, qb, ckv, kr128, z[:, C_GA:C_U], z[:, C_U:C_GP], z[:, C_GP:C_END], cosq, sinq


def _pool_out(pooled_g, g, wpool_ref, bpool_ref, pscale_ref, g_pool):
    lo, hi = g * POOL_GROUP, (g + 1) * POOL_GROUP
    y = _bdot(pooled_g, wpool_ref[g]) + bpool_ref[:, lo:hi]
    y = y * pscale_ref[:, lo:hi]
    return y * _silu(g_pool[:, lo:hi])


def _inproj_p_kernel(x_ref, mod_ref, nw_ref, win_ref, qnw_ref, kvnw_ref, wq_ref, wqr_ref,
                     wpool_ref, bpool_ref, pscale_ref, wk_ref, wvt_ref, cr_ref, sr_ref, cb_ref, sb_ref,
                     q_ref, k_ref, vt_ref, ckv_ref, krt_ref, sg_ref, mp_ref, utail_ref, uext_ref, *, tm):
    i = pl.program_id(1)
    shift = mod_ref[0, 0:1, :]
    scale = mod_ref[0, 1:2, :]

    @pl.when(i == 0)
    def _():
        uext_ref[0:HALO, :] = jnp.zeros((HALO, D_POOL), F32)

    hr = tm // ROW_PARTS
    zs = []
    for part in range(ROW_PARTS):
        rows = slice(part * hr, (part + 1) * hr)
        z = _project_z(x_ref[0, rows, :], shift, scale, nw_ref[...], win_ref[...])
        uext_ref[HALO + part * hr:HALO + (part + 1) * hr, :] = z[:, C_U:C_GP]
        zs.append(z)

    for part in range(ROW_PARTS):
        r0 = part * hr
        rows = slice(r0, r0 + hr)
        qa, qb, ckv, kr128, g_att, u, g_pool, cosq, sinq = _project_rest(
            zs[part], qnw_ref[...], kvnw_ref[...], wq_ref[...], wqr_ref[...],
            cr_ref[rows, :], sr_ref[rows, :], cb_ref[0], sb_ref[0])
        cosq, sinq = cosq * QK_PRESCALE, sinq * QK_PRESCALE
        for h in range(N_HEADS):
            sl = slice(h * HEAD_PAD, (h + 1) * HEAD_PAD)
            q_ref[0, rows, sl] = (qa[:, sl] * cosq + qb[:, sl] * sinq).astype(BF16)
        ckv_b = ckv.astype(BF16)
        k = jnp.dot(jnp.concatenate([ckv_b, kr128.astype(BF16)], axis=1), wk_ref[...],
                    preferred_element_type=F32)
        k_ref[0, rows, :] = k.astype(BF16)
        vt_ref[0, :, rows] = lax.dot_general(wvt_ref[...], ckv_b, _NT, preferred_element_type=F32).astype(BF16)
        ckv_ref[0, rows, :] = ckv
        krt_ref[0, :, rows] = kr128.T[:ROPE_DIM, :]
        sg_ref[0, rows, :] = _silu(g_att)
        pos = (i * tm + r0 + lax.broadcasted_iota(jnp.int32, (hr, 1), 0)).astype(F32)
        for g, w in enumerate(POOL_WINDOWS):
            lo, hi = g * POOL_GROUP, (g + 1) * POOL_GROUP
            wsum = u[:, lo:hi]
            for d in range(1, w):
                wsum = wsum + uext_ref[HALO + r0 - d:HALO + r0 - d + hr, lo:hi]
            cnt = jnp.minimum(pos + 1.0, float(w))
            pooled = wsum / cnt - u[:, lo:hi]
            mp_ref[0, rows, lo:hi] = _pool_out(pooled, g, wpool_ref, bpool_ref, pscale_ref, g_pool).astype(BF16)

    u_last = zs[-1][hr - HALO:hr, C_U:C_GP]
    uext_ref[0:HALO, :] = u_last

    @pl.when(i == pl.num_programs(1) - 1)
    def _():
        utail_ref[0] = u_last


def _full(shape):
    nd = len(shape)
    return pl.BlockSpec(shape, lambda *_: (0,) * nd)


def _inproj_p(x, mod3, wts, wk, wvt, tabs, tm=512):
    b, s, d = x.shape
    nt = s // tm
    row = lambda w: pl.BlockSpec((1, tm, w), lambda bi, i: (bi, i, 0))
    rtab = _full((tm, HEAD_PAD))
    btab = pl.BlockSpec((1, 1, HEAD_PAD), lambda bi, i: (i, 0, 0))
    in_specs = [row(d), pl.BlockSpec((1, 3, d), lambda bi, i: (bi, 0, 0))]
    in_specs += [_full(w.shape) for w in (*wts, wk, wvt)]
    in_specs += [rtab, rtab, btab, btab]
    out_shape = [
        jax.ShapeDtypeStruct((b, s, N_HEADS * HEAD_PAD), BF16),
        jax.ShapeDtypeStruct((b, s, N_HEADS * HEAD_PAD), BF16),
        jax.ShapeDtypeStruct((b, D_ATT, s), BF16),
        jax.ShapeDtypeStruct((b, s, KV_LORA), F32),
        jax.ShapeDtypeStruct((b, ROPE_DIM, s), F32),
        jax.ShapeDtypeStruct((b, s, D_ATT), F32),
        jax.ShapeDtypeStruct((b, s, D_POOL), BF16),
        jax.ShapeDtypeStruct((b, HALO, D_POOL), F32),
    ]
    out_specs = [row(N_HEADS * HEAD_PAD), row(N_HEADS * HEAD_PAD),
                 pl.BlockSpec((1, D_ATT, tm), lambda bi, i: (bi, 0, i)), row(KV_LORA),
                 pl.BlockSpec((1, ROPE_DIM, tm), lambda bi, i: (bi, 0, i)),
                 row(D_ATT), row(D_POOL), pl.BlockSpec((1, HALO, D_POOL), lambda bi, i: (bi, 0, 0))]
    return pl.pallas_call(
        functools.partial(_inproj_p_kernel, tm=tm),
        grid=(b, nt),
        in_specs=in_specs,
        out_specs=out_specs,
        out_shape=out_shape,
        scratch_shapes=[pltpu.VMEM((tm + HALO, D_POOL), F32)],
        compiler_params=pltpu.CompilerParams(dimension_semantics=("arbitrary", "arbitrary"),
                                             vmem_limit_bytes=VMEM_LIMIT),
        name="inproj_p",
    )(x, mod3, *wts, wk, wvt, *tabs)


def _inproj_s_kernel(x_ref, mod_ref, nw_ref, win_ref, qnw_ref, kvnw_ref, wq_ref, wqr_ref,
                     wpool_ref, bpool_ref, pscale_ref, cr_ref, sr_ref, cb_ref, sb_ref,
                     state_ref, wukt_ref,
                     qabs_ref, qrope_ref, ckv_ref, kr_ref, sg_ref, mp_ref, u_ref):
    d = x_ref.shape[1]
    shift = mod_ref[:, 0:d]
    scale = mod_ref[:, d:2 * d]
    z = _project_z(x_ref[...], shift, scale, nw_ref[...], win_ref[...])
    qa, qb, ckv, kr128, g_att, u, g_pool, cosq, sinq = _project_rest(
        z, qnw_ref[...], kvnw_ref[...], wq_ref[...], wqr_ref[...], cr_ref[...], sr_ref[...], cb_ref[0], sb_ref[0])
    for h in range(N_HEADS):
        sl = slice(h * HEAD_PAD, (h + 1) * HEAD_PAD)
        qrope_ref[:, sl] = qa[:, sl] * cosq + qb[:, sl] * sinq
        qabs_ref[:, sl] = _bdot(qa[:, sl], wukt_ref[h])
    ckv_ref[...] = ckv
    kr_ref[...] = kr128[:, :ROPE_DIM]
    sg_ref[...] = _silu(g_att)
    u_ref[...] = u
    for g, w in enumerate(POOL_WINDOWS):
        lo, hi = g * POOL_GROUP, (g + 1) * POOL_GROUP
        wsum = u[:, lo:hi]
        for j in range(1, w):
            wsum = wsum + state_ref[POOL_PREV - j, :, lo:hi]
        pooled = wsum / float(w) - u[:, lo:hi]
        mp_ref[:, lo:hi] = _pool_out(pooled, g, wpool_ref, bpool_ref, pscale_ref, g_pool).astype(BF16)


def _inproj_s(x, mod, wts, tabs, state, wukt):
    n, d = x.shape
    args = (x, mod, *wts, *tabs, state, wukt)
    out_shape = [
        jax.ShapeDtypeStruct((n, N_HEADS * HEAD_PAD), F32),
        jax.ShapeDtypeStruct((n, N_HEADS * HEAD_PAD), F32),
        jax.ShapeDtypeStruct((n, KV_LORA), F32),
        jax.ShapeDtypeStruct((n, ROPE_DIM), F32),
        jax.ShapeDtypeStruct((n, D_ATT), F32),
        jax.ShapeDtypeStruct((n, D_POOL), BF16),
        jax.ShapeDtypeStruct((n, D_POOL), F32),
    ]
    return pl.pallas_call(
        _inproj_s_kernel,
        in_specs=[_full(a.shape) for a in args],
        out_specs=[_full(o.shape) for o in out_shape],
        out_shape=out_shape,
        grid=(1,),
        compiler_params=pltpu.CompilerParams(dimension_semantics=("arbitrary",), vmem_limit_bytes=VMEM_LIMIT),
        name="inproj_s",
    )(*args)


def _flash_tile(qi, q_ref, k_ref, vt_ref, sg_ref, o_ref, st_ref, between, *, tq, tk):
    assert tq == tk
    cw = tq // COL_SPLIT
    units = [(hh, cs) for hh in range(2) for cs in range(COL_SPLIT)]
    qs = [q_ref[0, cs * cw:(cs + 1) * cw, hh * HEAD_PAD:(hh + 1) * HEAD_PAD] for hh, cs in units]

    def produce(j, slot, u, rows=None):
        hh, cs = units[u]
        r0, r1 = (0, tk) if rows is None else rows
        kb = k_ref[0, pl.ds(pl.multiple_of(j * tk, tk) + r0, r1 - r0), hh * HEAD_PAD:(hh + 1) * HEAD_PAD]
        st_ref[slot][hh][r0:r1, cs * cw:(cs + 1) * cw] = lax.dot_general(kb, qs[u], _NT,
                                                                         preferred_element_type=F32)

    def consume(j, slot, u, state, diagonal=False):
        hh, cs = units[u]
        m, acc = state
        st = st_ref[slot][hh][:, cs * cw:(cs + 1) * cw]
        if diagonal:
            kpos = lax.broadcasted_iota(jnp.int32, (tk, cw), 0)
            qpos = cs * cw + lax.broadcasted_iota(jnp.int32, (tk, cw), 1)
            st = jnp.where(kpos <= qpos, st, -jnp.inf)
        vtb = vt_ref[0, hh * V_DIM:(hh + 1) * V_DIM, pl.ds(pl.multiple_of(j * tk, tk), tk)]
        vtb = jnp.concatenate([vtb, jnp.ones((SUM_ROWS, tk), BF16)], axis=0)
        m_new = jnp.maximum(m, jnp.max(st, axis=0, keepdims=True))
        alpha = jnp.exp2(m - m_new)
        pt = jnp.exp2(st - m_new).astype(BF16)
        return m_new, alpha * acc + jnp.dot(vtb, pt, preferred_element_type=F32)

    def finish(states):
        heads = []
        for hh in range(2):
            acc = jnp.concatenate([states[u][1] for u in range(len(units)) if units[u][0] == hh], axis=1)
            heads.append(acc[:V_DIM] / acc[V_DIM:V_DIM + 1])
        o_ref[0] = (jnp.concatenate(heads, axis=0).T * sg_ref[0]).astype(BF16)

    def pair(t, states):
        j = 2 * t
        states = list(states)
        for u in range(len(units)):
            produce(j + 1, 1, u)
            states[u] = consume(j, 0, u, states[u])
        for u in range(len(units)):
            produce(j + 2, 0, u)
            states[u] = consume(j + 1, 1, u, states[u])
        return tuple(states)

    init = tuple((jnp.full((1, cw), -jnp.inf, F32), jnp.zeros((V_DIM + SUM_ROWS, cw), F32)) for _ in units)
    pieces = FIRST_CHUNKS // len(units)
    for u in range(len(units)):
        for c in range(pieces):
            produce(0, 0, u, (c * tk // pieces, (c + 1) * tk // pieces))
            between(u * pieces + c)
    states = lax.fori_loop(0, qi // 2, pair, init)

    @pl.when(qi % 2 == 0)
    def _():
        finish([consume(qi, 0, u, states[u], diagonal=True) for u in range(len(units))])

    @pl.when(qi % 2 == 1)
    def _():
        out = []
        for u in range(len(units)):
            produce(qi, 1, u)
            out.append(consume(qi - 1, 0, u, states[u]))
        finish([consume(qi, 1, u, out[u], diagonal=True) for u in range(len(units))])


def _page_copies(pt_ref, cckv_hbm, ckrt_hbm, ckv_buf, krt_buf, sem, n_pages, row, sl, p):
    page = pt_ref[row * n_pages + p]
    cols = pl.ds(p * PAGE_SIZE if isinstance(p, int) else pl.multiple_of(p * PAGE_SIZE, PAGE_SIZE), PAGE_SIZE)
    return (pltpu.make_async_copy(cckv_hbm.at[0, page], ckv_buf.at[sl, cols], sem.at[0, sl]),
            pltpu.make_async_copy(ckrt_hbm.at[0, page], krt_buf.at[sl, :, cols], sem.at[1, sl]))


def _decode_row(pt_ref, qa_ref, qr_ref, ckvn_ref, krn_ref, wv_ref, sg_ref, cckv_hbm, ckrt_hbm,
                o_ref, ckv_buf, krt_buf, sem, *, row, slot, n_pages):
    def wait_body(p, _):
        for cp in _page_copies(pt_ref, cckv_hbm, ckrt_hbm, ckv_buf, krt_buf, sem, n_pages, row, slot, p):
            cp.wait()
        return 0
    lax.fori_loop(0, n_pages, wait_body, 0, unroll=8)

    qa = qa_ref[0].astype(BF16)
    qr = qr_ref[0].astype(BF16)
    ckv = ckv_buf[slot].astype(BF16)
    krt = krt_buf[slot].astype(BF16)
    s = (lax.dot_general(qa, ckv, _NT, preferred_element_type=F32)
         + jnp.dot(qr, krt, preferred_element_type=F32))
    ckvn = ckvn_ref[0].astype(BF16).astype(F32)
    krn = krn_ref[0].astype(BF16).astype(F32)
    s_n = (jnp.sum(qa.astype(F32) * ckvn, axis=1, keepdims=True)
           + jnp.sum(qr.astype(F32) * krn, axis=1, keepdims=True))
    m = jnp.maximum(jnp.max(s, axis=1, keepdims=True), s_n)
    p = jnp.exp2((s - m) * QK_PRESCALE)
    p_n = jnp.exp2((s_n - m) * QK_PRESCALE)
    l = jnp.sum(p, axis=1, keepdims=True) + p_n
    o_lat = (jnp.dot(p.astype(BF16), ckv, preferred_element_type=F32)
             + p_n.astype(BF16).astype(F32) * ckvn) / l
    o_all = _bdot(o_lat, wv_ref[...])
    hrow = lax.broadcasted_iota(jnp.int32, o_all.shape, 0)
    hcol = lax.broadcasted_iota(jnp.int32, o_all.shape, 1) // V_DIM
    att = jnp.sum(jnp.where(hrow == hcol, o_all, 0.0), axis=0, keepdims=True)
    o_ref[0] = (att * sg_ref[0]).astype(BF16)


def _attn_kernel(pt_ref, q_ref, k_ref, vt_ref, sgp_ref, qa_ref, qr_ref, ckvn_ref, krn_ref, wv_ref, sgs_ref,
                 cckv_hbm, ckrt_hbm, op_ref, os_ref, st00, st01, st10, st11, ckv_buf, krt_buf, sem,
                 *, tq, tk, n_pages):
    qi = pl.program_id(2)
    r = (pl.program_id(0) * pl.num_programs(1) + pl.program_id(1)) * pl.num_programs(2) + qi
    n_rows = pl.num_programs(0) * pl.num_programs(1) * pl.num_programs(2)
    slot = r % 2
    dma = (pt_ref, cckv_hbm, ckrt_hbm, ckv_buf, krt_buf, sem, n_pages)

    @pl.when(r == 0)
    def _():
        def body(p, _):
            for cp in _page_copies(*dma, 0, 0, p):
                cp.start()
            return 0
        lax.fori_loop(0, n_pages, body, 0, unroll=8)

    nxt = jnp.minimum(r + 1, n_rows - 1)
    assert n_pages % FIRST_CHUNKS == 0
    per_chunk = n_pages // FIRST_CHUNKS

    def issue(c):
        for p in range(c * per_chunk, (c + 1) * per_chunk):
            for cp in _page_copies(*dma, nxt, 1 - slot, p):
                cp.start()

    _flash_tile(qi, q_ref, k_ref, vt_ref, sgp_ref, op_ref, ((st00, st01), (st10, st11)), issue, tq=tq, tk=tk)
    _decode_row(pt_ref, qa_ref, qr_ref, ckvn_ref, krn_ref, wv_ref, sgs_ref, cckv_hbm, ckrt_hbm,
                os_ref, ckv_buf, krt_buf, sem, row=r, slot=slot, n_pages=n_pages)

    @pl.when(r == n_rows - 1)
    def _():
        def body(p, _):
            for cp in _page_copies(*dma, nxt, 1 - slot, p):
                cp.wait()
            return 0
        lax.fori_loop(0, n_pages, body, 0, unroll=8)


def _attention(q, k, vt, sg_p, page_table, qabs, qrope, ckv_new, kr_new, wv, sg_s, cache_ckv, cache_krt,
               tq=512, tk=512):
    b, s, _ = q.shape
    n_pairs = N_HEADS // 2
    nq = s // tq
    n, n_pages = page_table.shape
    assert n == b * n_pairs * nq, "one sample row per prompt attention step"
    past = n_pages * PAGE_SIZE
    row = lambda bi, hp, qi: (bi * n_pairs + hp) * nq + qi
    per_r = lambda w: pl.BlockSpec((1, 1, w), lambda bi, hp, qi, pt: (row(bi, hp, qi), 0, 0))
    per_h = lambda w: pl.BlockSpec((1, N_HEADS, w), lambda bi, hp, qi, pt: (row(bi, hp, qi), 0, 0))
    grid_spec = pltpu.PrefetchScalarGridSpec(
        num_scalar_prefetch=1,
        grid=(b, n_pairs, nq),
        in_specs=[pl.BlockSpec((1, tq, 2 * HEAD_PAD), lambda bi, hp, qi, pt: (bi, qi, hp)),
                  pl.BlockSpec((1, s, 2 * HEAD_PAD), lambda bi, hp, qi, pt: (bi, 0, hp)),
                  pl.BlockSpec((1, 2 * V_DIM, s), lambda bi, hp, qi, pt: (bi, hp, 0)),
                  pl.BlockSpec((1, tq, 2 * V_DIM), lambda bi, hp, qi, pt: (bi, qi, hp)),
                  per_h(KV_LORA), per_h(ROPE_DIM), per_r(KV_LORA), per_r(ROPE_DIM),
                  pl.BlockSpec(wv.shape, lambda bi, hp, qi, pt: (0, 0)),
                  per_r(D_ATT),
                  pl.BlockSpec(memory_space=pl.ANY),
                  pl.BlockSpec(memory_space=pl.ANY)],
        out_specs=[pl.BlockSpec((1, tq, 2 * V_DIM), lambda bi, hp, qi, pt: (bi, qi, hp)), per_r(D_ATT)],
        scratch_shapes=[pltpu.VMEM((tk, tq), F32)] * 4 + [
            pltpu.VMEM((2, past, KV_LORA), F32),
            pltpu.VMEM((2, ROPE_DIM, past), F32),
            pltpu.SemaphoreType.DMA((2, 2))],
    )
    return pl.pallas_call(
        functools.partial(_attn_kernel, tq=tq, tk=tk, n_pages=n_pages),
        grid_spec=grid_spec,
        out_shape=[jax.ShapeDtypeStruct((b, s, D_ATT), BF16), jax.ShapeDtypeStruct((n, 1, D_ATT), BF16)],
        compiler_params=pltpu.CompilerParams(dimension_semantics=("arbitrary", "arbitrary", "arbitrary"),
                                             vmem_limit_bytes=VMEM_LIMIT),
        name="attention",
    )(page_table.reshape(-1), q, k, vt, sg_p, qabs, qrope, ckv_new, kr_new, wv, sg_s, cache_ckv, cache_krt)


def _combine_kernel(x_ref, gate_ref, ag_ref, mp_ref, wo_ref, fnw_ref, y_ref):
    proj = (jnp.dot(ag_ref[0], wo_ref[0:D_ATT, :], preferred_element_type=F32)
            + jnp.dot(mp_ref[0], wo_ref[D_ATT:D_ATT + D_POOL, :], preferred_element_type=F32))
    y = x_ref[0] + gate_ref[0] * proj
    y_ref[0] = _rms(y, fnw_ref[...])


def _combine(x, gate, ag, mp, wo, fnw, tm):
    b, s, d = x.shape
    gr = gate.shape[1]
    gate_spec = (pl.BlockSpec((1, 1, d), lambda bi, i: (bi, 0, 0)) if gr == 1
                 else pl.BlockSpec((1, tm, d), lambda bi, i: (bi, i, 0)))
    row = lambda w: pl.BlockSpec((1, tm, w), lambda bi, i: (bi, i, 0))
    return pl.pallas_call(
        _combine_kernel,
        grid=(b, s // tm),
        in_specs=[row(d), gate_spec, row(D_ATT), row(D_POOL), _full(wo.shape), _full(fnw.shape)],
        out_specs=row(d),
        out_shape=jax.ShapeDtypeStruct((b, s, d), F32),
        compiler_params=pltpu.CompilerParams(dimension_semantics=("arbitrary", "arbitrary"),
                                             vmem_limit_bytes=VMEM_LIMIT),
        name="combine",
    )(x, gate, ag, mp, wo, fnw)


def _rope_tables(offsets, bases):
    inv = ROPE_BASE ** (-jnp.arange(0, ROPE_DIM, 2, dtype=F32) / ROPE_DIM)
    inv = jnp.pad(jnp.concatenate([inv, inv]), (0, HEAD_PAD - ROPE_DIM))
    lanes = (jnp.arange(HEAD_PAD) < ROPE_DIM).astype(F32)
    a_off = offsets.astype(F32)[:, None] * inv[None, :]
    a_base = bases.astype(F32)[:, None, None] * inv[None, None, :]
    return jnp.cos(a_off) * lanes, jnp.sin(a_off) * lanes, jnp.cos(a_base), jnp.sin(a_base)


def _rot_half(w):
    half = w.shape[-1] // 2
    return jnp.concatenate([-w[..., half:], w[..., :half]], axis=-1)


def _pack_weights(norm_w, w_in, q_norm_w, w_uq, kv_norm_w, w_uk, w_uv, w_pool, b_pool, pool_scale):
    d = w_in.shape[0]
    off_kr = Q_LORA + KV_LORA
    wt = jnp.swapaxes(w_in, 0, 1).astype(BF16)
    krc = wt[off_kr:off_kr + ROPE_DIM]
    win = jnp.concatenate([wt[:off_kr], krc, -krc[ROPE_DIM // 2:], krc[:ROPE_DIM // 2],
                           jnp.zeros((HEAD_PAD - 2 * ROPE_DIM, d), BF16), wt[off_kr + ROPE_DIM:]], axis=0)
    pad_q = HEAD_PAD - QK_DIM
    wq = jnp.pad(w_uq, ((0, 0), (0, 0), (0, pad_q))).reshape(Q_LORA, N_HEADS * HEAD_PAD)
    wqr = jnp.pad(_rot_half(w_uq[..., NOPE_DIM:]), ((0, 0), (0, 0), (NOPE_DIM, pad_q)))
    wqr = wqr.reshape(Q_LORA, N_HEADS * HEAD_PAD)
    wk_nope = jnp.pad(w_uk, ((0, 0), (0, 0), (0, HEAD_PAD - NOPE_DIM))).reshape(KV_LORA, N_HEADS * HEAD_PAD)
    place = jnp.pad(jnp.eye(ROPE_DIM, dtype=F32), ((0, HEAD_PAD - ROPE_DIM), (NOPE_DIM, pad_q)))
    wk = jnp.concatenate([wk_nope, jnp.tile(place, (1, N_HEADS))], axis=0)
    wv = w_uv.reshape(KV_LORA, N_HEADS * V_DIM).astype(BF16)
    shared = (norm_w.reshape(1, -1), win, q_norm_w.reshape(1, -1), kv_norm_w.reshape(1, -1),
              wq.astype(BF16), wqr.astype(BF16), w_pool.astype(BF16), b_pool.reshape(1, -1),
              pool_scale.reshape(1, -1))
    return shared, wk.astype(BF16), wv


def kernel(x_prompt, x_sample, c_prompt, c_sample, cache_ckv, cache_krope, state_pool, page_table, ada_w, ada_b, norm_w, w_in, q_norm_w, w_uq, kv_norm_w, w_uk, w_uv, w_pool, b_pool, pool_scale, w_out, final_norm_w):
    assert ada_w.shape[0] == 1, "single-layer trunk only"
    b, s, d = x_prompt.shape
    n = x_sample.shape[0]
    assert x_sample.shape[1] == 1
    past_len = page_table.shape[1] * PAGE_SIZE

    c_all = jnp.concatenate([c_prompt, c_sample], axis=0)
    rows = -(-c_all.shape[0] // 8) * 8
    c_all = jnp.pad(c_all, ((0, rows - c_all.shape[0]), (0, 0)))
    mod = _ada(c_all, ada_w[0], ada_b[0].reshape(1, -1))
    mod_p = mod[:b].reshape(b, 3, d)
    mod_s = mod[b:b + n]

    wts, wk, wv = _pack_weights(norm_w[0], w_in[0], q_norm_w[0], w_uq[0], kv_norm_w[0], w_uk[0], w_uv[0],
                                w_pool[0], b_pool[0], pool_scale[0])
    wo = w_out[0].astype(BF16)
    fnw = final_norm_w.reshape(1, -1)

    tabs_p = _rope_tables(jnp.arange(ROW_TILE), jnp.arange(0, s, ROW_TILE))
    q, k, vt, ckv_p, krt_p, sg_p, mp_p, utail = _inproj_p(x_prompt, mod_p, wts, wk, wv.T, tabs_p, tm=ROW_TILE)

    tabs_s = _rope_tables(jnp.zeros((n,)), jnp.full((1,), past_len))
    wukt = jnp.pad(jnp.transpose(w_uk[0], (1, 2, 0)), ((0, 0), (0, HEAD_PAD - NOPE_DIM), (0, 0))).astype(BF16)
    state_t = jnp.transpose(state_pool[0], (1, 0, 2))
    qabs, qrope, ckv_s, kr_s, sg_s, mp_s, u_s = _inproj_s(x_sample[:, 0, :], mod_s, wts, tabs_s, state_t, wukt)
    qrope = qrope.reshape(n, N_HEADS, HEAD_PAD)[:, :, NOPE_DIM:QK_DIM]
    ag_p, ag_s = _attention(q, k, vt, sg_p, page_table, qabs.reshape(n, N_HEADS, KV_LORA), qrope,
                            ckv_s[:, None, :], kr_s[:, None, :], wv, sg_s[:, None, :],
                            cache_ckv, jnp.swapaxes(cache_krope, 2, 3))
    y_prompt = _combine(x_prompt, mod_p[:, 2:3, :], ag_p, mp_p, wo, fnw, tm=2 * ROW_TILE)
    y_sample = _combine(x_sample.reshape(1, n, d), mod_s[None, :, 2 * d:], ag_s.reshape(1, n, D_ATT),
                        mp_s[None], wo, fnw, tm=n).reshape(n, 1, d)

    new_pool_p = utail[:, HALO - POOL_PREV:, :]
    new_pool_s = jnp.transpose(jnp.concatenate([state_t[1:], u_s[None]], axis=0), (1, 0, 2))
    return (y_prompt, y_sample,
            ckv_p[None], jnp.swapaxes(krt_p, 1, 2)[None], new_pool_p[None],
            ckv_s[:, None, :][None], kr_s[:, None, :][None], new_pool_s[None])
```

```python
import functools
import math

import jax
import jax.numpy as jnp
from jax import lax
from jax.experimental import pallas as pl
from jax.experimental.pallas import tpu as pltpu

F32 = jnp.float32
BF16 = jnp.bfloat16

N_HEADS = 8
NOPE_DIM = 64
ROPE_DIM = 32
QK_DIM = NOPE_DIM + ROPE_DIM
V_DIM = 64
Q_LORA = 256
KV_LORA = 128
D_POOL = 512
D_ATT = 512
POOL_WINDOWS = (2, 4, 8, 16)
POOL_GROUP = 128
POOL_PREV = 15
PAGE_SIZE = 128
ROPE_BASE = 10000.0
EPS = 1e-6
SM_SCALE = QK_DIM ** -0.5
LOG2E = math.log2(math.e)
QK_PRESCALE = SM_SCALE * LOG2E
_NT = (((1,), (1,)), ((), ()))

HEAD_PAD = 128
COL_SPLIT = 1
DEC_CHUNKS = 4
FIRST_CHUNKS = 8
SUM_ROWS = 16
HALO = 16
C_Q, C_KV, C_KR, C_GA, C_U, C_GP, C_END = 0, 256, 384, 512, 1024, 1536, 2048

ROW_TILE = 512
ROW_PARTS = 2
VMEM_LIMIT = 56 * 1024 * 1024


def _rms(x, w):
    return (x * lax.rsqrt(jnp.mean(x * x, axis=-1, keepdims=True) + EPS)) * w


def _silu(x):
    return x * jax.nn.sigmoid(x)


def _bdot(a, b):
    return jnp.dot(a.astype(BF16), b, preferred_element_type=F32)


def _ada_kernel(c_ref, w_ref, b_ref, o_ref):
    o_ref[...] = _bdot(_silu(c_ref[...]), w_ref[...].astype(BF16)) + b_ref[...]


def _ada(c_all, ada_w, ada_b):
    rows, d = c_all.shape
    n = ada_w.shape[1]
    bn = 1024
    return pl.pallas_call(
        _ada_kernel,
        grid=(n // bn,),
        in_specs=[pl.BlockSpec((rows, d), lambda j: (0, 0)),
                  pl.BlockSpec((d, bn), lambda j: (0, j)),
                  pl.BlockSpec((1, bn), lambda j: (0, j))],
        out_specs=pl.BlockSpec((rows, bn), lambda j: (0, j)),
        out_shape=jax.ShapeDtypeStruct((rows, n), F32),
        compiler_params=pltpu.CompilerParams(dimension_semantics=("arbitrary",)),
        name="ada",
    )(c_all, ada_w, ada_b)


def _project_z(x, shift, scale, nw, win):
    h = _rms(x, nw) * (1.0 + scale) + shift
    return lax.dot_general(h.astype(BF16), win, _NT, preferred_element_type=F32)


def _project_rest(z, qnw, kvnw, wq, wqr, cr, sr, cb, sb):
    cosk = cb * cr - sb * sr
    sink = sb * cr + cb * sr
    nope = (lax.broadcasted_iota(jnp.int32, (1, HEAD_PAD), 1) < NOPE_DIM).astype(F32)
    cosq = pltpu.roll(cosk, NOPE_DIM, axis=1) + nope
    sinq = pltpu.roll(sink, NOPE_DIM, axis=1)
    q_lat = _rms(z[:, C_Q:C_KV], qnw)
    ckv = _rms(z[:, C_KV:C_KR], kvnw)
    krc = z[:, C_KR:C_GA]
    kr128 = krc * cosk + pltpu.roll(krc, HEAD_PAD - ROPE_DIM, axis=1) * sink
    ql = q_lat.astype(BF16)
    qa = jnp.dot(ql, wq, preferred_element_type=F32)
    qb = jnp.dot(ql, wqr, preferred_element_type=F32)
    return qa, qb, ckv, kr128, z[:, C_GA:C_U], z[:, C_U:C_GP], z[:, C_GP:C_END], cosq, sinq


def _pool_out(pooled_g, g, wpool_ref, bpool_ref, pscale_ref, g_pool):
    lo, hi = g * POOL_GROUP, (g + 1) * POOL_GROUP
    y = _bdot(pooled_g, wpool_ref[g]) + bpool_ref[:, lo:hi]
    y = y * pscale_ref[:, lo:hi]
    return y * _silu(g_pool[:, lo:hi])


def _inproj_p_kernel(x_ref, mod_ref, nw_ref, win_ref, qnw_ref, kvnw_ref, wq_ref, wqr_ref,
                     wpool_ref, bpool_ref, pscale_ref, wk_ref, wvt_ref, cr_ref, sr_ref, cb_ref, sb_ref,
                     q_ref, k_ref, vt_ref, ckv_ref, krt_ref, sg_ref, mp_ref, utail_ref, uext_ref, *, tm):
    i = pl.program_id(1)
    shift = mod_ref[0, 0:1, :]
    scale = mod_ref[0, 1:2, :]

    @pl.when(i == 0)
    def _():
        uext_ref[0:HALO, :] = jnp.zeros((HALO, D_POOL), F32)

    hr = tm // ROW_PARTS
    zs = []
    for part in range(ROW_PARTS):
        rows = slice(part * hr, (part + 1) * hr)
        z = _project_z(x_ref[0, rows, :], shift, scale, nw_ref[...], win_ref[...])
        uext_ref[HALO + part * hr:HALO + (part + 1) * hr, :] = z[:, C_U:C_GP]
        zs.append(z)

    for part in range(ROW_PARTS):
        r0 = part * hr
        rows = slice(r0, r0 + hr)
        qa, qb, ckv, kr128, g_att, u, g_pool, cosq, sinq = _project_rest(
            zs[part], qnw_ref[...], kvnw_ref[...], wq_ref[...], wqr_ref[...],
            cr_ref[rows, :], sr_ref[rows, :], cb_ref[0], sb_ref[0])
        cosq, sinq = cosq * QK_PRESCALE, sinq * QK_PRESCALE
        for h in range(N_HEADS):
            sl = slice(h * HEAD_PAD, (h + 1) * HEAD_PAD)
            q_ref[0, rows, sl] = (qa[:, sl] * cosq + qb[:, sl] * sinq).astype(BF16)
        ckv_b = ckv.astype(BF16)
        k = jnp.dot(jnp.concatenate([ckv_b, kr128.astype(BF16)], axis=1), wk_ref[...],
                    preferred_element_type=F32)
        k_ref[0, rows, :] = k.astype(BF16)
        vt_ref[0, :, rows] = lax.dot_general(wvt_ref[...], ckv_b, _NT, preferred_element_type=F32).astype(BF16)
        ckv_ref[0, rows, :] = ckv
        krt_ref[0, :, rows] = kr128.T[:ROPE_DIM, :]
        sg_ref[0, rows, :] = _silu(g_att)
        pos = (i * tm + r0 + lax.broadcasted_iota(jnp.int32, (hr, 1), 0)).astype(F32)
        for g, w in enumerate(POOL_WINDOWS):
            lo, hi = g * POOL_GROUP, (g + 1) * POOL_GROUP
            wsum = u[:, lo:hi]
            for d in range(1, w):
                wsum = wsum + uext_ref[HALO + r0 - d:HALO + r0 - d + hr, lo:hi]
            cnt = jnp.minimum(pos + 1.0, float(w))
            pooled = wsum / cnt - u[:, lo:hi]
            mp_ref[0, rows, lo:hi] = _pool_out(pooled, g, wpool_ref, bpool_ref, pscale_ref, g_pool).astype(BF16)

    u_last = zs[-1][hr - HALO:hr, C_U:C_GP]
    uext_ref[0:HALO, :] = u_last

    @pl.when(i == pl.num_programs(1) - 1)
    def _():
        utail_ref[0] = u_last


def _full(shape):
    nd = len(shape)
    return pl.BlockSpec(shape, lambda *_: (0,) * nd)


def _inproj_p(x, mod3, wts, wk, wvt, tabs, tm=512):
    b, s, d = x.shape
    nt = s // tm
    row = lambda w: pl.BlockSpec((1, tm, w), lambda bi, i: (bi, i, 0))
    rtab = _full((tm, HEAD_PAD))
    btab = pl.BlockSpec((1, 1, HEAD_PAD), lambda bi, i: (i, 0, 0))
    in_specs = [row(d), pl.BlockSpec((1, 3, d), lambda bi, i: (bi, 0, 0))]
    in_specs += [_full(w.shape) for w in (*wts, wk, wvt)]
    in_specs += [rtab, rtab, btab, btab]
    out_shape = [
        jax.ShapeDtypeStruct((b, s, N_HEADS * HEAD_PAD), BF16),
        jax.ShapeDtypeStruct((b, s, N_HEADS * HEAD_PAD), BF16),
        jax.ShapeDtypeStruct((b, D_ATT, s), BF16),
        jax.ShapeDtypeStruct((b, s, KV_LORA), F32),
        jax.ShapeDtypeStruct((b, ROPE_DIM, s), F32),
        jax.ShapeDtypeStruct((b, s, D_ATT), F32),
        jax.ShapeDtypeStruct((b, s, D_POOL), BF16),
        jax.ShapeDtypeStruct((b, HALO, D_POOL), F32),
    ]
    out_specs = [row(N_HEADS * HEAD_PAD), row(N_HEADS * HEAD_PAD),
                 pl.BlockSpec((1, D_ATT, tm), lambda bi, i: (bi, 0, i)), row(KV_LORA),
                 pl.BlockSpec((1, ROPE_DIM, tm), lambda bi, i: (bi, 0, i)),
                 row(D_ATT), row(D_POOL), pl.BlockSpec((1, HALO, D_POOL), lambda bi, i: (bi, 0, 0))]
    return pl.pallas_call(
        functools.partial(_inproj_p_kernel, tm=tm),
        grid=(b, nt),
        in_specs=in_specs,
        out_specs=out_specs,
        out_shape=out_shape,
        scratch_shapes=[pltpu.VMEM((tm + HALO, D_POOL), F32)],
        compiler_params=pltpu.CompilerParams(dimension_semantics=("arbitrary", "arbitrary"),
                                             vmem_limit_bytes=VMEM_LIMIT),
        name="inproj_p",
    )(x, mod3, *wts, wk, wvt, *tabs)


def _inproj_s_kernel(x_ref, mod_ref, nw_ref, win_ref, qnw_ref, kvnw_ref, wq_ref, wqr_ref,
                     wpool_ref, bpool_ref, pscale_ref, cr_ref, sr_ref, cb_ref, sb_ref,
                     state_ref, wukt_ref,
                     qabs_ref, qrope_ref, ckv_ref, kr_ref, sg_ref, mp_ref, u_ref):
    d = x_ref.shape[1]
    shift = mod_ref[:, 0:d]
    scale = mod_ref[:, d:2 * d]
    z = _project_z(x_ref[...], shift, scale, nw_ref[...], win_ref[...])
    qa, qb, ckv, kr128, g_att, u, g_pool, cosq, sinq = _project_rest(
        z, qnw_ref[...], kvnw_ref[...], wq_ref[...], wqr_ref[...], cr_ref[...], sr_ref[...], cb_ref[0], sb_ref[0])
    for h in range(N_HEADS):
        sl = slice(h * HEAD_PAD, (h + 1) * HEAD_PAD)
        qrope_ref[:, sl] = qa[:, sl] * cosq + qb[:, sl] * sinq
        qabs_ref[:, sl] = _bdot(qa[:, sl], wukt_ref[h])
    ckv_ref[...] = ckv
    kr_ref[...] = kr128[:, :ROPE_DIM]
    sg_ref[...] = _silu(g_att)
    u_ref[...] = u
    for g, w in enumerate(POOL_WINDOWS):
        lo, hi = g * POOL_GROUP, (g + 1) * POOL_GROUP
        wsum = u[:, lo:hi]
        for j in range(1, w):
            wsum = wsum + state_ref[POOL_PREV - j, :, lo:hi]
        pooled = wsum / float(w) - u[:, lo:hi]
        mp_ref[:, lo:hi] = _pool_out(pooled, g, wpool_ref, bpool_ref, pscale_ref, g_pool).astype(BF16)


def _inproj_s(x, mod, wts, tabs, state, wukt):
    n, d = x.shape
    args = (x, mod, *wts, *tabs, state, wukt)
    out_shape = [
        jax.ShapeDtypeStruct((n, N_HEADS * HEAD_PAD), F32),
        jax.ShapeDtypeStruct((n, N_HEADS * HEAD_PAD), F32),
        jax.ShapeDtypeStruct((n, KV_LORA), F32),
        jax.ShapeDtypeStruct((n, ROPE_DIM), F32),
        jax.ShapeDtypeStruct((n, D_ATT), F32),
        jax.ShapeDtypeStruct((n, D_POOL), BF16),
        jax.ShapeDtypeStruct((n, D_POOL), F32),
    ]
    return pl.pallas_call(
        _inproj_s_kernel,
        in_specs=[_full(a.shape) for a in args],
        out_specs=[_full(o.shape) for o in out_shape],
        out_shape=out_shape,
        grid=(1,),
        compiler_params=pltpu.CompilerParams(dimension_semantics=("arbitrary",), vmem_limit_bytes=VMEM_LIMIT),
        name="inproj_s",
    )(*args)


def _flash_tile(qi, q_ref, k_ref, vt_ref, sg_ref, o_ref, st_ref, between, side, *, tq, tk):
    assert tq == tk
    cw = tq // COL_SPLIT
    units = [(hh, cs) for hh in range(2) for cs in range(COL_SPLIT)]
    qs = [q_ref[0, cs * cw:(cs + 1) * cw, hh * HEAD_PAD:(hh + 1) * HEAD_PAD] for hh, cs in units]

    def produce(j, slot, u, rows=None):
        hh, cs = units[u]
        r0, r1 = (0, tk) if rows is None else rows
        kb = k_ref[0, pl.ds(pl.multiple_of(j * tk, tk) + r0, r1 - r0), hh * HEAD_PAD:(hh + 1) * HEAD_PAD]
        st_ref[slot][hh][r0:r1, cs * cw:(cs + 1) * cw] = lax.dot_general(kb, qs[u], _NT,
                                                                         preferred_element_type=F32)

    def consume(j, slot, u, state, diagonal=False):
        hh, cs = units[u]
        m, acc = state
        st = st_ref[slot][hh][:, cs * cw:(cs + 1) * cw]
        if diagonal:
            kpos = lax.broadcasted_iota(jnp.int32, (tk, cw), 0)
            qpos = cs * cw + lax.broadcasted_iota(jnp.int32, (tk, cw), 1)
            st = jnp.where(kpos <= qpos, st, -jnp.inf)
        vtb = vt_ref[0, hh * V_DIM:(hh + 1) * V_DIM, pl.ds(pl.multiple_of(j * tk, tk), tk)]
        vtb = jnp.concatenate([vtb, jnp.ones((SUM_ROWS, tk), BF16)], axis=0)
        m_new = jnp.maximum(m, jnp.max(st, axis=0, keepdims=True))
        alpha = jnp.exp2(m - m_new)
        pt = jnp.exp2(st - m_new).astype(BF16)
        return m_new, alpha * acc + jnp.dot(vtb, pt, preferred_element_type=F32)

    def finish(states):
        heads = []
        for hh in range(2):
            acc = jnp.concatenate([states[u][1] for u in range(len(units)) if units[u][0] == hh], axis=1)
            heads.append(acc[:V_DIM] / acc[V_DIM:V_DIM + 1])
        o_ref[0] = (jnp.concatenate(heads, axis=0).T * sg_ref[0]).astype(BF16)

    def pair(t, states):
        j = 2 * t
        states = list(states)
        for u in range(len(units)):
            produce(j + 1, 1, u)
            states[u] = consume(j, 0, u, states[u])
        for u in range(len(units)):
            produce(j + 2, 0, u)
            states[u] = consume(j + 1, 1, u, states[u])
        return tuple(states)

    init = tuple((jnp.full((1, cw), -jnp.inf, F32), jnp.zeros((V_DIM + SUM_ROWS, cw), F32)) for _ in units)
    pieces = FIRST_CHUNKS // len(units)
    for u in range(len(units)):
        for c in range(pieces):
            produce(0, 0, u, (c * tk // pieces, (c + 1) * tk // pieces))
            between(u * pieces + c)
    states = lax.fori_loop(0, qi // 2, pair, init)

    def run_interleaved(steps, pieces):
        per = -(-len(pieces) // len(steps))
        for n, step in enumerate(steps):
            for piece in pieces[n * per:(n + 1) * per]:
                piece()
            step()

    @pl.when(qi % 2 == 0)
    def _():
        wait, pieces = side(0)
        wait()
        done = [None] * len(units)

        def last(u):
            def run():
                done[u] = consume(qi, 0, u, states[u], diagonal=True)
            return run
        run_interleaved([last(u) for u in range(len(units))], pieces)
        finish(done)

    @pl.when(qi % 2 == 1)
    def _():
        wait, pieces = side(1)
        wait()
        mid, done = [None] * len(units), [None] * len(units)

        def before_last(u):
            def run():
                produce(qi, 1, u)
                mid[u] = consume(qi - 1, 0, u, states[u])
            return run

        def last(u):
            def run():
                done[u] = consume(qi, 1, u, mid[u], diagonal=True)
            return run
        run_interleaved([before_last(u) for u in range(len(units))] + [last(u) for u in range(len(units))], pieces)
        finish(done)


def _page_copies(pt_ref, cckv_hbm, ckrt_hbm, ckv_buf, krt_buf, sem, n_pages, row, sl, p):
    page = pt_ref[row * n_pages + p]
    cols = pl.ds(p * PAGE_SIZE if isinstance(p, int) else pl.multiple_of(p * PAGE_SIZE, PAGE_SIZE), PAGE_SIZE)
    return (pltpu.make_async_copy(cckv_hbm.at[0, page], ckv_buf.at[sl, cols], sem.at[0, sl]),
            pltpu.make_async_copy(ckrt_hbm.at[0, page], krt_buf.at[sl, :, cols], sem.at[1, sl]))


def _decode_stages(pt_ref, qa_ref, qr_ref, ckvn_ref, krn_ref, wv_ref, sg_ref, cckv_hbm, ckrt_hbm,
                   o_ref, ckv_buf, krt_buf, sem, *, row, slot, n_pages):
    def wait():
        def wait_body(p, _):
            for cp in _page_copies(pt_ref, cckv_hbm, ckrt_hbm, ckv_buf, krt_buf, sem, n_pages, row, slot, p):
                cp.wait()
            return 0
        lax.fori_loop(0, n_pages, wait_body, 0, unroll=8)

    past = ckv_buf.shape[1]
    ck = past // DEC_CHUNKS
    env = {"ckv": [], "s": []}

    def scores(c):
        def run():
            if c == 0:
                env["qa"] = qa_ref[0].astype(BF16)
                env["qr"] = qr_ref[0].astype(BF16)
            ckv = ckv_buf[slot, c * ck:(c + 1) * ck, :].astype(BF16)
            krt = krt_buf[slot, :, c * ck:(c + 1) * ck].astype(BF16)
            env["ckv"].append(ckv)
            env["s"].append(lax.dot_general(env["qa"], ckv, _NT, preferred_element_type=F32)
                            + jnp.dot(env["qr"], krt, preferred_element_type=F32))
        return run

    def values(c):
        def run():
            if c == 0:
                qa, qr = env["qa"], env["qr"]
                s = jnp.concatenate(env["s"], axis=1)
                ckvn = ckvn_ref[0].astype(BF16).astype(F32)
                krn = krn_ref[0].astype(BF16).astype(F32)
                s_n = (jnp.sum(qa.astype(F32) * ckvn, axis=1, keepdims=True)
                       + jnp.sum(qr.astype(F32) * krn, axis=1, keepdims=True))
                m = jnp.maximum(jnp.max(s, axis=1, keepdims=True), s_n)
                p = jnp.exp2((s - m) * QK_PRESCALE)
                p_n = jnp.exp2((s_n - m) * QK_PRESCALE)
                env["l"] = jnp.sum(p, axis=1, keepdims=True) + p_n
                env["p"] = p.astype(BF16)
                env["o"] = p_n.astype(BF16).astype(F32) * ckvn
            env["o"] = env["o"] + jnp.dot(env["p"][:, c * ck:(c + 1) * ck], env["ckv"][c],
                                          preferred_element_type=F32)
            if c == DEC_CHUNKS - 1:
                o_all = _bdot(env["o"] / env["l"], wv_ref[...])
                hrow = lax.broadcasted_iota(jnp.int32, o_all.shape, 0)
                hcol = lax.broadcasted_iota(jnp.int32, o_all.shape, 1) // V_DIM
                att = jnp.sum(jnp.where(hrow == hcol, o_all, 0.0), axis=0, keepdims=True)
                o_ref[0] = (att * sg_ref[0]).astype(BF16)
        return run

    return wait, [scores(c) for c in range(DEC_CHUNKS)] + [values(c) for c in range(DEC_CHUNKS)]


def _attn_kernel(pt_ref, q_ref, k_ref, vt_ref, sgp_ref, qa_ref, qr_ref, ckvn_ref, krn_ref, wv_ref, sgs_ref,
                 cckv_hbm, ckrt_hbm, op_ref, os_ref, st00, st01, st10, st11, ckv_buf, krt_buf, sem,
                 *, tq, tk, n_pages):
    qi = pl.program_id(2)
    r = (pl.program_id(0) * pl.num_programs(1) + pl.program_id(1)) * pl.num_programs(2) + qi
    n_rows = pl.num_programs(0) * pl.num_programs(1) * pl.num_programs(2)
    slot = r % 2
    dma = (pt_ref, cckv_hbm, ckrt_hbm, ckv_buf, krt_buf, sem, n_pages)

    @pl.when(r == 0)
    def _():
        def body(p, _):
            for cp in _page_copies(*dma, 0, 0, p):
                cp.start()
            return 0
        lax.fori_loop(0, n_pages, body, 0, unroll=8)

    nxt = jnp.minimum(r + 1, n_rows - 1)
    assert n_pages % FIRST_CHUNKS == 0
    per_chunk = n_pages // FIRST_CHUNKS

    def issue(c):
        for p in range(c * per_chunk, (c + 1) * per_chunk):
            for cp in _page_copies(*dma, nxt, 1 - slot, p):
                cp.start()

    def row_stages(parity):
        return _decode_stages(pt_ref, qa_ref, qr_ref, ckvn_ref, krn_ref, wv_ref, sgs_ref, cckv_hbm, ckrt_hbm,
                              os_ref, ckv_buf, krt_buf, sem, row=r, slot=parity, n_pages=n_pages)

    _flash_tile(qi, q_ref, k_ref, vt_ref, sgp_ref, op_ref, ((st00, st01), (st10, st11)), issue, row_stages,
                tq=tq, tk=tk)

    @pl.when(r == n_rows - 1)
    def _():
        def body(p, _):
            for cp in _page_copies(*dma, nxt, 1 - slot, p):
                cp.wait()
            return 0
        lax.fori_loop(0, n_pages, body, 0, unroll=8)


def _attention(q, k, vt, sg_p, page_table, qabs, qrope, ckv_new, kr_new, wv, sg_s, cache_ckv, cache_krt,
               tq=512, tk=512):
    b, s, _ = q.shape
    n_pairs = N_HEADS // 2
    nq = s // tq
    n, n_pages = page_table.shape
    assert n == b * n_pairs * nq, "one sample row per prompt attention step"
    assert nq % 2 == 0, "row parity must equal q-tile parity"
    past = n_pages * PAGE_SIZE
    row = lambda bi, hp, qi: (bi * n_pairs + hp) * nq + qi
    per_r = lambda w: pl.BlockSpec((1, 1, w), lambda bi, hp, qi, pt: (row(bi, hp, qi), 0, 0))
    per_h = lambda w: pl.BlockSpec((1, N_HEADS, w), lambda bi, hp, qi, pt: (row(bi, hp, qi), 0, 0))
    grid_spec = pltpu.PrefetchScalarGridSpec(
        num_scalar_prefetch=1,
        grid=(b, n_pairs, nq),
        in_specs=[pl.BlockSpec((1, tq, 2 * HEAD_PAD), lambda bi, hp, qi, pt: (bi, qi, hp)),
                  pl.BlockSpec((1, s, 2 * HEAD_PAD), lambda bi, hp, qi, pt: (bi, 0, hp)),
                  pl.BlockSpec((1, 2 * V_DIM, s), lambda bi, hp, qi, pt: (bi, hp, 0)),
                  pl.BlockSpec((1, tq, 2 * V_DIM), lambda bi, hp, qi, pt: (bi, qi, hp)),
                  per_h(KV_LORA), per_h(ROPE_DIM), per_r(KV_LORA), per_r(ROPE_DIM),
                  pl.BlockSpec(wv.shape, lambda bi, hp, qi, pt: (0, 0)),
                  per_r(D_ATT),
                  pl.BlockSpec(memory_space=pl.ANY),
                  pl.BlockSpec(memory_space=pl.ANY)],
        out_specs=[pl.BlockSpec((1, tq, 2 * V_DIM), lambda bi, hp, qi, pt: (bi, qi, hp)), per_r(D_ATT)],
        scratch_shapes=[pltpu.VMEM((tk, tq), F32)] * 4 + [
            pltpu.VMEM((2, past, KV_LORA), F32),
            pltpu.VMEM((2, ROPE_DIM, past), F32),
            pltpu.SemaphoreType.DMA((2, 2))],
    )
    return pl.pallas_call(
        functools.partial(_attn_kernel, tq=tq, tk=tk, n_pages=n_pages),
        grid_spec=grid_spec,
        out_shape=[jax.ShapeDtypeStruct((b, s, D_ATT), BF16), jax.ShapeDtypeStruct((n, 1, D_ATT), BF16)],
        compiler_params=pltpu.CompilerParams(dimension_semantics=("arbitrary", "arbitrary", "arbitrary"),
                                             vmem_limit_bytes=VMEM_LIMIT),
        name="attention",
    )(page_table.reshape(-1), q, k, vt, sg_p, qabs, qrope, ckv_new, kr_new, wv, sg_s, cache_ckv, cache_krt)


def _combine_kernel(x_ref, gate_ref, ag_ref, mp_ref, wo_ref, fnw_ref, y_ref):
    proj = (jnp.dot(ag_ref[0], wo_ref[0:D_ATT, :], preferred_element_type=F32)
            + jnp.dot(mp_ref[0], wo_ref[D_ATT:D_ATT + D_POOL, :], preferred_element_type=F32))
    y = x_ref[0] + gate_ref[0] * proj
    y_ref[0] = _rms(y, fnw_ref[...])


def _combine(x, gate, ag, mp, wo, fnw, tm):
    b, s, d = x.shape
    gr = gate.shape[1]
    gate_spec = (pl.BlockSpec((1, 1, d), lambda bi, i: (bi, 0, 0)) if gr == 1
                 else pl.BlockSpec((1, tm, d), lambda bi, i: (bi, i, 0)))
    row = lambda w: pl.BlockSpec((1, tm, w), lambda bi, i: (bi, i, 0))
    return pl.pallas_call(
        _combine_kernel,
        grid=(b, s // tm),
        in_specs=[row(d), gate_spec, row(D_ATT), row(D_POOL), _full(wo.shape), _full(fnw.shape)],
        out_specs=row(d),
        out_shape=jax.ShapeDtypeStruct((b, s, d), F32),
        compiler_params=pltpu.CompilerParams(dimension_semantics=("arbitrary", "arbitrary"),
                                             vmem_limit_bytes=VMEM_LIMIT),
        name="combine",
    )(x, gate, ag, mp, wo, fnw)


def _rope_tables(offsets, bases):
    inv = ROPE_BASE ** (-jnp.arange(0, ROPE_DIM, 2, dtype=F32) / ROPE_DIM)
    inv = jnp.pad(jnp.concatenate([inv, inv]), (0, HEAD_PAD - ROPE_DIM))
    lanes = (jnp.arange(HEAD_PAD) < ROPE_DIM).astype(F32)
    a_off = offsets.astype(F32)[:, None] * inv[None, :]
    a_base = bases.astype(F32)[:, None, None] * inv[None, None, :]
    return jnp.cos(a_off) * lanes, jnp.sin(a_off) * lanes, jnp.cos(a_base), jnp.sin(a_base)


def _rot_half(w):
    half = w.shape[-1] // 2
    return jnp.concatenate([-w[..., half:], w[..., :half]], axis=-1)


def _pack_weights(norm_w, w_in, q_norm_w, w_uq, kv_norm_w, w_uk, w_uv, w_pool, b_pool, pool_scale):
    d = w_in.shape[0]
    off_kr = Q_LORA + KV_LORA
    wt = jnp.swapaxes(w_in, 0, 1).astype(BF16)
    krc = wt[off_kr:off_kr + ROPE_DIM]
    win = jnp.concatenate([wt[:off_kr], krc, -krc[ROPE_DIM // 2:], krc[:ROPE_DIM // 2],
                           jnp.zeros((HEAD_PAD - 2 * ROPE_DIM, d), BF16), wt[off_kr + ROPE_DIM:]], axis=0)
    pad_q = HEAD_PAD - QK_DIM
    wq = jnp.pad(w_uq, ((0, 0), (0, 0), (0, pad_q))).reshape(Q_LORA, N_HEADS * HEAD_PAD)
    wqr = jnp.pad(_rot_half(w_uq[..., NOPE_DIM:]), ((0, 0), (0, 0), (NOPE_DIM, pad_q)))
    wqr = wqr.reshape(Q_LORA, N_HEADS * HEAD_PAD)
    wk_nope = jnp.pad(w_uk, ((0, 0), (0, 0), (0, HEAD_PAD - NOPE_DIM))).reshape(KV_LORA, N_HEADS * HEAD_PAD)
    place = jnp.pad(jnp.eye(ROPE_DIM, dtype=F32), ((0, HEAD_PAD - ROPE_DIM), (NOPE_DIM, pad_q)))
    wk = jnp.concatenate([wk_nope, jnp.tile(place, (1, N_HEADS))], axis=0)
    wv = w_uv.reshape(KV_LORA, N_HEADS * V_DIM).astype(BF16)
    shared = (norm_w.reshape(1, -1), win, q_norm_w.reshape(1, -1), kv_norm_w.reshape(1, -1),
              wq.astype(BF16), wqr.astype(BF16), w_pool.astype(BF16), b_pool.reshape(1, -1),
              pool_scale.reshape(1, -1))
    return shared, wk.astype(BF16), wv


def kernel(x_prompt, x_sample, c_prompt, c_sample, cache_ckv, cache_krope, state_pool, page_table, ada_w, ada_b, norm_w, w_in, q_norm_w, w_uq, kv_norm_w, w_uk, w_uv, w_pool, b_pool, pool_scale, w_out, final_norm_w):
    assert ada_w.shape[0] == 1, "single-layer trunk only"
    b, s, d = x_prompt.shape
    n = x_sample.shape[0]
    assert x_sample.shape[1] == 1
    past_len = page_table.shape[1] * PAGE_SIZE

    c_all = jnp.concatenate([c_prompt, c_sample], axis=0)
    rows = -(-c_all.shape[0] // 8) * 8
    c_all = jnp.pad(c_all, ((0, rows - c_all.shape[0]), (0, 0)))
    mod = _ada(c_all, ada_w[0], ada_b[0].reshape(1, -1))
    mod_p = mod[:b].reshape(b, 3, d)
    mod_s = mod[b:b + n]

    wts, wk, wv = _pack_weights(norm_w[0], w_in[0], q_norm_w[0], w_uq[0], kv_norm_w[0], w_uk[0], w_uv[0],
                                w_pool[0], b_pool[0], pool_scale[0])
    wo = w_out[0].astype(BF16)
    fnw = final_norm_w.reshape(1, -1)

    tabs_p = _rope_tables(jnp.arange(ROW_TILE), jnp.arange(0, s, ROW_TILE))
    q, k, vt, ckv_p, krt_p, sg_p, mp_p, utail = _inproj_p(x_prompt, mod_p, wts, wk, wv.T, tabs_p, tm=ROW_TILE)

    tabs_s = _rope_tables(jnp.zeros((n,)), jnp.full((1,), past_len))
    wukt = jnp.pad(jnp.transpose(w_uk[0], (1, 2, 0)), ((0, 0), (0, HEAD_PAD - NOPE_DIM), (0, 0))).astype(BF16)
    state_t = jnp.transpose(state_pool[0], (1, 0, 2))
    qabs, qrope, ckv_s, kr_s, sg_s, mp_s, u_s = _inproj_s(x_sample[:, 0, :], mod_s, wts, tabs_s, state_t, wukt)
    qrope = qrope.reshape(n, N_HEADS, HEAD_PAD)[:, :, NOPE_DIM:QK_DIM]
    ag_p, ag_s = _attention(q, k, vt, sg_p, page_table, qabs.reshape(n, N_HEADS, KV_LORA), qrope,
                            ckv_s[:, None, :], kr_s[:, None, :], wv, sg_s[:, None, :],
                            cache_ckv, jnp.swapaxes(cache_krope, 2, 3))
    y_prompt = _combine(x_prompt, mod_p[:, 2:3, :], ag_p, mp_p, wo, fnw, tm=2 * ROW_TILE)
    y_sample = _combine(x_sample.reshape(1, n, d), mod_s[None, :, 2 * d:], ag_s.reshape(1, n, D_ATT),
                        mp_s[None], wo, fnw, tm=n).reshape(n, 1, d)

    new_pool_p = utail[:, HALO - POOL_PREV:, :]
    new_pool_s = jnp.transpose(jnp.concatenate([state_t[1:], u_s[None]], axis=0), (1, 0, 2))
    return (y_prompt, y_sample,
            ckv_p[None], jnp.swapaxes(krt_p, 1, 2)[None], new_pool_p[None],
            ckv_s[:, None, :][None], kr_s[:, None, :][None], new_pool_s[None])
```

```python
import functools
import math

import jax
import jax.numpy as jnp
from jax import lax
from jax.experimental import pallas as pl
from jax.experimental.pallas import tpu as pltpu

F32 = jnp.float32
BF16 = jnp.bfloat16

N_HEADS = 8
NOPE_DIM = 64
ROPE_DIM = 32
QK_DIM = NOPE_DIM + ROPE_DIM
V_DIM = 64
Q_LORA = 256
KV_LORA = 128
D_POOL = 512
D_ATT = 512
POOL_WINDOWS = (2, 4, 8, 16)
POOL_GROUP = 128
POOL_PREV = 15
PAGE_SIZE = 128
ROPE_BASE = 10000.0
EPS = 1e-6
SM_SCALE = QK_DIM ** -0.5
LOG2E = math.log2(math.e)
QK_PRESCALE = SM_SCALE * LOG2E
_NT = (((1,), (1,)), ((), ()))

HEAD_PAD = 128
COL_SPLIT = 1
DEC_CHUNKS = 4
FIRST_CHUNKS = 2
SUM_ROWS = 16
HALO = 16
C_Q, C_KV, C_KR, C_GA, C_U, C_GP, C_END = 0, 256, 384, 512, 1024, 1536, 2048

ROW_TILE = 512
ROW_PARTS = 2
VMEM_LIMIT = 56 * 1024 * 1024


def _rms(x, w):
    return (x * lax.rsqrt(jnp.mean(x * x, axis=-1, keepdims=True) + EPS)) * w


def _silu(x):
    return x * jax.nn.sigmoid(x)


def _bdot(a, b):
    return jnp.dot(a.astype(BF16), b, preferred_element_type=F32)


def _ada_kernel(c_ref, w_ref, b_ref, o_ref):
    o_ref[...] = _bdot(_silu(c_ref[...]), w_ref[...].astype(BF16)) + b_ref[...]


def _ada(c_all, ada_w, ada_b):
    rows, d = c_all.shape
    n = ada_w.shape[1]
    bn = 1024
    return pl.pallas_call(
        _ada_kernel,
        grid=(n // bn,),
        in_specs=[pl.BlockSpec((rows, d), lambda j: (0, 0)),
                  pl.BlockSpec((d, bn), lambda j: (0, j)),
                  pl.BlockSpec((1, bn), lambda j: (0, j))],
        out_specs=pl.BlockSpec((rows, bn), lambda j: (0, j)),
        out_shape=jax.ShapeDtypeStruct((rows, n), F32),
        compiler_params=pltpu.CompilerParams(dimension_semantics=("arbitrary",)),
        name="ada",
    )(c_all, ada_w, ada_b)


def _project_z(x, shift, scale, nw, win):
    h = _rms(x, nw) * (1.0 + scale) + shift
    return lax.dot_general(h.astype(BF16), win, _NT, preferred_element_type=F32)


def _project_rest(z, qnw, kvnw, wq, wqr, cr, sr, cb, sb):
    cosk = cb * cr - sb * sr
    sink = sb * cr + cb * sr
    nope = (lax.broadcasted_iota(jnp.int32, (1, HEAD_PAD), 1) < NOPE_DIM).astype(F32)
    cosq = pltpu.roll(cosk, NOPE_DIM, axis=1) + nope
    sinq = pltpu.roll(sink, NOPE_DIM, axis=1)
    q_lat = _rms(z[:, C_Q:C_KV], qnw)
    ckv = _rms(z[:, C_KV:C_KR], kvnw)
    krc = z[:, C_KR:C_GA]
    kr128 = krc * cosk + pltpu.roll(krc, HEAD_PAD - ROPE_DIM, axis=1) * sink
    ql = q_lat.astype(BF16)
    qa = jnp.dot(ql, wq, preferred_element_type=F32)
    qb = jnp.dot(ql, wqr, preferred_element_type=F32)
    return qa, qb, ckv, kr128, z[:, C_GA:C_U], z[:, C_U:C_GP], z[:, C_GP:C_END], cosq, sinq


def _pool_out(pooled_g, g, wpool_ref, bpool_ref, pscale_ref, g_pool):
    lo, hi = g * POOL_GROUP, (g + 1) * POOL_GROUP
    y = _bdot(pooled_g, wpool_ref[g]) + bpool_ref[:, lo:hi]
    y = y * pscale_ref[:, lo:hi]
    return y * _silu(g_pool[:, lo:hi])


def _inproj_p_kernel(x_ref, mod_ref, nw_ref, win_ref, qnw_ref, kvnw_ref, wq_ref, wqr_ref,
                     wpool_ref, bpool_ref, pscale_ref, wk_ref, wvt_ref, cr_ref, sr_ref, cb_ref, sb_ref,
                     q_ref, k_ref, vt_ref, ckv_ref, krt_ref, sg_ref, mp_ref, utail_ref, uext_ref, *, tm):
    i = pl.program_id(1)
    shift = mod_ref[0, 0:1, :]
    scale = mod_ref[0, 1:2, :]

    @pl.when(i == 0)
    def _():
        uext_ref[0:HALO, :] = jnp.zeros((HALO, D_POOL), F32)

    hr = tm // ROW_PARTS
    zs = []
    for part in range(ROW_PARTS):
        rows = slice(part * hr, (part + 1) * hr)
        z = _project_z(x_ref[0, rows, :], shift, scale, nw_ref[...], win_ref[...])
        uext_ref[HALO + part * hr:HALO + (part + 1) * hr, :] = z[:, C_U:C_GP]
        zs.append(z)

    for part in range(ROW_PARTS):
        r0 = part * hr
        rows = slice(r0, r0 + hr)
        qa, qb, ckv, kr128, g_att, u, g_pool, cosq, sinq = _project_rest(
            zs[part], qnw_ref[...], kvnw_ref[...], wq_ref[...], wqr_ref[...],
            cr_ref[rows, :], sr_ref[rows, :], cb_ref[0], sb_ref[0])
        cosq, sinq = cosq * QK_PRESCALE, sinq * QK_PRESCALE
        for h in range(N_HEADS):
            sl = slice(h * HEAD_PAD, (h + 1) * HEAD_PAD)
            q_ref[0, rows, sl] = (qa[:, sl] * cosq + qb[:, sl] * sinq).astype(BF16)
        ckv_b = ckv.astype(BF16)
        k = jnp.dot(jnp.concatenate([ckv_b, kr128.astype(BF16)], axis=1), wk_ref[...],
                    preferred_element_type=F32)
        k_ref[0, rows, :] = k.astype(BF16)
        vt_ref[0, :, rows] = lax.dot_general(wvt_ref[...], ckv_b, _NT, preferred_element_type=F32).astype(BF16)
        ckv_ref[0, rows, :] = ckv
        krt_ref[0, :, rows] = kr128.T[:ROPE_DIM, :]
        sg_ref[0, rows, :] = _silu(g_att)
        pos = (i * tm + r0 + lax.broadcasted_iota(jnp.int32, (hr, 1), 0)).astype(F32)
        for g, w in enumerate(POOL_WINDOWS):
            lo, hi = g * POOL_GROUP, (g + 1) * POOL_GROUP
            wsum = u[:, lo:hi]
            for d in range(1, w):
                wsum = wsum + uext_ref[HALO + r0 - d:HALO + r0 - d + hr, lo:hi]
            cnt = jnp.minimum(pos + 1.0, float(w))
            pooled = wsum / cnt - u[:, lo:hi]
            mp_ref[0, rows, lo:hi] = _pool_out(pooled, g, wpool_ref, bpool_ref, pscale_ref, g_pool).astype(BF16)

    u_last = zs[-1][hr - HALO:hr, C_U:C_GP]
    uext_ref[0:HALO, :] = u_last

    @pl.when(i == pl.num_programs(1) - 1)
    def _():
        utail_ref[0] = u_last


def _full(shape):
    nd = len(shape)
    return pl.BlockSpec(shape, lambda *_: (0,) * nd)


def _inproj_p(x, mod3, wts, wk, wvt, tabs, tm=512):
    b, s, d = x.shape
    nt = s // tm
    row = lambda w: pl.BlockSpec((1, tm, w), lambda bi, i: (bi, i, 0))
    rtab = _full((tm, HEAD_PAD))
    btab = pl.BlockSpec((1, 1, HEAD_PAD), lambda bi, i: (i, 0, 0))
    in_specs = [row(d), pl.BlockSpec((1, 3, d), lambda bi, i: (bi, 0, 0))]
    in_specs += [_full(w.shape) for w in (*wts, wk, wvt)]
    in_specs += [rtab, rtab, btab, btab]
    out_shape = [
        jax.ShapeDtypeStruct((b, s, N_HEADS * HEAD_PAD), BF16),
        jax.ShapeDtypeStruct((b, s, N_HEADS * HEAD_PAD), BF16),
        jax.ShapeDtypeStruct((b, D_ATT, s), BF16),
        jax.ShapeDtypeStruct((b, s, KV_LORA), F32),
        jax.ShapeDtypeStruct((b, ROPE_DIM, s), F32),
        jax.ShapeDtypeStruct((b, s, D_ATT), F32),
        jax.ShapeDtypeStruct((b, s, D_POOL), BF16),
        jax.ShapeDtypeStruct((b, HALO, D_POOL), F32),
    ]
    out_specs = [row(N_HEADS * HEAD_PAD), row(N_HEADS * HEAD_PAD),
                 pl.BlockSpec((1, D_ATT, tm), lambda bi, i: (bi, 0, i)), row(KV_LORA),
                 pl.BlockSpec((1, ROPE_DIM, tm), lambda bi, i: (bi, 0, i)),
                 row(D_ATT), row(D_POOL), pl.BlockSpec((1, HALO, D_POOL), lambda bi, i: (bi, 0, 0))]
    return pl.pallas_call(
        functools.partial(_inproj_p_kernel, tm=tm),
        grid=(b, nt),
        in_specs=in_specs,
        out_specs=out_specs,
        out_shape=out_shape,
        scratch_shapes=[pltpu.VMEM((tm + HALO, D_POOL), F32)],
        compiler_params=pltpu.CompilerParams(dimension_semantics=("arbitrary", "arbitrary"),
                                             vmem_limit_bytes=VMEM_LIMIT),
        name="inproj_p",
    )(x, mod3, *wts, wk, wvt, *tabs)


def _inproj_s_kernel(x_ref, mod_ref, nw_ref, win_ref, qnw_ref, kvnw_ref, wq_ref, wqr_ref,
                     wpool_ref, bpool_ref, pscale_ref, cr_ref, sr_ref, cb_ref, sb_ref,
                     state_ref, wukt_ref,
                     qabs_ref, qrope_ref, ckv_ref, kr_ref, sg_ref, mp_ref, u_ref):
    d = x_ref.shape[1]
    shift = mod_ref[:, 0:d]
    scale = mod_ref[:, d:2 * d]
    z = _project_z(x_ref[...], shift, scale, nw_ref[...], win_ref[...])
    qa, qb, ckv, kr128, g_att, u, g_pool, cosq, sinq = _project_rest(
        z, qnw_ref[...], kvnw_ref[...], wq_ref[...], wqr_ref[...], cr_ref[...], sr_ref[...], cb_ref[0], sb_ref[0])
    for h in range(N_HEADS):
        sl = slice(h * HEAD_PAD, (h + 1) * HEAD_PAD)
        qrope_ref[:, sl] = qa[:, sl] * cosq + qb[:, sl] * sinq
        qabs_ref[:, sl] = _bdot(qa[:, sl], wukt_ref[h])
    ckv_ref[...] = ckv
    kr_ref[...] = kr128[:, :ROPE_DIM]
    sg_ref[...] = _silu(g_att)
    u_ref[...] = u
    for g, w in enumerate(POOL_WINDOWS):
        lo, hi = g * POOL_GROUP, (g + 1) * POOL_GROUP
        wsum = u[:, lo:hi]
        for j in range(1, w):
            wsum = wsum + state_ref[POOL_PREV - j, :, lo:hi]
        pooled = wsum / float(w) - u[:, lo:hi]
        mp_ref[:, lo:hi] = _pool_out(pooled, g, wpool_ref, bpool_ref, pscale_ref, g_pool).astype(BF16)


def _inproj_s(x, mod, wts, tabs, state, wukt):
    n, d = x.shape
    args = (x, mod, *wts, *tabs, state, wukt)
    out_shape = [
        jax.ShapeDtypeStruct((n, N_HEADS * HEAD_PAD), F32),
        jax.ShapeDtypeStruct((n, N_HEADS * HEAD_PAD), F32),
        jax.ShapeDtypeStruct((n, KV_LORA), F32),
        jax.ShapeDtypeStruct((n, ROPE_DIM), F32),
        jax.ShapeDtypeStruct((n, D_ATT), F32),
        jax.ShapeDtypeStruct((n, D_POOL), BF16),
        jax.ShapeDtypeStruct((n, D_POOL), F32),
    ]
    return pl.pallas_call(
        _inproj_s_kernel,
        in_specs=[_full(a.shape) for a in args],
        out_specs=[_full(o.shape) for o in out_shape],
        out_shape=out_shape,
        grid=(1,),
        compiler_params=pltpu.CompilerParams(dimension_semantics=("arbitrary",), vmem_limit_bytes=VMEM_LIMIT),
        name="inproj_s",
    )(*args)


def _flash_tile(qi, q_ref, k_ref, vt_ref, sg_ref, o_ref, st_ref, between, side, *, tq, tk):
    assert tq == tk
    cw = tq // COL_SPLIT
    units = [(hh, cs) for hh in range(2) for cs in range(COL_SPLIT)]
    qs = [q_ref[0, cs * cw:(cs + 1) * cw, hh * HEAD_PAD:(hh + 1) * HEAD_PAD] for hh, cs in units]

    def produce(j, slot, u, rows=None):
        hh, cs = units[u]
        r0, r1 = (0, tk) if rows is None else rows
        kb = k_ref[0, pl.ds(pl.multiple_of(j * tk, tk) + r0, r1 - r0), hh * HEAD_PAD:(hh + 1) * HEAD_PAD]
        st_ref[slot][hh][r0:r1, cs * cw:(cs + 1) * cw] = lax.dot_general(kb, qs[u], _NT,
                                                                         preferred_element_type=F32)

    def consume(j, slot, u, state, diagonal=False):
        hh, cs = units[u]
        m, acc = state
        st = st_ref[slot][hh][:, cs * cw:(cs + 1) * cw]
        if diagonal:
            kpos = lax.broadcasted_iota(jnp.int32, (tk, cw), 0)
            qpos = cs * cw + lax.broadcasted_iota(jnp.int32, (tk, cw), 1)
            st = jnp.where(kpos <= qpos, st, -jnp.inf)
        vtb = vt_ref[0, hh * V_DIM:(hh + 1) * V_DIM, pl.ds(pl.multiple_of(j * tk, tk), tk)]
        vtb = jnp.concatenate([vtb, jnp.ones((SUM_ROWS, tk), BF16)], axis=0)
        m_new = jnp.maximum(m, jnp.max(st, axis=0, keepdims=True))
        alpha = jnp.exp2(m - m_new)
        pt = jnp.exp2(st - m_new).astype(BF16)
        return m_new, alpha * acc + jnp.dot(vtb, pt, preferred_element_type=F32)

    def finish(states):
        heads = []
        for hh in range(2):
            acc = jnp.concatenate([states[u][1] for u in range(len(units)) if units[u][0] == hh], axis=1)
            heads.append(acc[:V_DIM] / acc[V_DIM:V_DIM + 1])
        o_ref[0] = (jnp.concatenate(heads, axis=0).T * sg_ref[0]).astype(BF16)

    def pair(t, states):
        j = 2 * t
        states = list(states)
        for u in range(len(units)):
            produce(j + 1, 1, u)
            states[u] = consume(j, 0, u, states[u])
        for u in range(len(units)):
            produce(j + 2, 0, u)
            states[u] = consume(j + 1, 1, u, states[u])
        return tuple(states)

    init = tuple((jnp.full((1, cw), -jnp.inf, F32), jnp.zeros((V_DIM + SUM_ROWS, cw), F32)) for _ in units)
    pieces = FIRST_CHUNKS // len(units)
    for u in range(len(units)):
        for c in range(pieces):
            produce(0, 0, u, (c * tk // pieces, (c + 1) * tk // pieces))
            between(u * pieces + c)
    states = lax.fori_loop(0, qi // 2, pair, init)

    def run_interleaved(steps, pieces, per):
        for n, step in enumerate(steps):
            for piece in pieces[n * per:(n + 1) * per]:
                piece()
            step()
        for piece in pieces[len(steps) * per:]:
            piece()

    @pl.when(qi % 2 == 0)
    def _():
        wait, pieces = side(0)
        wait()
        done = [None] * len(units)

        def last(u):
            def run():
                done[u] = consume(qi, 0, u, states[u], diagonal=True)
            return run
        run_interleaved([last(u) for u in range(len(units))], pieces, 2)
        finish(done)

    @pl.when(qi % 2 == 1)
    def _():
        wait, pieces = side(1)
        wait()
        mid, done = [None] * len(units), [None] * len(units)

        def before_last(u):
            def run():
                produce(qi, 1, u)
                mid[u] = consume(qi - 1, 0, u, states[u])
            return run

        def last(u):
            def run():
                done[u] = consume(qi, 1, u, mid[u], diagonal=True)
            return run
        run_interleaved([before_last(u) for u in range(len(units))] + [last(u) for u in range(len(units))], pieces, 1)
        finish(done)


def _page_copies(pt_ref, cckv_hbm, ckrt_hbm, ckv_buf, krt_buf, sem, n_pages, row, sl, p):
    page = pt_ref[row * n_pages + p]
    cols = pl.ds(p * PAGE_SIZE if isinstance(p, int) else pl.multiple_of(p * PAGE_SIZE, PAGE_SIZE), PAGE_SIZE)
    return (pltpu.make_async_copy(cckv_hbm.at[0, page], ckv_buf.at[sl, cols], sem.at[0, sl]),
            pltpu.make_async_copy(ckrt_hbm.at[0, page], krt_buf.at[sl, :, cols], sem.at[1, sl]))


def _decode_stages(pt_ref, qa_ref, qr_ref, ckvn_ref, krn_ref, wv_ref, sg_ref, cckv_hbm, ckrt_hbm,
                   o_ref, ckv_buf, krt_buf, sem, *, row, slot, n_pages):
    def wait():
        def wait_body(p, _):
            for cp in _page_copies(pt_ref, cckv_hbm, ckrt_hbm, ckv_buf, krt_buf, sem, n_pages, row, slot, p):
                cp.wait()
            return 0
        lax.fori_loop(0, n_pages, wait_body, 0, unroll=8)

    past = ckv_buf.shape[1]
    ck = past // DEC_CHUNKS
    env = {"ckv": [], "s": []}

    def scores(c):
        def run():
            if c == 0:
                env["qa"] = qa_ref[0].astype(BF16)
                env["qr"] = qr_ref[0].astype(BF16)
            ckv = ckv_buf[slot, c * ck:(c + 1) * ck, :].astype(BF16)
            krt = krt_buf[slot, :, c * ck:(c + 1) * ck].astype(BF16)
            env["ckv"].append(ckv)
            env["s"].append(lax.dot_general(env["qa"], ckv, _NT, preferred_element_type=F32)
                            + jnp.dot(env["qr"], krt, preferred_element_type=F32))
        return run

    def values(c):
        def run():
            if c == 0:
                qa, qr = env["qa"], env["qr"]
                s = jnp.concatenate(env["s"], axis=1)
                ckvn = ckvn_ref[0].astype(BF16).astype(F32)
                krn = krn_ref[0].astype(BF16).astype(F32)
                s_n = (jnp.sum(qa.astype(F32) * ckvn, axis=1, keepdims=True)
                       + jnp.sum(qr.astype(F32) * krn, axis=1, keepdims=True))
                m = jnp.maximum(jnp.max(s, axis=1, keepdims=True), s_n)
                p = jnp.exp2((s - m) * QK_PRESCALE)
                p_n = jnp.exp2((s_n - m) * QK_PRESCALE)
                env["l"] = jnp.sum(p, axis=1, keepdims=True) + p_n
                env["p"] = p.astype(BF16)
                env["o"] = p_n.astype(BF16).astype(F32) * ckvn
            env["o"] = env["o"] + jnp.dot(env["p"][:, c * ck:(c + 1) * ck], env["ckv"][c],
                                          preferred_element_type=F32)
            if c == DEC_CHUNKS - 1:
                o_all = _bdot(env["o"] / env["l"], wv_ref[...])
                hrow = lax.broadcasted_iota(jnp.int32, o_all.shape, 0)
                hcol = lax.broadcasted_iota(jnp.int32, o_all.shape, 1) // V_DIM
                att = jnp.sum(jnp.where(hrow == hcol, o_all, 0.0), axis=0, keepdims=True)
                o_ref[0] = (att * sg_ref[0]).astype(BF16)
        return run

    return wait, [scores(c) for c in range(DEC_CHUNKS)] + [values(c) for c in range(DEC_CHUNKS)]


def _attn_kernel(pt_ref, q_ref, k_ref, vt_ref, sgp_ref, qa_ref, qr_ref, ckvn_ref, krn_ref, wv_ref, sgs_ref,
                 cckv_hbm, ckrt_hbm, op_ref, os_ref, st00, st01, st10, st11, ckv_buf, krt_buf, sem,
                 *, tq, tk, n_pages):
    qi = pl.program_id(2)
    r = (pl.program_id(0) * pl.num_programs(1) + pl.program_id(1)) * pl.num_programs(2) + qi
    n_rows = pl.num_programs(0) * pl.num_programs(1) * pl.num_programs(2)
    slot = r % 2
    dma = (pt_ref, cckv_hbm, ckrt_hbm, ckv_buf, krt_buf, sem, n_pages)

    @pl.when(r == 0)
    def _():
        def body(p, _):
            for cp in _page_copies(*dma, 0, 0, p):
                cp.start()
            return 0
        lax.fori_loop(0, n_pages, body, 0, unroll=8)

    nxt = jnp.minimum(r + 1, n_rows - 1)
    assert n_pages % FIRST_CHUNKS == 0
    per_chunk = n_pages // FIRST_CHUNKS

    def issue(c):
        for p in range(c * per_chunk, (c + 1) * per_chunk):
            for cp in _page_copies(*dma, nxt, 1 - slot, p):
                cp.start()

    def row_stages(parity):
        return _decode_stages(pt_ref, qa_ref, qr_ref, ckvn_ref, krn_ref, wv_ref, sgs_ref, cckv_hbm, ckrt_hbm,
                              os_ref, ckv_buf, krt_buf, sem, row=r, slot=parity, n_pages=n_pages)

    _flash_tile(qi, q_ref, k_ref, vt_ref, sgp_ref, op_ref, ((st00, st01), (st10, st11)), issue, row_stages,
                tq=tq, tk=tk)

    @pl.when(r == n_rows - 1)
    def _():
        def body(p, _):
            for cp in _page_copies(*dma, nxt, 1 - slot, p):
                cp.wait()
            return 0
        lax.fori_loop(0, n_pages, body, 0, unroll=8)


def _attention(q, k, vt, sg_p, page_table, qabs, qrope, ckv_new, kr_new, wv, sg_s, cache_ckv, cache_krt,
               tq=512, tk=512):
    b, s, _ = q.shape
    n_pairs = N_HEADS // 2
    nq = s // tq
    n, n_pages = page_table.shape
    assert n == b * n_pairs * nq, "one sample row per prompt attention step"
    assert nq % 2 == 0, "row parity must equal q-tile parity"
    past = n_pages * PAGE_SIZE
    row = lambda bi, hp, qi: (bi * n_pairs + hp) * nq + qi
    per_r = lambda w: pl.BlockSpec((1, 1, w), lambda bi, hp, qi, pt: (row(bi, hp, qi), 0, 0))
    per_h = lambda w: pl.BlockSpec((1, N_HEADS, w), lambda bi, hp, qi, pt: (row(bi, hp, qi), 0, 0))
    grid_spec = pltpu.PrefetchScalarGridSpec(
        num_scalar_prefetch=1,
        grid=(b, n_pairs, nq),
        in_specs=[pl.BlockSpec((1, tq, 2 * HEAD_PAD), lambda bi, hp, qi, pt: (bi, qi, hp)),
                  pl.BlockSpec((1, s, 2 * HEAD_PAD), lambda bi, hp, qi, pt: (bi, 0, hp)),
                  pl.BlockSpec((1, 2 * V_DIM, s), lambda bi, hp, qi, pt: (bi, hp, 0)),
                  pl.BlockSpec((1, tq, 2 * V_DIM), lambda bi, hp, qi, pt: (bi, qi, hp)),
                  per_h(KV_LORA), per_h(ROPE_DIM), per_r(KV_LORA), per_r(ROPE_DIM),
                  pl.BlockSpec(wv.shape, lambda bi, hp, qi, pt: (0, 0)),
                  per_r(D_ATT),
                  pl.BlockSpec(memory_space=pl.ANY),
                  pl.BlockSpec(memory_space=pl.ANY)],
        out_specs=[pl.BlockSpec((1, tq, 2 * V_DIM), lambda bi, hp, qi, pt: (bi, qi, hp)), per_r(D_ATT)],
        scratch_shapes=[pltpu.VMEM((tk, tq), F32)] * 4 + [
            pltpu.VMEM((2, past, KV_LORA), F32),
            pltpu.VMEM((2, ROPE_DIM, past), F32),
            pltpu.SemaphoreType.DMA((2, 2))],
    )
    return pl.pallas_call(
        functools.partial(_attn_kernel, tq=tq, tk=tk, n_pages=n_pages),
        grid_spec=grid_spec,
        out_shape=[jax.ShapeDtypeStruct((b, s, D_ATT), BF16), jax.ShapeDtypeStruct((n, 1, D_ATT), BF16)],
        compiler_params=pltpu.CompilerParams(dimension_semantics=("arbitrary", "arbitrary", "arbitrary"),
                                             vmem_limit_bytes=VMEM_LIMIT),
        name="attention",
    )(page_table.reshape(-1), q, k, vt, sg_p, qabs, qrope, ckv_new, kr_new, wv, sg_s, cache_ckv, cache_krt)


def _combine_kernel(x_ref, gate_ref, ag_ref, mp_ref, wo_ref, fnw_ref, y_ref):
    proj = (jnp.dot(ag_ref[0], wo_ref[0:D_ATT, :], preferred_element_type=F32)
            + jnp.dot(mp_ref[0], wo_ref[D_ATT:D_ATT + D_POOL, :], preferred_element_type=F32))
    y = x_ref[0] + gate_ref[0] * proj
    y_ref[0] = _rms(y, fnw_ref[...])


def _combine(x, gate, ag, mp, wo, fnw, tm):
    b, s, d = x.shape
    gr = gate.shape[1]
    gate_spec = (pl.BlockSpec((1, 1, d), lambda bi, i: (bi, 0, 0)) if gr == 1
                 else pl.BlockSpec((1, tm, d), lambda bi, i: (bi, i, 0)))
    row = lambda w: pl.BlockSpec((1, tm, w), lambda bi, i: (bi, i, 0))
    return pl.pallas_call(
        _combine_kernel,
        grid=(b, s // tm),
        in_specs=[row(d), gate_spec, row(D_ATT), row(D_POOL), _full(wo.shape), _full(fnw.shape)],
        out_specs=row(d),
        out_shape=jax.ShapeDtypeStruct((b, s, d), F32),
        compiler_params=pltpu.CompilerParams(dimension_semantics=("arbitrary", "arbitrary"),
                                             vmem_limit_bytes=VMEM_LIMIT),
        name="combine",
    )(x, gate, ag, mp, wo, fnw)


def _rope_tables(offsets, bases):
    inv = ROPE_BASE ** (-jnp.arange(0, ROPE_DIM, 2, dtype=F32) / ROPE_DIM)
    inv = jnp.pad(jnp.concatenate([inv, inv]), (0, HEAD_PAD - ROPE_DIM))
    lanes = (jnp.arange(HEAD_PAD) < ROPE_DIM).astype(F32)
    a_off = offsets.astype(F32)[:, None] * inv[None, :]
    a_base = bases.astype(F32)[:, None, None] * inv[None, None, :]
    return jnp.cos(a_off) * lanes, jnp.sin(a_off) * lanes, jnp.cos(a_base), jnp.sin(a_base)


def _rot_half(w):
    half = w.shape[-1] // 2
    return jnp.concatenate([-w[..., half:], w[..., :half]], axis=-1)


def _pack_weights(norm_w, w_in, q_norm_w, w_uq, kv_norm_w, w_uk, w_uv, w_pool, b_pool, pool_scale):
    d = w_in.shape[0]
    off_kr = Q_LORA + KV_LORA
    wt = jnp.swapaxes(w_in, 0, 1).astype(BF16)
    krc = wt[off_kr:off_kr + ROPE_DIM]
    win = jnp.concatenate([wt[:off_kr], krc, -krc[ROPE_DIM // 2:], krc[:ROPE_DIM // 2],
                           jnp.zeros((HEAD_PAD - 2 * ROPE_DIM, d), BF16), wt[off_kr + ROPE_DIM:]], axis=0)
    pad_q = HEAD_PAD - QK_DIM
    wq = jnp.pad(w_uq, ((0, 0), (0, 0), (0, pad_q))).reshape(Q_LORA, N_HEADS * HEAD_PAD)
    wqr = jnp.pad(_rot_half(w_uq[..., NOPE_DIM:]), ((0, 0), (0, 0), (NOPE_DIM, pad_q)))
    wqr = wqr.reshape(Q_LORA, N_HEADS * HEAD_PAD)
    wk_nope = jnp.pad(w_uk, ((0, 0), (0, 0), (0, HEAD_PAD - NOPE_DIM))).reshape(KV_LORA, N_HEADS * HEAD_PAD)
    place = jnp.pad(jnp.eye(ROPE_DIM, dtype=F32), ((0, HEAD_PAD - ROPE_DIM), (NOPE_DIM, pad_q)))
    wk = jnp.concatenate([wk_nope, jnp.tile(place, (1, N_HEADS))], axis=0)
    wv = w_uv.reshape(KV_LORA, N_HEADS * V_DIM).astype(BF16)
    shared = (norm_w.reshape(1, -1), win, q_norm_w.reshape(1, -1), kv_norm_w.reshape(1, -1),
              wq.astype(BF16), wqr.astype(BF16), w_pool.astype(BF16), b_pool.reshape(1, -1),
              pool_scale.reshape(1, -1))
    return shared, wk.astype(BF16), wv


def kernel(x_prompt, x_sample, c_prompt, c_sample, cache_ckv, cache_krope, state_pool, page_table, ada_w, ada_b, norm_w, w_in, q_norm_w, w_uq, kv_norm_w, w_uk, w_uv, w_pool, b_pool, pool_scale, w_out, final_norm_w):
    assert ada_w.shape[0] == 1, "single-layer trunk only"
    b, s, d = x_prompt.shape
    n = x_sample.shape[0]
    assert x_sample.shape[1] == 1
    past_len = page_table.shape[1] * PAGE_SIZE

    c_all = jnp.concatenate([c_prompt, c_sample], axis=0)
    rows = -(-c_all.shape[0] // 8) * 8
    c_all = jnp.pad(c_all, ((0, rows - c_all.shape[0]), (0, 0)))
    mod = _ada(c_all, ada_w[0], ada_b[0].reshape(1, -1))
    mod_p = mod[:b].reshape(b, 3, d)
    mod_s = mod[b:b + n]

    wts, wk, wv = _pack_weights(norm_w[0], w_in[0], q_norm_w[0], w_uq[0], kv_norm_w[0], w_uk[0], w_uv[0],
                                w_pool[0], b_pool[0], pool_scale[0])
    wo = w_out[0].astype(BF16)
    fnw = final_norm_w.reshape(1, -1)

    tabs_p = _rope_tables(jnp.arange(ROW_TILE), jnp.arange(0, s, ROW_TILE))
    q, k, vt, ckv_p, krt_p, sg_p, mp_p, utail = _inproj_p(x_prompt, mod_p, wts, wk, wv.T, tabs_p, tm=ROW_TILE)

    tabs_s = _rope_tables(jnp.zeros((n,)), jnp.full((1,), past_len))
    wukt = jnp.pad(jnp.transpose(w_uk[0], (1, 2, 0)), ((0, 0), (0, HEAD_PAD - NOPE_DIM), (0, 0))).astype(BF16)
    state_t = jnp.transpose(state_pool[0], (1, 0, 2))
    qabs, qrope, ckv_s, kr_s, sg_s, mp_s, u_s = _inproj_s(x_sample[:, 0, :], mod_s, wts, tabs_s, state_t, wukt)
    qrope = qrope.reshape(n, N_HEADS, HEAD_PAD)[:, :, NOPE_DIM:QK_DIM]
    ag_p, ag_s = _attention(q, k, vt, sg_p, page_table, qabs.reshape(n, N_HEADS, KV_LORA), qrope,
                            ckv_s[:, None, :], kr_s[:, None, :], wv, sg_s[:, None, :],
                            cache_ckv, jnp.swapaxes(cache_krope, 2, 3))
    y_prompt = _combine(x_prompt, mod_p[:, 2:3, :], ag_p, mp_p, wo, fnw, tm=2 * ROW_TILE)
    y_sample = _combine(x_sample.reshape(1, n, d), mod_s[None, :, 2 * d:], ag_s.reshape(1, n, D_ATT),
                        mp_s[None], wo, fnw, tm=n).reshape(n, 1, d)

    new_pool_p = utail[:, HALO - POOL_PREV:, :]
    new_pool_s = jnp.transpose(jnp.concatenate([state_t[1:], u_s[None]], axis=0), (1, 0, 2))
    return (y_prompt, y_sample,
            ckv_p[None], jnp.swapaxes(krt_p, 1, 2)[None], new_pool_p[None],
            ckv_s[:, None, :][None], kr_s[:, None, :][None], new_pool_s[None])
```

```python
import functools
import math

import jax
import jax.numpy as jnp
from jax import lax
from jax.experimental import pallas as pl
from jax.experimental.pallas import tpu as pltpu

F32 = jnp.float32
BF16 = jnp.bfloat16

N_HEADS = 8
NOPE_DIM = 64
ROPE_DIM = 32
QK_DIM = NOPE_DIM + ROPE_DIM
V_DIM = 64
Q_LORA = 256
KV_LORA = 128
D_POOL = 512
D_ATT = 512
POOL_WINDOWS = (2, 4, 8, 16)
POOL_GROUP = 128
POOL_PREV = 15
PAGE_SIZE = 128
ROPE_BASE = 10000.0
EPS = 1e-6
SM_SCALE = QK_DIM ** -0.5
LOG2E = math.log2(math.e)
QK_PRESCALE = SM_SCALE * LOG2E
_NT = (((1,), (1,)), ((), ()))

HEAD_PAD = 128
SUM_ROWS = 16
HALO = 16
C_Q, C_KV, C_KR, C_GA, C_U, C_GP, C_END = 0, 256, 384, 512, 1024, 1536, 2048

ROW_TILE = 512
ROW_PARTS = 2
COL_SPLIT = 1
FIRST_CHUNKS = 2
DEC_CHUNKS = 4
EVEN_DEAL = (2, 2)
ODD_DEAL = (2, 2, 1, 1)

VMEM_LIMIT = 56 * 1024 * 1024


def _rms(x, w):
    return (x * lax.rsqrt(jnp.mean(x * x, axis=-1, keepdims=True) + EPS)) * w


def _silu(x):
    return x * jax.nn.sigmoid(x)


def _bdot(a, b):
    return jnp.dot(a.astype(BF16), b, preferred_element_type=F32)


def _ada_kernel(c_ref, w_ref, b_ref, o_ref):
    o_ref[...] = _bdot(_silu(c_ref[...]), w_ref[...].astype(BF16)) + b_ref[...]


def _ada(c_all, ada_w, ada_b):
    rows, d = c_all.shape
    n = ada_w.shape[1]
    bn = 1024
    return pl.pallas_call(
        _ada_kernel,
        grid=(n // bn,),
        in_specs=[pl.BlockSpec((rows, d), lambda j: (0, 0)),
                  pl.BlockSpec((d, bn), lambda j: (0, j)),
                  pl.BlockSpec((1, bn), lambda j: (0, j))],
        out_specs=pl.BlockSpec((rows, bn), lambda j: (0, j)),
        out_shape=jax.ShapeDtypeStruct((rows, n), F32),
        compiler_params=pltpu.CompilerParams(dimension_semantics=("arbitrary",)),
        name="ada",
    )(c_all, ada_w, ada_b)


def _project_z(x, shift, scale, nw, win):
    h = _rms(x, nw) * (1.0 + scale) + shift
    return lax.dot_general(h.astype(BF16), win, _NT, preferred_element_type=F32)


def _project_rest(z, qnw, kvnw, wq, wqr, cr, sr, cb, sb):
    cosk = cb * cr - sb * sr
    sink = sb * cr + cb * sr
    nope = (lax.broadcasted_iota(jnp.int32, (1, HEAD_PAD), 1) < NOPE_DIM).astype(F32)
    cosq = pltpu.roll(cosk, NOPE_DIM, axis=1) + nope
    sinq = pltpu.roll(sink, NOPE_DIM, axis=1)
    q_lat = _rms(z[:, C_Q:C_KV], qnw)
    ckv = _rms(z[:, C_KV:C_KR], kvnw)
    krc = z[:, C_KR:C_GA]
    kr128 = krc * cosk + pltpu.roll(krc, HEAD_PAD - ROPE_DIM, axis=1) * sink
    ql = q_lat.astype(BF16)
    qa = jnp.dot(ql, wq, preferred_element_type=F32)
    qb = jnp.dot(ql, wqr, preferred_element_type=F32)
    return qa, qb, ckv, kr128, z[:, C_GA:C_U], z[:, C_U:C_GP], z[:, C_GP:C_END], cosq, sinq


def _pool_out(pooled_g, g, wpool_ref, bpool_ref, pscale_ref, g_pool):
    lo, hi = g * POOL_GROUP, (g + 1) * POOL_GROUP
    y = _bdot(pooled_g, wpool_ref[g]) + bpool_ref[:, lo:hi]
    y = y * pscale_ref[:, lo:hi]
    return y * _silu(g_pool[:, lo:hi])


def _inproj_p_kernel(x_ref, mod_ref, nw_ref, win_ref, qnw_ref, kvnw_ref, wq_ref, wqr_ref,
                     wpool_ref, bpool_ref, pscale_ref, wk_ref, wvt_ref, cr_ref, sr_ref, cb_ref, sb_ref,
                     q_ref, k_ref, vt_ref, ckv_ref, krt_ref, sg_ref, mp_ref, utail_ref, uext_ref, *, tm):
    i = pl.program_id(1)
    shift = mod_ref[0, 0:1, :]
    scale = mod_ref[0, 1:2, :]

    @pl.when(i == 0)
    def _():
        uext_ref[0:HALO, :] = jnp.zeros((HALO, D_POOL), F32)

    hr = tm // ROW_PARTS
    zs = []
    for part in range(ROW_PARTS):
        rows = slice(part * hr, (part + 1) * hr)
        z = _project_z(x_ref[0, rows, :], shift, scale, nw_ref[...], win_ref[...])
        uext_ref[HALO + part * hr:HALO + (part + 1) * hr, :] = z[:, C_U:C_GP]
        zs.append(z)

    for part in range(ROW_PARTS):
        r0 = part * hr
        rows = slice(r0, r0 + hr)
        qa, qb, ckv, kr128, g_att, u, g_pool, cosq, sinq = _project_rest(
            zs[part], qnw_ref[...], kvnw_ref[...], wq_ref[...], wqr_ref[...],
            cr_ref[rows, :], sr_ref[rows, :], cb_ref[0], sb_ref[0])
        cosq, sinq = cosq * QK_PRESCALE, sinq * QK_PRESCALE
        for h in range(N_HEADS):
            sl = slice(h * HEAD_PAD, (h + 1) * HEAD_PAD)
            q_ref[0, rows, sl] = (qa[:, sl] * cosq + qb[:, sl] * sinq).astype(BF16)
        ckv_b = ckv.astype(BF16)
        k = jnp.dot(jnp.concatenate([ckv_b, kr128.astype(BF16)], axis=1), wk_ref[...],
                    preferred_element_type=F32)
        k_ref[0, rows, :] = k.astype(BF16)
        vt_ref[0, :, rows] = lax.dot_general(wvt_ref[...], ckv_b, _NT, preferred_element_type=F32).astype(BF16)
        ckv_ref[0, rows, :] = ckv
        krt_ref[0, :, rows] = kr128.T[:ROPE_DIM, :]
        sg_ref[0, rows, :] = _silu(g_att)
        pos = (i * tm + r0 + lax.broadcasted_iota(jnp.int32, (hr, 1), 0)).astype(F32)
        for g, w in enumerate(POOL_WINDOWS):
            lo, hi = g * POOL_GROUP, (g + 1) * POOL_GROUP
            wsum = u[:, lo:hi]
            for d in range(1, w):
                wsum = wsum + uext_ref[HALO + r0 - d:HALO + r0 - d + hr, lo:hi]
            cnt = jnp.minimum(pos + 1.0, float(w))
            pooled = wsum / cnt - u[:, lo:hi]
            mp_ref[0, rows, lo:hi] = _pool_out(pooled, g, wpool_ref, bpool_ref, pscale_ref, g_pool).astype(BF16)

    u_last = zs[-1][hr - HALO:hr, C_U:C_GP]
    uext_ref[0:HALO, :] = u_last

    @pl.when(i == pl.num_programs(1) - 1)
    def _():
        utail_ref[0] = u_last


def _full(shape):
    nd = len(shape)
    return pl.BlockSpec(shape, lambda *_: (0,) * nd)


def _inproj_p(x, mod3, wts, wk, wvt, tabs, tm=512):
    b, s, d = x.shape
    nt = s // tm
    row = lambda w: pl.BlockSpec((1, tm, w), lambda bi, i: (bi, i, 0))
    rtab = _full((tm, HEAD_PAD))
    btab = pl.BlockSpec((1, 1, HEAD_PAD), lambda bi, i: (i, 0, 0))
    in_specs = [row(d), pl.BlockSpec((1, 3, d), lambda bi, i: (bi, 0, 0))]
    in_specs += [_full(w.shape) for w in (*wts, wk, wvt)]
    in_specs += [rtab, rtab, btab, btab]
    out_shape = [
        jax.ShapeDtypeStruct((b, s, N_HEADS * HEAD_PAD), BF16),
        jax.ShapeDtypeStruct((b, s, N_HEADS * HEAD_PAD), BF16),
        jax.ShapeDtypeStruct((b, D_ATT, s), BF16),
        jax.ShapeDtypeStruct((b, s, KV_LORA), F32),
        jax.ShapeDtypeStruct((b, ROPE_DIM, s), F32),
        jax.ShapeDtypeStruct((b, s, D_ATT), F32),
        jax.ShapeDtypeStruct((b, s, D_POOL), BF16),
        jax.ShapeDtypeStruct((b, HALO, D_POOL), F32),
    ]
    out_specs = [row(N_HEADS * HEAD_PAD), row(N_HEADS * HEAD_PAD),
                 pl.BlockSpec((1, D_ATT, tm), lambda bi, i: (bi, 0, i)), row(KV_LORA),
                 pl.BlockSpec((1, ROPE_DIM, tm), lambda bi, i: (bi, 0, i)),
                 row(D_ATT), row(D_POOL), pl.BlockSpec((1, HALO, D_POOL), lambda bi, i: (bi, 0, 0))]
    return pl.pallas_call(
        functools.partial(_inproj_p_kernel, tm=tm),
        grid=(b, nt),
        in_specs=in_specs,
        out_specs=out_specs,
        out_shape=out_shape,
        scratch_shapes=[pltpu.VMEM((tm + HALO, D_POOL), F32)],
        compiler_params=pltpu.CompilerParams(dimension_semantics=("arbitrary", "arbitrary"),
                                             vmem_limit_bytes=VMEM_LIMIT),
        name="inproj_p",
    )(x, mod3, *wts, wk, wvt, *tabs)


def _inproj_s_kernel(x_ref, mod_ref, nw_ref, win_ref, qnw_ref, kvnw_ref, wq_ref, wqr_ref,
                     wpool_ref, bpool_ref, pscale_ref, cr_ref, sr_ref, cb_ref, sb_ref,
                     state_ref, wukt_ref,
                     qabs_ref, qrope_ref, ckv_ref, kr_ref, sg_ref, mp_ref, u_ref):
    d = x_ref.shape[1]
    shift = mod_ref[:, 0:d]
    scale = mod_ref[:, d:2 * d]
    z = _project_z(x_ref[...], shift, scale, nw_ref[...], win_ref[...])
    qa, qb, ckv, kr128, g_att, u, g_pool, cosq, sinq = _project_rest(
        z, qnw_ref[...], kvnw_ref[...], wq_ref[...], wqr_ref[...], cr_ref[...], sr_ref[...], cb_ref[0], sb_ref[0])
    for h in range(N_HEADS):
        sl = slice(h * HEAD_PAD, (h + 1) * HEAD_PAD)
        qrope_ref[:, sl] = qa[:, sl] * cosq + qb[:, sl] * sinq
        qabs_ref[:, sl] = _bdot(qa[:, sl], wukt_ref[h])
    ckv_ref[...] = ckv
    kr_ref[...] = kr128[:, :ROPE_DIM]
    sg_ref[...] = _silu(g_att)
    u_ref[...] = u
    for g, w in enumerate(POOL_WINDOWS):
        lo, hi = g * POOL_GROUP, (g + 1) * POOL_GROUP
        wsum = u[:, lo:hi]
        for j in range(1, w):
            wsum = wsum + state_ref[POOL_PREV - j, :, lo:hi]
        pooled = wsum / float(w) - u[:, lo:hi]
        mp_ref[:, lo:hi] = _pool_out(pooled, g, wpool_ref, bpool_ref, pscale_ref, g_pool).astype(BF16)


def _inproj_s(x, mod, wts, tabs, state, wukt):
    n, d = x.shape
    args = (x, mod, *wts, *tabs, state, wukt)
    out_shape = [
        jax.ShapeDtypeStruct((n, N_HEADS * HEAD_PAD), F32),
        jax.ShapeDtypeStruct((n, N_HEADS * HEAD_PAD), F32),
        jax.ShapeDtypeStruct((n, KV_LORA), F32),
        jax.ShapeDtypeStruct((n, ROPE_DIM), F32),
        jax.ShapeDtypeStruct((n, D_ATT), F32),
        jax.ShapeDtypeStruct((n, D_POOL), BF16),
        jax.ShapeDtypeStruct((n, D_POOL), F32),
    ]
    return pl.pallas_call(
        _inproj_s_kernel,
        in_specs=[_full(a.shape) for a in args],
        out_specs=[_full(o.shape) for o in out_shape],
        out_shape=out_shape,
        grid=(1,),
        compiler_params=pltpu.CompilerParams(dimension_semantics=("arbitrary",), vmem_limit_bytes=VMEM_LIMIT),
        name="inproj_s",
    )(*args)


def _flash_tile(qi, q_ref, k_ref, vt_ref, sg_ref, o_ref, st_ref, between, side, *, tq, tk):
    assert tq == tk
    cw = tq // COL_SPLIT
    units = [(hh, cs) for hh in range(2) for cs in range(COL_SPLIT)]
    qs = [q_ref[0, cs * cw:(cs + 1) * cw, hh * HEAD_PAD:(hh + 1) * HEAD_PAD] for hh, cs in units]

    def produce(j, slot, u, rows=None):
        hh, cs = units[u]
        r0, r1 = (0, tk) if rows is None else rows
        kb = k_ref[0, pl.ds(pl.multiple_of(j * tk, tk) + r0, r1 - r0), hh * HEAD_PAD:(hh + 1) * HEAD_PAD]
        st_ref[slot][hh][r0:r1, cs * cw:(cs + 1) * cw] = lax.dot_general(kb, qs[u], _NT,
                                                                         preferred_element_type=F32)

    def consume(j, slot, u, state, diagonal=False):
        hh, cs = units[u]
        m, acc = state
        st = st_ref[slot][hh][:, cs * cw:(cs + 1) * cw]
        if diagonal:
            kpos = lax.broadcasted_iota(jnp.int32, (tk, cw), 0)
            qpos = cs * cw + lax.broadcasted_iota(jnp.int32, (tk, cw), 1)
            st = jnp.where(kpos <= qpos, st, -jnp.inf)
        vtb = vt_ref[0, hh * V_DIM:(hh + 1) * V_DIM, pl.ds(pl.multiple_of(j * tk, tk), tk)]
        vtb = jnp.concatenate([vtb, jnp.ones((SUM_ROWS, tk), BF16)], axis=0)
        m_new = jnp.maximum(m, jnp.max(st, axis=0, keepdims=True))
        alpha = jnp.exp2(m - m_new)
        pt = jnp.exp2(st - m_new).astype(BF16)
        return m_new, alpha * acc + jnp.dot(vtb, pt, preferred_element_type=F32)

    def finish(states):
        heads = []
        for hh in range(2):
            acc = jnp.concatenate([states[u][1] for u in range(len(units)) if units[u][0] == hh], axis=1)
            heads.append(acc[:V_DIM] / acc[V_DIM:V_DIM + 1])
        o_ref[0] = (jnp.concatenate(heads, axis=0).T * sg_ref[0]).astype(BF16)

    def pair(t, states):
        j = 2 * t
        states = list(states)
        for u in range(len(units)):
            produce(j + 1, 1, u)
            states[u] = consume(j, 0, u, states[u])
        for u in range(len(units)):
            produce(j + 2, 0, u)
            states[u] = consume(j + 1, 1, u, states[u])
        return tuple(states)

    init = tuple((jnp.full((1, cw), -jnp.inf, F32), jnp.zeros((V_DIM + SUM_ROWS, cw), F32)) for _ in units)
    pieces = FIRST_CHUNKS // len(units)
    for u in range(len(units)):
        for c in range(pieces):
            produce(0, 0, u, (c * tk // pieces, (c + 1) * tk // pieces))
            between(u * pieces + c)
    states = lax.fori_loop(0, qi // 2, pair, init)

    def run_interleaved(steps, pieces, counts):
        pieces = list(pieces)
        for step, count in zip(steps, counts):
            for piece in pieces[:count]:
                piece()
            del pieces[:count]
            step()
        for piece in pieces:
            piece()

    @pl.when(qi % 2 == 0)
    def _():
        wait, pieces = side(0)
        wait()
        done = [None] * len(units)

        def last(u):
            def run():
                done[u] = consume(qi, 0, u, states[u], diagonal=True)
            return run
        run_interleaved([last(u) for u in range(len(units))], pieces, EVEN_DEAL)
        finish(done)

    @pl.when(qi % 2 == 1)
    def _():
        wait, pieces = side(1)
        wait()
        mid, done = [None] * len(units), [None] * len(units)

        def before_last(u):
            def run():
                produce(qi, 1, u)
                mid[u] = consume(qi - 1, 0, u, states[u])
            return run

        def last(u):
            def run():
                done[u] = consume(qi, 1, u, mid[u], diagonal=True)
            return run
        run_interleaved([before_last(u) for u in range(len(units))] + [last(u) for u in range(len(units))], pieces, ODD_DEAL)
        finish(done)


def _page_copies(pt_ref, cckv_hbm, ckrt_hbm, ckv_buf, krt_buf, sem, n_pages, row, sl, p):
    page = pt_ref[row * n_pages + p]
    cols = pl.ds(p * PAGE_SIZE if isinstance(p, int) else pl.multiple_of(p * PAGE_SIZE, PAGE_SIZE), PAGE_SIZE)
    return (pltpu.make_async_copy(cckv_hbm.at[0, page], ckv_buf.at[sl, cols], sem.at[0, sl]),
            pltpu.make_async_copy(ckrt_hbm.at[0, page], krt_buf.at[sl, :, cols], sem.at[1, sl]))


def _decode_stages(pt_ref, qa_ref, qr_ref, ckvn_ref, krn_ref, wv_ref, sg_ref, cckv_hbm, ckrt_hbm,
                   o_ref, ckv_buf, krt_buf, sem, *, row, slot, n_pages):
    def wait():
        def wait_body(p, _):
            for cp in _page_copies(pt_ref, cckv_hbm, ckrt_hbm, ckv_buf, krt_buf, sem, n_pages, row, slot, p):
                cp.wait()
            return 0
        lax.fori_loop(0, n_pages, wait_body, 0, unroll=8)

    past = ckv_buf.shape[1]
    ck = past // DEC_CHUNKS
    env = {"ckv": [], "s": []}

    def scores(c):
        def run():
            if c == 0:
                env["qa"] = qa_ref[0].astype(BF16)
                env["qr"] = qr_ref[0].astype(BF16)
            ckv = ckv_buf[slot, c * ck:(c + 1) * ck, :].astype(BF16)
            krt = krt_buf[slot, :, c * ck:(c + 1) * ck].astype(BF16)
            env["ckv"].append(ckv)
            env["s"].append(lax.dot_general(env["qa"], ckv, _NT, preferred_element_type=F32)
                            + jnp.dot(env["qr"], krt, preferred_element_type=F32))
        return run

    def values(c):
        def run():
            if c == 0:
                qa, qr = env["qa"], env["qr"]
                s = jnp.concatenate(env["s"], axis=1)
                ckvn = ckvn_ref[0].astype(BF16).astype(F32)
                krn = krn_ref[0].astype(BF16).astype(F32)
                s_n = (jnp.sum(qa.astype(F32) * ckvn, axis=1, keepdims=True)
                       + jnp.sum(qr.astype(F32) * krn, axis=1, keepdims=True))
                m = jnp.maximum(jnp.max(s, axis=1, keepdims=True), s_n)
                p = jnp.exp2((s - m) * QK_PRESCALE)
                p_n = jnp.exp2((s_n - m) * QK_PRESCALE)
                env["l"] = jnp.sum(p, axis=1, keepdims=True) + p_n
                env["p"] = p.astype(BF16)
                env["o"] = p_n.astype(BF16).astype(F32) * ckvn
            env["o"] = env["o"] + jnp.dot(env["p"][:, c * ck:(c + 1) * ck], env["ckv"][c],
                                          preferred_element_type=F32)
            if c == DEC_CHUNKS - 1:
                o_all = _bdot(env["o"] / env["l"], wv_ref[...])
                hrow = lax.broadcasted_iota(jnp.int32, o_all.shape, 0)
                hcol = lax.broadcasted_iota(jnp.int32, o_all.shape, 1) // V_DIM
                att = jnp.sum(jnp.where(hrow == hcol, o_all, 0.0), axis=0, keepdims=True)
                o_ref[0] = (att * sg_ref[0]).astype(BF16)
        return run

    return wait, [scores(c) for c in range(DEC_CHUNKS)] + [values(c) for c in range(DEC_CHUNKS)]


def _attn_kernel(pt_ref, q_ref, k_ref, vt_ref, sgp_ref, qa_ref, qr_ref, ckvn_ref, krn_ref, wv_ref, sgs_ref,
                 cckv_hbm, ckrt_hbm, op_ref, os_ref, st00, st01, st10, st11, ckv_buf, krt_buf, sem,
                 *, tq, tk, n_pages):
    qi = pl.program_id(2)
    r = (pl.program_id(0) * pl.num_programs(1) + pl.program_id(1)) * pl.num_programs(2) + qi
    n_rows = pl.num_programs(0) * pl.num_programs(1) * pl.num_programs(2)
    slot = r % 2
    dma = (pt_ref, cckv_hbm, ckrt_hbm, ckv_buf, krt_buf, sem, n_pages)

    @pl.when(r == 0)
    def _():
        def body(p, _):
            for cp in _page_copies(*dma, 0, 0, p):
                cp.start()
            return 0
        lax.fori_loop(0, n_pages, body, 0, unroll=8)

    nxt = jnp.minimum(r + 1, n_rows - 1)
    assert n_pages % FIRST_CHUNKS == 0
    per_chunk = n_pages // FIRST_CHUNKS

    def issue(c):
        for p in range(c * per_chunk, (c + 1) * per_chunk):
            for cp in _page_copies(*dma, nxt, 1 - slot, p):
                cp.start()

    def row_stages(parity):
        return _decode_stages(pt_ref, qa_ref, qr_ref, ckvn_ref, krn_ref, wv_ref, sgs_ref, cckv_hbm, ckrt_hbm,
                              os_ref, ckv_buf, krt_buf, sem, row=r, slot=parity, n_pages=n_pages)

    _flash_tile(qi, q_ref, k_ref, vt_ref, sgp_ref, op_ref, ((st00, st01), (st10, st11)), issue, row_stages,
                tq=tq, tk=tk)

    @pl.when(r == n_rows - 1)
    def _():
        def body(p, _):
            for cp in _page_copies(*dma, nxt, 1 - slot, p):
                cp.wait()
            return 0
        lax.fori_loop(0, n_pages, body, 0, unroll=8)


def _attention(q, k, vt, sg_p, page_table, qabs, qrope, ckv_new, kr_new, wv, sg_s, cache_ckv, cache_krt,
               tq=512, tk=512):
    b, s, _ = q.shape
    n_pairs = N_HEADS // 2
    nq = s // tq
    n, n_pages = page_table.shape
    assert n == b * n_pairs * nq, "one sample row per prompt attention step"
    assert nq % 2 == 0, "row parity must equal q-tile parity"
    past = n_pages * PAGE_SIZE
    row = lambda bi, hp, qi: (bi * n_pairs + hp) * nq + qi
    per_r = lambda w: pl.BlockSpec((1, 1, w), lambda bi, hp, qi, pt: (row(bi, hp, qi), 0, 0))
    per_h = lambda w: pl.BlockSpec((1, N_HEADS, w), lambda bi, hp, qi, pt: (row(bi, hp, qi), 0, 0))
    grid_spec = pltpu.PrefetchScalarGridSpec(
        num_scalar_prefetch=1,
        grid=(b, n_pairs, nq),
        in_specs=[pl.BlockSpec((1, tq, 2 * HEAD_PAD), lambda bi, hp, qi, pt: (bi, qi, hp)),
                  pl.BlockSpec((1, s, 2 * HEAD_PAD), lambda bi, hp, qi, pt: (bi, 0, hp)),
                  pl.BlockSpec((1, 2 * V_DIM, s), lambda bi, hp, qi, pt: (bi, hp, 0)),
                  pl.BlockSpec((1, tq, 2 * V_DIM), lambda bi, hp, qi, pt: (bi, qi, hp)),
                  per_h(KV_LORA), per_h(ROPE_DIM), per_r(KV_LORA), per_r(ROPE_DIM),
                  pl.BlockSpec(wv.shape, lambda bi, hp, qi, pt: (0, 0)),
                  per_r(D_ATT),
                  pl.BlockSpec(memory_space=pl.ANY),
                  pl.BlockSpec(memory_space=pl.ANY)],
        out_specs=[pl.BlockSpec((1, tq, 2 * V_DIM), lambda bi, hp, qi, pt: (bi, qi, hp)), per_r(D_ATT)],
        scratch_shapes=[pltpu.VMEM((tk, tq), F32)] * 4 + [
            pltpu.VMEM((2, past, KV_LORA), F32),
            pltpu.VMEM((2, ROPE_DIM, past), F32),
            pltpu.SemaphoreType.DMA((2, 2))],
    )
    return pl.pallas_call(
        functools.partial(_attn_kernel, tq=tq, tk=tk, n_pages=n_pages),
        grid_spec=grid_spec,
        out_shape=[jax.ShapeDtypeStruct((b, s, D_ATT), BF16), jax.ShapeDtypeStruct((n, 1, D_ATT), BF16)],
        compiler_params=pltpu.CompilerParams(dimension_semantics=("arbitrary", "arbitrary", "arbitrary"),
                                             vmem_limit_bytes=VMEM_LIMIT),
        name="attention",
    )(page_table.reshape(-1), q, k, vt, sg_p, qabs, qrope, ckv_new, kr_new, wv, sg_s, cache_ckv, cache_krt)


def _combine_kernel(x_ref, gate_ref, ag_ref, mp_ref, wo_ref, fnw_ref, y_ref):
    tm = x_ref.shape[1]
    hr = tm // ROW_PARTS
    parts = [slice(p * hr, (p + 1) * hr) for p in range(ROW_PARTS)]
    projs = [jnp.dot(ag_ref[0, rows, :], wo_ref[0:D_ATT, :], preferred_element_type=F32)
             + jnp.dot(mp_ref[0, rows, :], wo_ref[D_ATT:D_ATT + D_POOL, :], preferred_element_type=F32)
             for rows in parts]
    for rows, proj in zip(parts, projs):
        gate = gate_ref[0] if gate_ref.shape[1] == 1 else gate_ref[0, rows, :]
        y_ref[0, rows, :] = _rms(x_ref[0, rows, :] + gate * proj, fnw_ref[...])


def _combine(x, gate, ag, mp, wo, fnw, tm):
    b, s, d = x.shape
    gr = gate.shape[1]
    gate_spec = (pl.BlockSpec((1, 1, d), lambda bi, i: (bi, 0, 0)) if gr == 1
                 else pl.BlockSpec((1, tm, d), lambda bi, i: (bi, i, 0)))
    row = lambda w: pl.BlockSpec((1, tm, w), lambda bi, i: (bi, i, 0))
    return pl.pallas_call(
        _combine_kernel,
        grid=(b, s // tm),
        in_specs=[row(d), gate_spec, row(D_ATT), row(D_POOL), _full(wo.shape), _full(fnw.shape)],
        out_specs=row(d),
        out_shape=jax.ShapeDtypeStruct((b, s, d), F32),
        compiler_params=pltpu.CompilerParams(dimension_semantics=("arbitrary", "arbitrary"),
                                             vmem_limit_bytes=VMEM_LIMIT),
        name="combine",
    )(x, gate, ag, mp, wo, fnw)


def _rope_tables(offsets, bases):
    inv = ROPE_BASE ** (-jnp.arange(0, ROPE_DIM, 2, dtype=F32) / ROPE_DIM)
    inv = jnp.pad(jnp.concatenate([inv, inv]), (0, HEAD_PAD - ROPE_DIM))
    lanes = (jnp.arange(HEAD_PAD) < ROPE_DIM).astype(F32)
    a_off = offsets.astype(F32)[:, None] * inv[None, :]
    a_base = bases.astype(F32)[:, None, None] * inv[None, None, :]
    return jnp.cos(a_off) * lanes, jnp.sin(a_off) * lanes, jnp.cos(a_base), jnp.sin(a_base)


def _rot_half(w):
    half = w.shape[-1] // 2
    return jnp.concatenate([-w[..., half:], w[..., :half]], axis=-1)


def _pack_weights(norm_w, w_in, q_norm_w, w_uq, kv_norm_w, w_uk, w_uv, w_pool, b_pool, pool_scale):
    d = w_in.shape[0]
    off_kr = Q_LORA + KV_LORA
    wt = jnp.swapaxes(w_in, 0, 1).astype(BF16)
    krc = wt[off_kr:off_kr + ROPE_DIM]
    win = jnp.concatenate([wt[:off_kr], krc, -krc[ROPE_DIM // 2:], krc[:ROPE_DIM // 2],
                           jnp.zeros((HEAD_PAD - 2 * ROPE_DIM, d), BF16), wt[off_kr + ROPE_DIM:]], axis=0)
    pad_q = HEAD_PAD - QK_DIM
    wq = jnp.pad(w_uq, ((0, 0), (0, 0), (0, pad_q))).reshape(Q_LORA, N_HEADS * HEAD_PAD)
    wqr = jnp.pad(_rot_half(w_uq[..., NOPE_DIM:]), ((0, 0), (0, 0), (NOPE_DIM, pad_q)))
    wqr = wqr.reshape(Q_LORA, N_HEADS * HEAD_PAD)
    wk_nope = jnp.pad(w_uk, ((0, 0), (0, 0), (0, HEAD_PAD - NOPE_DIM))).reshape(KV_LORA, N_HEADS * HEAD_PAD)
    place = jnp.pad(jnp.eye(ROPE_DIM, dtype=F32), ((0, HEAD_PAD - ROPE_DIM), (NOPE_DIM, pad_q)))
    wk = jnp.concatenate([wk_nope, jnp.tile(place, (1, N_HEADS))], axis=0)
    wv = w_uv.reshape(KV_LORA, N_HEADS * V_DIM).astype(BF16)
    shared = (norm_w.reshape(1, -1), win, q_norm_w.reshape(1, -1), kv_norm_w.reshape(1, -1),
              wq.astype(BF16), wqr.astype(BF16), w_pool.astype(BF16), b_pool.reshape(1, -1),
              pool_scale.reshape(1, -1))
    return shared, wk.astype(BF16), wv


def kernel(x_prompt, x_sample, c_prompt, c_sample, cache_ckv, cache_krope, state_pool, page_table, ada_w, ada_b, norm_w, w_in, q_norm_w, w_uq, kv_norm_w, w_uk, w_uv, w_pool, b_pool, pool_scale, w_out, final_norm_w):
    assert ada_w.shape[0] == 1, "single-layer trunk only"
    b, s, d = x_prompt.shape
    n = x_sample.shape[0]
    assert x_sample.shape[1] == 1
    past_len = page_table.shape[1] * PAGE_SIZE

    c_all = jnp.concatenate([c_prompt, c_sample], axis=0)
    rows = -(-c_all.shape[0] // 8) * 8
    c_all = jnp.pad(c_all, ((0, rows - c_all.shape[0]), (0, 0)))
    mod = _ada(c_all, ada_w[0], ada_b[0].reshape(1, -1))
    mod_p = mod[:b].reshape(b, 3, d)
    mod_s = mod[b:b + n]

    wts, wk, wv = _pack_weights(norm_w[0], w_in[0], q_norm_w[0], w_uq[0], kv_norm_w[0], w_uk[0], w_uv[0],
                                w_pool[0], b_pool[0], pool_scale[0])
    wo = w_out[0].astype(BF16)
    fnw = final_norm_w.reshape(1, -1)

    tabs_p = _rope_tables(jnp.arange(ROW_TILE), jnp.arange(0, s, ROW_TILE))
    q, k, vt, ckv_p, krt_p, sg_p, mp_p, utail = _inproj_p(x_prompt, mod_p, wts, wk, wv.T, tabs_p, tm=ROW_TILE)

    tabs_s = _rope_tables(jnp.zeros((n,)), jnp.full((1,), past_len))
    wukt = jnp.pad(jnp.transpose(w_uk[0], (1, 2, 0)), ((0, 0), (0, HEAD_PAD - NOPE_DIM), (0, 0))).astype(BF16)
    state_t = jnp.transpose(state_pool[0], (1, 0, 2))
    qabs, qrope, ckv_s, kr_s, sg_s, mp_s, u_s = _inproj_s(x_sample[:, 0, :], mod_s, wts, tabs_s, state_t, wukt)
    qrope = qrope.reshape(n, N_HEADS, HEAD_PAD)[:, :, NOPE_DIM:QK_DIM]
    ag_p, ag_s = _attention(q, k, vt, sg_p, page_table, qabs.reshape(n, N_HEADS, KV_LORA), qrope,
                            ckv_s[:, None, :], kr_s[:, None, :], wv, sg_s[:, None, :],
                            cache_ckv, jnp.swapaxes(cache_krope, 2, 3))
    y_prompt = _combine(x_prompt, mod_p[:, 2:3, :], ag_p, mp_p, wo, fnw, tm=2 * ROW_TILE)
    y_sample = _combine(x_sample.reshape(1, n, d), mod_s[None, :, 2 * d:], ag_s.reshape(1, n, D_ATT),
                        mp_s[None], wo, fnw, tm=n).reshape(n, 1, d)

    new_pool_p = utail[:, HALO - POOL_PREV:, :]
    new_pool_s = jnp.transpose(jnp.concatenate([state_t[1:], u_s[None]], axis=0), (1, 0, 2))
    return (y_prompt, y_sample,
            ckv_p[None], jnp.swapaxes(krt_p, 1, 2)[None], new_pool_p[None],
            ckv_s[:, None, :][None], kr_s[:, None, :][None], new_pool_s[None])
```

```python
import functools
import math

import jax
import jax.numpy as jnp
from jax import lax
from jax.experimental import pallas as pl
from jax.experimental.pallas import tpu as pltpu

F32 = jnp.float32
BF16 = jnp.bfloat16

N_HEADS = 8
NOPE_DIM = 64
ROPE_DIM = 32
QK_DIM = NOPE_DIM + ROPE_DIM
V_DIM = 64
Q_LORA = 256
KV_LORA = 128
D_POOL = 512
D_ATT = 512
POOL_WINDOWS = (2, 4, 8, 16)
POOL_GROUP = 128
POOL_PREV = 15
PAGE_SIZE = 128
ROPE_BASE = 10000.0
EPS = 1e-6
SM_SCALE = QK_DIM ** -0.5
LOG2E = math.log2(math.e)
QK_PRESCALE = SM_SCALE * LOG2E
_NT = (((1,), (1,)), ((), ()))

HEAD_PAD = 128
SUM_ROWS = 16
HALO = 16
C_Q, C_KV, C_KR, C_GA, C_U, C_GP, C_END = 0, 256, 384, 512, 1024, 1536, 2048

ROW_TILE = 512
ROW_PARTS = 2
COL_SPLIT = 1
FIRST_CHUNKS = 2
DEC_CHUNKS = 4
LOOP_BLOCKS = 4
LAST_DEAL = ((2, 2), (2, 2, 1, 1), (2, 2, 1, 1, 1, 1), (1, 1, 1, 1, 1, 1, 1, 1))

VMEM_LIMIT = 56 * 1024 * 1024


def _rms(x, w):
    return (x * lax.rsqrt(jnp.mean(x * x, axis=-1, keepdims=True) + EPS)) * w


def _silu(x):
    return x * jax.nn.sigmoid(x)


def _bdot(a, b):
    return jnp.dot(a.astype(BF16), b, preferred_element_type=F32)


def _ada_kernel(c_ref, w_ref, b_ref, o_ref):
    o_ref[...] = _bdot(_silu(c_ref[...]), w_ref[...].astype(BF16)) + b_ref[...]


def _ada(c_all, ada_w, ada_b):
    rows, d = c_all.shape
    n = ada_w.shape[1]
    bn = 1024
    return pl.pallas_call(
        _ada_kernel,
        grid=(n // bn,),
        in_specs=[pl.BlockSpec((rows, d), lambda j: (0, 0)),
                  pl.BlockSpec((d, bn), lambda j: (0, j)),
                  pl.BlockSpec((1, bn), lambda j: (0, j))],
        out_specs=pl.BlockSpec((rows, bn), lambda j: (0, j)),
        out_shape=jax.ShapeDtypeStruct((rows, n), F32),
        compiler_params=pltpu.CompilerParams(dimension_semantics=("arbitrary",)),
        name="ada",
    )(c_all, ada_w, ada_b)


def _project_z(x, shift, scale, nw, win):
    h = _rms(x, nw) * (1.0 + scale) + shift
    return lax.dot_general(h.astype(BF16), win, _NT, preferred_element_type=F32)


def _project_rest(z, qnw, kvnw, wq, wqr, cr, sr, cb, sb):
    cosk = cb * cr - sb * sr
    sink = sb * cr + cb * sr
    nope = (lax.broadcasted_iota(jnp.int32, (1, HEAD_PAD), 1) < NOPE_DIM).astype(F32)
    cosq = pltpu.roll(cosk, NOPE_DIM, axis=1) + nope
    sinq = pltpu.roll(sink, NOPE_DIM, axis=1)
    q_lat = _rms(z[:, C_Q:C_KV], qnw)
    ckv = _rms(z[:, C_KV:C_KR], kvnw)
    krc = z[:, C_KR:C_GA]
    kr128 = krc * cosk + pltpu.roll(krc, HEAD_PAD - ROPE_DIM, axis=1) * sink
    ql = q_lat.astype(BF16)
    qa = jnp.dot(ql, wq, preferred_element_type=F32)
    qb = jnp.dot(ql, wqr, preferred_element_type=F32)
    return qa, qb, ckv, kr128, z[:, C_GA:C_U], z[:, C_U:C_GP], z[:, C_GP:C_END], cosq, sinq


def _pool_out(pooled_g, g, wpool_ref, bpool_ref, pscale_ref, g_pool):
    lo, hi = g * POOL_GROUP, (g + 1) * POOL_GROUP
    y = _bdot(pooled_g, wpool_ref[g]) + bpool_ref[:, lo:hi]
    y = y * pscale_ref[:, lo:hi]
    return y * _silu(g_pool[:, lo:hi])


def _inproj_p_kernel(x_ref, mod_ref, nw_ref, win_ref, qnw_ref, kvnw_ref, wq_ref, wqr_ref,
                     wpool_ref, bpool_ref, pscale_ref, wk_ref, wvt_ref, cr_ref, sr_ref, cb_ref, sb_ref,
                     q_ref, k_ref, vt_ref, ckv_ref, krt_ref, sg_ref, mp_ref, utail_ref, uext_ref, *, tm):
    i = pl.program_id(1)
    shift = mod_ref[0, 0:1, :]
    scale = mod_ref[0, 1:2, :]

    @pl.when(i == 0)
    def _():
        uext_ref[0:HALO, :] = jnp.zeros((HALO, D_POOL), F32)

    hr = tm // ROW_PARTS
    zs = []
    for part in range(ROW_PARTS):
        rows = slice(part * hr, (part + 1) * hr)
        z = _project_z(x_ref[0, rows, :], shift, scale, nw_ref[...], win_ref[...])
        uext_ref[HALO + part * hr:HALO + (part + 1) * hr, :] = z[:, C_U:C_GP]
        zs.append(z)

    for part in range(ROW_PARTS):
        r0 = part * hr
        rows = slice(r0, r0 + hr)
        qa, qb, ckv, kr128, g_att, u, g_pool, cosq, sinq = _project_rest(
            zs[part], qnw_ref[...], kvnw_ref[...], wq_ref[...], wqr_ref[...],
            cr_ref[rows, :], sr_ref[rows, :], cb_ref[0], sb_ref[0])
        cosq, sinq = cosq * QK_PRESCALE, sinq * QK_PRESCALE
        for h in range(N_HEADS):
            sl = slice(h * HEAD_PAD, (h + 1) * HEAD_PAD)
            q_ref[0, rows, sl] = (qa[:, sl] * cosq + qb[:, sl] * sinq).astype(BF16)
        ckv_b = ckv.astype(BF16)
        k = jnp.dot(jnp.concatenate([ckv_b, kr128.astype(BF16)], axis=1), wk_ref[...],
                    preferred_element_type=F32)
        k_ref[0, rows, :] = k.astype(BF16)
        vt_ref[0, :, rows] = lax.dot_general(wvt_ref[...], ckv_b, _NT, preferred_element_type=F32).astype(BF16)
        ckv_ref[0, rows, :] = ckv
        krt_ref[0, :, rows] = kr128.T[:ROPE_DIM, :]
        sg_ref[0, rows, :] = _silu(g_att)
        pos = (i * tm + r0 + lax.broadcasted_iota(jnp.int32, (hr, 1), 0)).astype(F32)
        for g, w in enumerate(POOL_WINDOWS):
            lo, hi = g * POOL_GROUP, (g + 1) * POOL_GROUP
            wsum = u[:, lo:hi]
            for d in range(1, w):
                wsum = wsum + uext_ref[HALO + r0 - d:HALO + r0 - d + hr, lo:hi]
            cnt = jnp.minimum(pos + 1.0, float(w))
            pooled = wsum / cnt - u[:, lo:hi]
            mp_ref[0, rows, lo:hi] = _pool_out(pooled, g, wpool_ref, bpool_ref, pscale_ref, g_pool).astype(BF16)

    u_last = zs[-1][hr - HALO:hr, C_U:C_GP]
    uext_ref[0:HALO, :] = u_last

    @pl.when(i == pl.num_programs(1) - 1)
    def _():
        utail_ref[0] = u_last


def _full(shape):
    nd = len(shape)
    return pl.BlockSpec(shape, lambda *_: (0,) * nd)


def _inproj_p(x, mod3, wts, wk, wvt, tabs, tm=512):
    b, s, d = x.shape
    nt = s // tm
    row = lambda w: pl.BlockSpec((1, tm, w), lambda bi, i: (bi, i, 0))
    rtab = _full((tm, HEAD_PAD))
    btab = pl.BlockSpec((1, 1, HEAD_PAD), lambda bi, i: (i, 0, 0))
    in_specs = [row(d), pl.BlockSpec((1, 3, d), lambda bi, i: (bi, 0, 0))]
    in_specs += [_full(w.shape) for w in (*wts, wk, wvt)]
    in_specs += [rtab, rtab, btab, btab]
    out_shape = [
        jax.ShapeDtypeStruct((b, s, N_HEADS * HEAD_PAD), BF16),
        jax.ShapeDtypeStruct((b, s, N_HEADS * HEAD_PAD), BF16),
        jax.ShapeDtypeStruct((b, D_ATT, s), BF16),
        jax.ShapeDtypeStruct((b, s, KV_LORA), F32),
        jax.ShapeDtypeStruct((b, ROPE_DIM, s), F32),
        jax.ShapeDtypeStruct((b, s, D_ATT), F32),
        jax.ShapeDtypeStruct((b, s, D_POOL), BF16),
        jax.ShapeDtypeStruct((b, HALO, D_POOL), F32),
    ]
    out_specs = [row(N_HEADS * HEAD_PAD), row(N_HEADS * HEAD_PAD),
                 pl.BlockSpec((1, D_ATT, tm), lambda bi, i: (bi, 0, i)), row(KV_LORA),
                 pl.BlockSpec((1, ROPE_DIM, tm), lambda bi, i: (bi, 0, i)),
                 row(D_ATT), row(D_POOL), pl.BlockSpec((1, HALO, D_POOL), lambda bi, i: (bi, 0, 0))]
    return pl.pallas_call(
        functools.partial(_inproj_p_kernel, tm=tm),
        grid=(b, nt),
        in_specs=in_specs,
        out_specs=out_specs,
        out_shape=out_shape,
        scratch_shapes=[pltpu.VMEM((tm + HALO, D_POOL), F32)],
        compiler_params=pltpu.CompilerParams(dimension_semantics=("arbitrary", "arbitrary"),
                                             vmem_limit_bytes=VMEM_LIMIT),
        name="inproj_p",
    )(x, mod3, *wts, wk, wvt, *tabs)


def _inproj_s_kernel(x_ref, mod_ref, nw_ref, win_ref, qnw_ref, kvnw_ref, wq_ref, wqr_ref,
                     wpool_ref, bpool_ref, pscale_ref, cr_ref, sr_ref, cb_ref, sb_ref,
                     state_ref, wukt_ref,
                     qabs_ref, qrope_ref, ckv_ref, kr_ref, sg_ref, mp_ref, u_ref):
    d = x_ref.shape[1]
    shift = mod_ref[:, 0:d]
    scale = mod_ref[:, d:2 * d]
    z = _project_z(x_ref[...], shift, scale, nw_ref[...], win_ref[...])
    qa, qb, ckv, kr128, g_att, u, g_pool, cosq, sinq = _project_rest(
        z, qnw_ref[...], kvnw_ref[...], wq_ref[...], wqr_ref[...], cr_ref[...], sr_ref[...], cb_ref[0], sb_ref[0])
    for h in range(N_HEADS):
        sl = slice(h * HEAD_PAD, (h + 1) * HEAD_PAD)
        qrope_ref[:, sl] = qa[:, sl] * cosq + qb[:, sl] * sinq
        qabs_ref[:, sl] = _bdot(qa[:, sl], wukt_ref[h])
    ckv_ref[...] = ckv
    kr_ref[...] = kr128[:, :ROPE_DIM]
    sg_ref[...] = _silu(g_att)
    u_ref[...] = u
    for g, w in enumerate(POOL_WINDOWS):
        lo, hi = g * POOL_GROUP, (g + 1) * POOL_GROUP
        wsum = u[:, lo:hi]
        for j in range(1, w):
            wsum = wsum + state_ref[POOL_PREV - j, :, lo:hi]
        pooled = wsum / float(w) - u[:, lo:hi]
        mp_ref[:, lo:hi] = _pool_out(pooled, g, wpool_ref, bpool_ref, pscale_ref, g_pool).astype(BF16)


def _inproj_s(x, mod, wts, tabs, state, wukt):
    n, d = x.shape
    args = (x, mod, *wts, *tabs, state, wukt)
    out_shape = [
        jax.ShapeDtypeStruct((n, N_HEADS * HEAD_PAD), F32),
        jax.ShapeDtypeStruct((n, N_HEADS * HEAD_PAD), F32),
        jax.ShapeDtypeStruct((n, KV_LORA), F32),
        jax.ShapeDtypeStruct((n, ROPE_DIM), F32),
        jax.ShapeDtypeStruct((n, D_ATT), F32),
        jax.ShapeDtypeStruct((n, D_POOL), BF16),
        jax.ShapeDtypeStruct((n, D_POOL), F32),
    ]
    return pl.pallas_call(
        _inproj_s_kernel,
        in_specs=[_full(a.shape) for a in args],
        out_specs=[_full(o.shape) for o in out_shape],
        out_shape=out_shape,
        grid=(1,),
        compiler_params=pltpu.CompilerParams(dimension_semantics=("arbitrary",), vmem_limit_bytes=VMEM_LIMIT),
        name="inproj_s",
    )(*args)


def _flash_tile(qi, q_ref, k_ref, vt_ref, sg_ref, o_ref, st_ref, between, side, *, tq, tk):
    assert tq == tk
    cw = tq // COL_SPLIT
    units = [(hh, cs) for hh in range(2) for cs in range(COL_SPLIT)]
    qs = [q_ref[0, cs * cw:(cs + 1) * cw, hh * HEAD_PAD:(hh + 1) * HEAD_PAD] for hh, cs in units]

    def produce(j, slot, u, rows=None):
        hh, cs = units[u]
        r0, r1 = (0, tk) if rows is None else rows
        kb = k_ref[0, pl.ds(pl.multiple_of(j * tk, tk) + r0, r1 - r0), hh * HEAD_PAD:(hh + 1) * HEAD_PAD]
        st_ref[slot][hh][r0:r1, cs * cw:(cs + 1) * cw] = lax.dot_general(kb, qs[u], _NT,
                                                                         preferred_element_type=F32)

    def consume(j, slot, u, state, diagonal=False):
        hh, cs = units[u]
        m, acc = state
        st = st_ref[slot][hh][:, cs * cw:(cs + 1) * cw]
        if diagonal:
            kpos = lax.broadcasted_iota(jnp.int32, (tk, cw), 0)
            qpos = cs * cw + lax.broadcasted_iota(jnp.int32, (tk, cw), 1)
            st = jnp.where(kpos <= qpos, st, -jnp.inf)
        vtb = vt_ref[0, hh * V_DIM:(hh + 1) * V_DIM, pl.ds(pl.multiple_of(j * tk, tk), tk)]
        vtb = jnp.concatenate([vtb, jnp.ones((SUM_ROWS, tk), BF16)], axis=0)
        m_new = jnp.maximum(m, jnp.max(st, axis=0, keepdims=True))
        alpha = jnp.exp2(m - m_new)
        pt = jnp.exp2(st - m_new).astype(BF16)
        return m_new, alpha * acc + jnp.dot(vtb, pt, preferred_element_type=F32)

    def finish(states):
        heads = []
        for hh in range(2):
            acc = jnp.concatenate([states[u][1] for u in range(len(units)) if units[u][0] == hh], axis=1)
            heads.append(acc[:V_DIM] / acc[V_DIM:V_DIM + 1])
        o_ref[0] = (jnp.concatenate(heads, axis=0).T * sg_ref[0]).astype(BF16)

    def trip(t, states):
        j = LOOP_BLOCKS * t
        cur = list(states)
        for i in range(LOOP_BLOCKS):
            for u in range(len(units)):
                produce(j + i + 1, (i + 1) % 2, u)
                cur[u] = consume(j + i, i % 2, u, cur[u])
        return tuple(cur)

    init = tuple((jnp.full((1, cw), -jnp.inf, F32), jnp.zeros((V_DIM + SUM_ROWS, cw), F32)) for _ in units)
    pieces = FIRST_CHUNKS // len(units)
    for u in range(len(units)):
        for c in range(pieces):
            produce(0, 0, u, (c * tk // pieces, (c + 1) * tk // pieces))
            between(u * pieces + c)
    states = lax.fori_loop(0, qi // LOOP_BLOCKS, trip, init)

    def run_interleaved(steps, side_pieces, counts):
        side_pieces = list(side_pieces)
        for step, count in zip(steps, counts):
            for piece in side_pieces[:count]:
                piece()
            del side_pieces[:count]
            step()
        for piece in side_pieces:
            piece()

    for rest in range(LOOP_BLOCKS):
        @pl.when(qi % LOOP_BLOCKS == rest)
        def _():
            wait, side_pieces = side(rest % 2)
            wait()
            j0 = qi - rest
            cur = list(states)
            steps = []
            for i in range(rest):
                for u in range(len(units)):
                    def run(i=i, u=u):
                        produce(j0 + i + 1, (i + 1) % 2, u)
                        cur[u] = consume(j0 + i, i % 2, u, cur[u])
                    steps.append(run)
            for u in range(len(units)):
                def run(u=u):
                    cur[u] = consume(qi, rest % 2, u, cur[u], diagonal=True)
                steps.append(run)
            run_interleaved(steps, side_pieces, LAST_DEAL[rest])
            finish(cur)


def _page_copies(pt_ref, cckv_hbm, ckrt_hbm, ckv_buf, krt_buf, sem, n_pages, row, sl, p):
    page = pt_ref[row * n_pages + p]
    cols = pl.ds(p * PAGE_SIZE if isinstance(p, int) else pl.multiple_of(p * PAGE_SIZE, PAGE_SIZE), PAGE_SIZE)
    return (pltpu.make_async_copy(cckv_hbm.at[0, page], ckv_buf.at[sl, cols], sem.at[0, sl]),
            pltpu.make_async_copy(ckrt_hbm.at[0, page], krt_buf.at[sl, :, cols], sem.at[1, sl]))


def _decode_stages(pt_ref, qa_ref, qr_ref, ckvn_ref, krn_ref, wv_ref, sg_ref, cckv_hbm, ckrt_hbm,
                   o_ref, ckv_buf, krt_buf, sem, *, row, slot, n_pages):
    def wait():
        def wait_body(p, _):
            for cp in _page_copies(pt_ref, cckv_hbm, ckrt_hbm, ckv_buf, krt_buf, sem, n_pages, row, slot, p):
                cp.wait()
            return 0
        lax.fori_loop(0, n_pages, wait_body, 0, unroll=8)

    past = ckv_buf.shape[1]
    ck = past // DEC_CHUNKS
    env = {"ckv": [], "s": []}

    def scores(c):
        def run():
            if c == 0:
                env["qa"] = qa_ref[0].astype(BF16)
                env["qr"] = qr_ref[0].astype(BF16)
            ckv = ckv_buf[slot, c * ck:(c + 1) * ck, :].astype(BF16)
            krt = krt_buf[slot, :, c * ck:(c + 1) * ck].astype(BF16)
            env["ckv"].append(ckv)
            env["s"].append(lax.dot_general(env["qa"], ckv, _NT, preferred_element_type=F32)
                            + jnp.dot(env["qr"], krt, preferred_element_type=F32))
        return run

    def values(c):
        def run():
            if c == 0:
                qa, qr = env["qa"], env["qr"]
                s = jnp.concatenate(env["s"], axis=1)
                ckvn = ckvn_ref[0].astype(BF16).astype(F32)
                krn = krn_ref[0].astype(BF16).astype(F32)
                s_n = (jnp.sum(qa.astype(F32) * ckvn, axis=1, keepdims=True)
                       + jnp.sum(qr.astype(F32) * krn, axis=1, keepdims=True))
                m = jnp.maximum(jnp.max(s, axis=1, keepdims=True), s_n)
                p = jnp.exp2((s - m) * QK_PRESCALE)
                p_n = jnp.exp2((s_n - m) * QK_PRESCALE)
                env["l"] = jnp.sum(p, axis=1, keepdims=True) + p_n
                env["p"] = p.astype(BF16)
                env["o"] = p_n.astype(BF16).astype(F32) * ckvn
            env["o"] = env["o"] + jnp.dot(env["p"][:, c * ck:(c + 1) * ck], env["ckv"][c],
                                          preferred_element_type=F32)
            if c == DEC_CHUNKS - 1:
                o_all = _bdot(env["o"] / env["l"], wv_ref[...])
                hrow = lax.broadcasted_iota(jnp.int32, o_all.shape, 0)
                hcol = lax.broadcasted_iota(jnp.int32, o_all.shape, 1) // V_DIM
                att = jnp.sum(jnp.where(hrow == hcol, o_all, 0.0), axis=0, keepdims=True)
                o_ref[0] = (att * sg_ref[0]).astype(BF16)
        return run

    return wait, [scores(c) for c in range(DEC_CHUNKS)] + [values(c) for c in range(DEC_CHUNKS)]


def _attn_kernel(pt_ref, q_ref, k_ref, vt_ref, sgp_ref, qa_ref, qr_ref, ckvn_ref, krn_ref, wv_ref, sgs_ref,
                 cckv_hbm, ckrt_hbm, op_ref, os_ref, st00, st01, st10, st11, ckv_buf, krt_buf, sem,
                 *, tq, tk, n_pages):
    qi = pl.program_id(2)
    r = (pl.program_id(0) * pl.num_programs(1) + pl.program_id(1)) * pl.num_programs(2) + qi
    n_rows = pl.num_programs(0) * pl.num_programs(1) * pl.num_programs(2)
    slot = r % 2
    dma = (pt_ref, cckv_hbm, ckrt_hbm, ckv_buf, krt_buf, sem, n_pages)

    @pl.when(r == 0)
    def _():
        def body(p, _):
            for cp in _page_copies(*dma, 0, 0, p):
                cp.start()
            return 0
        lax.fori_loop(0, n_pages, body, 0, unroll=8)

    nxt = jnp.minimum(r + 1, n_rows - 1)
    assert n_pages % FIRST_CHUNKS == 0
    per_chunk = n_pages // FIRST_CHUNKS

    def issue(c):
        for p in range(c * per_chunk, (c + 1) * per_chunk):
            for cp in _page_copies(*dma, nxt, 1 - slot, p):
                cp.start()

    def row_stages(parity):
        return _decode_stages(pt_ref, qa_ref, qr_ref, ckvn_ref, krn_ref, wv_ref, sgs_ref, cckv_hbm, ckrt_hbm,
                              os_ref, ckv_buf, krt_buf, sem, row=r, slot=parity, n_pages=n_pages)

    _flash_tile(qi, q_ref, k_ref, vt_ref, sgp_ref, op_ref, ((st00, st01), (st10, st11)), issue, row_stages,
                tq=tq, tk=tk)

    @pl.when(r == n_rows - 1)
    def _():
        def body(p, _):
            for cp in _page_copies(*dma, nxt, 1 - slot, p):
                cp.wait()
            return 0
        lax.fori_loop(0, n_pages, body, 0, unroll=8)


def _attention(q, k, vt, sg_p, page_table, qabs, qrope, ckv_new, kr_new, wv, sg_s, cache_ckv, cache_krt,
               tq=512, tk=512):
    b, s, _ = q.shape
    n_pairs = N_HEADS // 2
    nq = s // tq
    n, n_pages = page_table.shape
    assert n == b * n_pairs * nq, "one sample row per prompt attention step"
    assert nq % 2 == 0, "row parity must equal q-tile parity"
    past = n_pages * PAGE_SIZE
    row = lambda bi, hp, qi: (bi * n_pairs + hp) * nq + qi
    per_r = lambda w: pl.BlockSpec((1, 1, w), lambda bi, hp, qi, pt: (row(bi, hp, qi), 0, 0))
    per_h = lambda w: pl.BlockSpec((1, N_HEADS, w), lambda bi, hp, qi, pt: (row(bi, hp, qi), 0, 0))
    grid_spec = pltpu.PrefetchScalarGridSpec(
        num_scalar_prefetch=1,
        grid=(b, n_pairs, nq),
        in_specs=[pl.BlockSpec((1, tq, 2 * HEAD_PAD), lambda bi, hp, qi, pt: (bi, qi, hp)),
                  pl.BlockSpec((1, s, 2 * HEAD_PAD), lambda bi, hp, qi, pt: (bi, 0, hp)),
                  pl.BlockSpec((1, 2 * V_DIM, s), lambda bi, hp, qi, pt: (bi, hp, 0)),
                  pl.BlockSpec((1, tq, 2 * V_DIM), lambda bi, hp, qi, pt: (bi, qi, hp)),
                  per_h(KV_LORA), per_h(ROPE_DIM), per_r(KV_LORA), per_r(ROPE_DIM),
                  pl.BlockSpec(wv.shape, lambda bi, hp, qi, pt: (0, 0)),
                  per_r(D_ATT),
                  pl.BlockSpec(memory_space=pl.ANY),
                  pl.BlockSpec(memory_space=pl.ANY)],
        out_specs=[pl.BlockSpec((1, tq, 2 * V_DIM), lambda bi, hp, qi, pt: (bi, qi, hp)), per_r(D_ATT)],
        scratch_shapes=[pltpu.VMEM((tk, tq), F32)] * 4 + [
            pltpu.VMEM((2, past, KV_LORA), F32),
            pltpu.VMEM((2, ROPE_DIM, past), F32),
            pltpu.SemaphoreType.DMA((2, 2))],
    )
    return pl.pallas_call(
        functools.partial(_attn_kernel, tq=tq, tk=tk, n_pages=n_pages),
        grid_spec=grid_spec,
        out_shape=[jax.ShapeDtypeStruct((b, s, D_ATT), BF16), jax.ShapeDtypeStruct((n, 1, D_ATT), BF16)],
        compiler_params=pltpu.CompilerParams(dimension_semantics=("arbitrary", "arbitrary", "arbitrary"),
                                             vmem_limit_bytes=VMEM_LIMIT),
        name="attention",
    )(page_table.reshape(-1), q, k, vt, sg_p, qabs, qrope, ckv_new, kr_new, wv, sg_s, cache_ckv, cache_krt)


def _combine_kernel(x_ref, gate_ref, ag_ref, mp_ref, wo_ref, fnw_ref, y_ref):
    tm = x_ref.shape[1]
    hr = tm // ROW_PARTS
    parts = [slice(p * hr, (p + 1) * hr) for p in range(ROW_PARTS)]
    projs = [jnp.dot(ag_ref[0, rows, :], wo_ref[0:D_ATT, :], preferred_element_type=F32)
             + jnp.dot(mp_ref[0, rows, :], wo_ref[D_ATT:D_ATT + D_POOL, :], preferred_element_type=F32)
             for rows in parts]
    for rows, proj in zip(parts, projs):
        gate = gate_ref[0] if gate_ref.shape[1] == 1 else gate_ref[0, rows, :]
        y_ref[0, rows, :] = _rms(x_ref[0, rows, :] + gate * proj, fnw_ref[...])


def _combine(x, gate, ag, mp, wo, fnw, tm):
    b, s, d = x.shape
    gr = gate.shape[1]
    gate_spec = (pl.BlockSpec((1, 1, d), lambda bi, i: (bi, 0, 0)) if gr == 1
                 else pl.BlockSpec((1, tm, d), lambda bi, i: (bi, i, 0)))
    row = lambda w: pl.BlockSpec((1, tm, w), lambda bi, i: (bi, i, 0))
    return pl.pallas_call(
        _combine_kernel,
        grid=(b, s // tm),
        in_specs=[row(d), gate_spec, row(D_ATT), row(D_POOL), _full(wo.shape), _full(fnw.shape)],
        out_specs=row(d),
        out_shape=jax.ShapeDtypeStruct((b, s, d), F32),
        compiler_params=pltpu.CompilerParams(dimension_semantics=("arbitrary", "arbitrary"),
                                             vmem_limit_bytes=VMEM_LIMIT),
        name="combine",
    )(x, gate, ag, mp, wo, fnw)


def _rope_tables(offsets, bases):
    inv = ROPE_BASE ** (-jnp.arange(0, ROPE_DIM, 2, dtype=F32) / ROPE_DIM)
    inv = jnp.pad(jnp.concatenate([inv, inv]), (0, HEAD_PAD - ROPE_DIM))
    lanes = (jnp.arange(HEAD_PAD) < ROPE_DIM).astype(F32)
    a_off = offsets.astype(F32)[:, None] * inv[None, :]
    a_base = bases.astype(F32)[:, None, None] * inv[None, None, :]
    return jnp.cos(a_off) * lanes, jnp.sin(a_off) * lanes, jnp.cos(a_base), jnp.sin(a_base)


def _rot_half(w):
    half = w.shape[-1] // 2
    return jnp.concatenate([-w[..., half:], w[..., :half]], axis=-1)


def _pack_weights(norm_w, w_in, q_norm_w, w_uq, kv_norm_w, w_uk, w_uv, w_pool, b_pool, pool_scale):
    d = w_in.shape[0]
    off_kr = Q_LORA + KV_LORA
    wt = jnp.swapaxes(w_in, 0, 1).astype(BF16)
    krc = wt[off_kr:off_kr + ROPE_DIM]
    win = jnp.concatenate([wt[:off_kr], krc, -krc[ROPE_DIM // 2:], krc[:ROPE_DIM // 2],
                           jnp.zeros((HEAD_PAD - 2 * ROPE_DIM, d), BF16), wt[off_kr + ROPE_DIM:]], axis=0)
    pad_q = HEAD_PAD - QK_DIM
    wq = jnp.pad(w_uq, ((0, 0), (0, 0), (0, pad_q))).reshape(Q_LORA, N_HEADS * HEAD_PAD)
    wqr = jnp.pad(_rot_half(w_uq[..., NOPE_DIM:]), ((0, 0), (0, 0), (NOPE_DIM, pad_q)))
    wqr = wqr.reshape(Q_LORA, N_HEADS * HEAD_PAD)
    wk_nope = jnp.pad(w_uk, ((0, 0), (0, 0), (0, HEAD_PAD - NOPE_DIM))).reshape(KV_LORA, N_HEADS * HEAD_PAD)
    place = jnp.pad(jnp.eye(ROPE_DIM, dtype=F32), ((0, HEAD_PAD - ROPE_DIM), (NOPE_DIM, pad_q)))
    wk = jnp.concatenate([wk_nope, jnp.tile(place, (1, N_HEADS))], axis=0)
    wv = w_uv.reshape(KV_LORA, N_HEADS * V_DIM).astype(BF16)
    shared = (norm_w.reshape(1, -1), win, q_norm_w.reshape(1, -1), kv_norm_w.reshape(1, -1),
              wq.astype(BF16), wqr.astype(BF16), w_pool.astype(BF16), b_pool.reshape(1, -1),
              pool_scale.reshape(1, -1))
    return shared, wk.astype(BF16), wv


def kernel(x_prompt, x_sample, c_prompt, c_sample, cache_ckv, cache_krope, state_pool, page_table, ada_w, ada_b, norm_w, w_in, q_norm_w, w_uq, kv_norm_w, w_uk, w_uv, w_pool, b_pool, pool_scale, w_out, final_norm_w):
    assert ada_w.shape[0] == 1, "single-layer trunk only"
    b, s, d = x_prompt.shape
    n = x_sample.shape[0]
    assert x_sample.shape[1] == 1
    past_len = page_table.shape[1] * PAGE_SIZE

    c_all = jnp.concatenate([c_prompt, c_sample], axis=0)
    rows = -(-c_all.shape[0] // 8) * 8
    c_all = jnp.pad(c_all, ((0, rows - c_all.shape[0]), (0, 0)))
    mod = _ada(c_all, ada_w[0], ada_b[0].reshape(1, -1))
    mod_p = mod[:b].reshape(b, 3, d)
    mod_s = mod[b:b + n]

    wts, wk, wv = _pack_weights(norm_w[0], w_in[0], q_norm_w[0], w_uq[0], kv_norm_w[0], w_uk[0], w_uv[0],
                                w_pool[0], b_pool[0], pool_scale[0])
    wo = w_out[0].astype(BF16)
    fnw = final_norm_w.reshape(1, -1)

    tabs_p = _rope_tables(jnp.arange(ROW_TILE), jnp.arange(0, s, ROW_TILE))
    q, k, vt, ckv_p, krt_p, sg_p, mp_p, utail = _inproj_p(x_prompt, mod_p, wts, wk, wv.T, tabs_p, tm=ROW_TILE)

    tabs_s = _rope_tables(jnp.zeros((n,)), jnp.full((1,), past_len))
    wukt = jnp.pad(jnp.transpose(w_uk[0], (1, 2, 0)), ((0, 0), (0, HEAD_PAD - NOPE_DIM), (0, 0))).astype(BF16)
    state_t = jnp.transpose(state_pool[0], (1, 0, 2))
    qabs, qrope, ckv_s, kr_s, sg_s, mp_s, u_s = _inproj_s(x_sample[:, 0, :], mod_s, wts, tabs_s, state_t, wukt)
    qrope = qrope.reshape(n, N_HEADS, HEAD_PAD)[:, :, NOPE_DIM:QK_DIM]
    ag_p, ag_s = _attention(q, k, vt, sg_p, page_table, qabs.reshape(n, N_HEADS, KV_LORA), qrope,
                            ckv_s[:, None, :], kr_s[:, None, :], wv, sg_s[:, None, :],
                            cache_ckv, jnp.swapaxes(cache_krope, 2, 3))
    y_prompt = _combine(x_prompt, mod_p[:, 2:3, :], ag_p, mp_p, wo, fnw, tm=2 * ROW_TILE)
    y_sample = _combine(x_sample.reshape(1, n, d), mod_s[None, :, 2 * d:], ag_s.reshape(1, n, D_ATT),
                        mp_s[None], wo, fnw, tm=n).reshape(n, 1, d)

    new_pool_p = utail[:, HALO - POOL_PREV:, :]
    new_pool_s = jnp.transpose(jnp.concatenate([state_t[1:], u_s[None]], axis=0), (1, 0, 2))
    return (y_prompt, y_sample,
            ckv_p[None], jnp.swapaxes(krt_p, 1, 2)[None], new_pool_p[None],
            ckv_s[:, None, :][None], kr_s[:, None, :][None], new_pool_s[None])
```

```python
import functools
import math

import jax
import jax.numpy as jnp
from jax import lax
from jax.experimental import pallas as pl
from jax.experimental.pallas import tpu as pltpu

F32 = jnp.float32
BF16 = jnp.bfloat16

N_HEADS = 8
NOPE_DIM = 64
ROPE_DIM = 32
QK_DIM = NOPE_DIM + ROPE_DIM
V_DIM = 64
Q_LORA = 256
KV_LORA = 128
D_POOL = 512
D_ATT = 512
POOL_WINDOWS = (2, 4, 8, 16)
POOL_GROUP = 128
POOL_PREV = 15
PAGE_SIZE = 128
ROPE_BASE = 10000.0
EPS = 1e-6
SM_SCALE = QK_DIM ** -0.5
LOG2E = math.log2(math.e)
QK_PRESCALE = SM_SCALE * LOG2E
_NT = (((1,), (1,)), ((), ()))

HEAD_PAD = 128
SUM_ROWS = 16
HALO = 16
C_Q, C_KV, C_KR, C_GA, C_U, C_GP, C_END = 0, 256, 384, 512, 1024, 1536, 2048

ROW_TILE = 512
ROW_PARTS = 2
COL_SPLIT = 1
FIRST_CHUNKS = 2
DEC_CHUNKS = 4
LOOP_BLOCKS = 6
LAST_DEAL = ((2, 2), (2, 2, 1, 1), (2, 2, 0, 0, 1, 1), (2, 2, 0, 0, 0, 0, 1, 1),
             (2, 2, 0, 0, 0, 0, 0, 0, 1, 1), (2, 2, 0, 0, 0, 0, 0, 0, 0, 0, 1, 1))

VMEM_LIMIT = 56 * 1024 * 1024


def _rms(x, w):
    return (x * lax.rsqrt(jnp.mean(x * x, axis=-1, keepdims=True) + EPS)) * w


def _silu(x):
    return x * jax.nn.sigmoid(x)


def _bdot(a, b):
    return jnp.dot(a.astype(BF16), b, preferred_element_type=F32)


def _ada_kernel(c_ref, w_ref, b_ref, o_ref):
    o_ref[...] = _bdot(_silu(c_ref[...]), w_ref[...].astype(BF16)) + b_ref[...]


def _ada(c_all, ada_w, ada_b):
    rows, d = c_all.shape
    n = ada_w.shape[1]
    bn = 1024
    return pl.pallas_call(
        _ada_kernel,
        grid=(n // bn,),
        in_specs=[pl.BlockSpec((rows, d), lambda j: (0, 0)),
                  pl.BlockSpec((d, bn), lambda j: (0, j)),
                  pl.BlockSpec((1, bn), lambda j: (0, j))],
        out_specs=pl.BlockSpec((rows, bn), lambda j: (0, j)),
        out_shape=jax.ShapeDtypeStruct((rows, n), F32),
        compiler_params=pltpu.CompilerParams(dimension_semantics=("arbitrary",)),
        name="ada",
    )(c_all, ada_w, ada_b)


def _project_z(x, shift, scale, nw, win):
    h = _rms(x, nw) * (1.0 + scale) + shift
    return lax.dot_general(h.astype(BF16), win, _NT, preferred_element_type=F32)


def _project_rest(z, qnw, kvnw, wq, wqr, cr, sr, cb, sb):
    cosk = cb * cr - sb * sr
    sink = sb * cr + cb * sr
    nope = (lax.broadcasted_iota(jnp.int32, (1, HEAD_PAD), 1) < NOPE_DIM).astype(F32)
    cosq = pltpu.roll(cosk, NOPE_DIM, axis=1) + nope
    sinq = pltpu.roll(sink, NOPE_DIM, axis=1)
    q_lat = _rms(z[:, C_Q:C_KV], qnw)
    ckv = _rms(z[:, C_KV:C_KR], kvnw)
    krc = z[:, C_KR:C_GA]
    kr128 = krc * cosk + pltpu.roll(krc, HEAD_PAD - ROPE_DIM, axis=1) * sink
    ql = q_lat.astype(BF16)
    qa = jnp.dot(ql, wq, preferred_element_type=F32)
    qb = jnp.dot(ql, wqr, preferred_element_type=F32)
    return qa, qb, ckv, kr128, z[:, C_GA:C_U], z[:, C_U:C_GP], z[:, C_GP:C_END], cosq, sinq


def _pool_out(pooled_g, g, wpool_ref, bpool_ref, pscale_ref, g_pool):
    lo, hi = g * POOL_GROUP, (g + 1) * POOL_GROUP
    y = _bdot(pooled_g, wpool_ref[g]) + bpool_ref[:, lo:hi]
    y = y * pscale_ref[:, lo:hi]
    return y * _silu(g_pool[:, lo:hi])


def _inproj_p_kernel(x_ref, mod_ref, nw_ref, win_ref, qnw_ref, kvnw_ref, wq_ref, wqr_ref,
                     wpool_ref, bpool_ref, pscale_ref, wk_ref, wvt_ref, cr_ref, sr_ref, cb_ref, sb_ref,
                     q_ref, k_ref, vt_ref, ckv_ref, krt_ref, sg_ref, mp_ref, utail_ref, uext_ref, *, tm):
    i = pl.program_id(1)
    shift = mod_ref[0, 0:1, :]
    scale = mod_ref[0, 1:2, :]

    @pl.when(i == 0)
    def _():
        uext_ref[0:HALO, :] = jnp.zeros((HALO, D_POOL), F32)

    hr = tm // ROW_PARTS
    zs = []
    for part in range(ROW_PARTS):
        rows = slice(part * hr, (part + 1) * hr)
        z = _project_z(x_ref[0, rows, :], shift, scale, nw_ref[...], win_ref[...])
        uext_ref[HALO + part * hr:HALO + (part + 1) * hr, :] = z[:, C_U:C_GP]
        zs.append(z)

    for part in range(ROW_PARTS):
        r0 = part * hr
        rows = slice(r0, r0 + hr)
        qa, qb, ckv, kr128, g_att, u, g_pool, cosq, sinq = _project_rest(
            zs[part], qnw_ref[...], kvnw_ref[...], wq_ref[...], wqr_ref[...],
            cr_ref[rows, :], sr_ref[rows, :], cb_ref[0], sb_ref[0])
        cosq, sinq = cosq * QK_PRESCALE, sinq * QK_PRESCALE
        for h in range(N_HEADS):
            sl = slice(h * HEAD_PAD, (h + 1) * HEAD_PAD)
            q_ref[0, rows, sl] = (qa[:, sl] * cosq + qb[:, sl] * sinq).astype(BF16)
        ckv_b = ckv.astype(BF16)
        k = jnp.dot(jnp.concatenate([ckv_b, kr128.astype(BF16)], axis=1), wk_ref[...],
                    preferred_element_type=F32)
        k_ref[0, rows, :] = k.astype(BF16)
        vt_ref[0, :, rows] = lax.dot_general(wvt_ref[...], ckv_b, _NT, preferred_element_type=F32).astype(BF16)
        ckv_ref[0, rows, :] = ckv
        krt_ref[0, :, rows] = kr128.T[:ROPE_DIM, :]
        sg_ref[0, rows, :] = _silu(g_att)
        pos = (i * tm + r0 + lax.broadcasted_iota(jnp.int32, (hr, 1), 0)).astype(F32)
        for g, w in enumerate(POOL_WINDOWS):
            lo, hi = g * POOL_GROUP, (g + 1) * POOL_GROUP
            wsum = u[:, lo:hi]
            for d in range(1, w):
                wsum = wsum + uext_ref[HALO + r0 - d:HALO + r0 - d + hr, lo:hi]
            cnt = jnp.minimum(pos + 1.0, float(w))
            pooled = wsum / cnt - u[:, lo:hi]
            mp_ref[0, rows, lo:hi] = _pool_out(pooled, g, wpool_ref, bpool_ref, pscale_ref, g_pool).astype(BF16)

    u_last = zs[-1][hr - HALO:hr, C_U:C_GP]
    uext_ref[0:HALO, :] = u_last

    @pl.when(i == pl.num_programs(1) - 1)
    def _():
        utail_ref[0] = u_last


def _full(shape):
    nd = len(shape)
    return pl.BlockSpec(shape, lambda *_: (0,) * nd)


def _inproj_p(x, mod3, wts, wk, wvt, tabs, tm=512):
    b, s, d = x.shape
    nt = s // tm
    row = lambda w: pl.BlockSpec((1, tm, w), lambda bi, i: (bi, i, 0))
    rtab = _full((tm, HEAD_PAD))
    btab = pl.BlockSpec((1, 1, HEAD_PAD), lambda bi, i: (i, 0, 0))
    in_specs = [row(d), pl.BlockSpec((1, 3, d), lambda bi, i: (bi, 0, 0))]
    in_specs += [_full(w.shape) for w in (*wts, wk, wvt)]
    in_specs += [rtab, rtab, btab, btab]
    out_shape = [
        jax.ShapeDtypeStruct((b, s, N_HEADS * HEAD_PAD), BF16),
        jax.ShapeDtypeStruct((b, s, N_HEADS * HEAD_PAD), BF16),
        jax.ShapeDtypeStruct((b, D_ATT, s), BF16),
        jax.ShapeDtypeStruct((b, s, KV_LORA), F32),
        jax.ShapeDtypeStruct((b, ROPE_DIM, s), F32),
        jax.ShapeDtypeStruct((b, s, D_ATT), F32),
        jax.ShapeDtypeStruct((b, s, D_POOL), BF16),
        jax.ShapeDtypeStruct((b, HALO, D_POOL), F32),
    ]
    out_specs = [row(N_HEADS * HEAD_PAD), row(N_HEADS * HEAD_PAD),
                 pl.BlockSpec((1, D_ATT, tm), lambda bi, i: (bi, 0, i)), row(KV_LORA),
                 pl.BlockSpec((1, ROPE_DIM, tm), lambda bi, i: (bi, 0, i)),
                 row(D_ATT), row(D_POOL), pl.BlockSpec((1, HALO, D_POOL), lambda bi, i: (bi, 0, 0))]
    return pl.pallas_call(
        functools.partial(_inproj_p_kernel, tm=tm),
        grid=(b, nt),
        in_specs=in_specs,
        out_specs=out_specs,
        out_shape=out_shape,
        scratch_shapes=[pltpu.VMEM((tm + HALO, D_POOL), F32)],
        compiler_params=pltpu.CompilerParams(dimension_semantics=("arbitrary", "arbitrary"),
                                             vmem_limit_bytes=VMEM_LIMIT),
        name="inproj_p",
    )(x, mod3, *wts, wk, wvt, *tabs)


def _inproj_s_kernel(x_ref, mod_ref, nw_ref, win_ref, qnw_ref, kvnw_ref, wq_ref, wqr_ref,
                     wpool_ref, bpool_ref, pscale_ref, cr_ref, sr_ref, cb_ref, sb_ref,
                     state_ref, wukt_ref,
                     qabs_ref, qrope_ref, ckv_ref, kr_ref, sg_ref, mp_ref, u_ref):
    d = x_ref.shape[1]
    shift = mod_ref[:, 0:d]
    scale = mod_ref[:, d:2 * d]
    z = _project_z(x_ref[...], shift, scale, nw_ref[...], win_ref[...])
    qa, qb, ckv, kr128, g_att, u, g_pool, cosq, sinq = _project_rest(
        z, qnw_ref[...], kvnw_ref[...], wq_ref[...], wqr_ref[...], cr_ref[...], sr_ref[...], cb_ref[0], sb_ref[0])
    for h in range(N_HEADS):
        sl = slice(h * HEAD_PAD, (h + 1) * HEAD_PAD)
        qrope_ref[:, sl] = qa[:, sl] * cosq + qb[:, sl] * sinq
        qabs_ref[:, sl] = _bdot(qa[:, sl], wukt_ref[h])
    ckv_ref[...] = ckv
    kr_ref[...] = kr128[:, :ROPE_DIM]
    sg_ref[...] = _silu(g_att)
    u_ref[...] = u
    for g, w in enumerate(POOL_WINDOWS):
        lo, hi = g * POOL_GROUP, (g + 1) * POOL_GROUP
        wsum = u[:, lo:hi]
        for j in range(1, w):
            wsum = wsum + state_ref[POOL_PREV - j, :, lo:hi]
        pooled = wsum / float(w) - u[:, lo:hi]
        mp_ref[:, lo:hi] = _pool_out(pooled, g, wpool_ref, bpool_ref, pscale_ref, g_pool).astype(BF16)


def _inproj_s(x, mod, wts, tabs, state, wukt):
    n, d = x.shape
    args = (x, mod, *wts, *tabs, state, wukt)
    out_shape = [
        jax.ShapeDtypeStruct((n, N_HEADS * HEAD_PAD), F32),
        jax.ShapeDtypeStruct((n, N_HEADS * HEAD_PAD), F32),
        jax.ShapeDtypeStruct((n, KV_LORA), F32),
        jax.ShapeDtypeStruct((n, ROPE_DIM), F32),
        jax.ShapeDtypeStruct((n, D_ATT), F32),
        jax.ShapeDtypeStruct((n, D_POOL), BF16),
        jax.ShapeDtypeStruct((n, D_POOL), F32),
    ]
    return pl.pallas_call(
        _inproj_s_kernel,
        in_specs=[_full(a.shape) for a in args],
        out_specs=[_full(o.shape) for o in out_shape],
        out_shape=out_shape,
        grid=(1,),
        compiler_params=pltpu.CompilerParams(dimension_semantics=("arbitrary",), vmem_limit_bytes=VMEM_LIMIT),
        name="inproj_s",
    )(*args)


def _flash_tile(qi, q_ref, k_ref, vt_ref, sg_ref, o_ref, st_ref, between, side, *, tq, tk):
    assert tq == tk
    cw = tq // COL_SPLIT
    units = [(hh, cs) for hh in range(2) for cs in range(COL_SPLIT)]
    qs = [q_ref[0, cs * cw:(cs + 1) * cw, hh * HEAD_PAD:(hh + 1) * HEAD_PAD] for hh, cs in units]

    def produce(j, slot, u, rows=None):
        hh, cs = units[u]
        r0, r1 = (0, tk) if rows is None else rows
        kb = k_ref[0, pl.ds(pl.multiple_of(j * tk, tk) + r0, r1 - r0), hh * HEAD_PAD:(hh + 1) * HEAD_PAD]
        st_ref[slot][hh][r0:r1, cs * cw:(cs + 1) * cw] = lax.dot_general(kb, qs[u], _NT,
                                                                         preferred_element_type=F32)

    def consume(j, slot, u, state, diagonal=False):
        hh, cs = units[u]
        m, acc = state
        st = st_ref[slot][hh][:, cs * cw:(cs + 1) * cw]
        if diagonal:
            kpos = lax.broadcasted_iota(jnp.int32, (tk, cw), 0)
            qpos = cs * cw + lax.broadcasted_iota(jnp.int32, (tk, cw), 1)
            st = jnp.where(kpos <= qpos, st, -jnp.inf)
        vtb = vt_ref[0, hh * V_DIM:(hh + 1) * V_DIM, pl.ds(pl.multiple_of(j * tk, tk), tk)]
        vtb = jnp.concatenate([vtb, jnp.ones((SUM_ROWS, tk), BF16)], axis=0)
        m_new = jnp.maximum(m, jnp.max(st, axis=0, keepdims=True))
        alpha = jnp.exp2(m - m_new)
        pt = jnp.exp2(st - m_new).astype(BF16)
        return m_new, alpha * acc + jnp.dot(vtb, pt, preferred_element_type=F32)

    def finish(states):
        heads = []
        for hh in range(2):
            acc = jnp.concatenate([states[u][1] for u in range(len(units)) if units[u][0] == hh], axis=1)
            heads.append(acc[:V_DIM] / acc[V_DIM:V_DIM + 1])
        o_ref[0] = (jnp.concatenate(heads, axis=0).T * sg_ref[0]).astype(BF16)

    def trip(t, states):
        j = LOOP_BLOCKS * t
        cur = list(states)
        for i in range(LOOP_BLOCKS):
            for u in range(len(units)):
                produce(j + i + 1, (i + 1) % 2, u)
                cur[u] = consume(j + i, i % 2, u, cur[u])
        return tuple(cur)

    init = tuple((jnp.full((1, cw), -jnp.inf, F32), jnp.zeros((V_DIM + SUM_ROWS, cw), F32)) for _ in units)
    pieces = FIRST_CHUNKS // len(units)
    for u in range(len(units)):
        for c in range(pieces):
            produce(0, 0, u, (c * tk // pieces, (c + 1) * tk // pieces))
            between(u * pieces + c)
    states = lax.fori_loop(0, qi // LOOP_BLOCKS, trip, init)

    def run_interleaved(steps, side_pieces, counts):
        side_pieces = list(side_pieces)
        for step, count in zip(steps, counts):
            for piece in side_pieces[:count]:
                piece()
            del side_pieces[:count]
            step()
        for piece in side_pieces:
            piece()

    for rest in range(LOOP_BLOCKS):
        @pl.when(qi % LOOP_BLOCKS == rest)
        def _():
            wait, side_pieces = side(rest % 2)
            wait()
            j0 = qi - rest
            cur = list(states)
            steps = []
            for i in range(rest):
                for u in range(len(units)):
                    def run(i=i, u=u):
                        produce(j0 + i + 1, (i + 1) % 2, u)
                        cur[u] = consume(j0 + i, i % 2, u, cur[u])
                    steps.append(run)
            for u in range(len(units)):
                def run(u=u):
                    cur[u] = consume(qi, rest % 2, u, cur[u], diagonal=True)
                steps.append(run)
            run_interleaved(steps, side_pieces, LAST_DEAL[rest])
            finish(cur)


def _page_copies(pt_ref, cckv_hbm, ckrt_hbm, ckv_buf, krt_buf, sem, n_pages, row, sl, p):
    page = pt_ref[row * n_pages + p]
    cols = pl.ds(p * PAGE_SIZE if isinstance(p, int) else pl.multiple_of(p * PAGE_SIZE, PAGE_SIZE), PAGE_SIZE)
    return (pltpu.make_async_copy(cckv_hbm.at[0, page], ckv_buf.at[sl, cols], sem.at[0, sl]),
            pltpu.make_async_copy(ckrt_hbm.at[0, page], krt_buf.at[sl, :, cols], sem.at[1, sl]))


def _decode_stages(pt_ref, qa_ref, qr_ref, ckvn_ref, krn_ref, wv_ref, sg_ref, cckv_hbm, ckrt_hbm,
                   o_ref, ckv_buf, krt_buf, sem, *, row, slot, n_pages):
    def wait():
        def wait_body(p, _):
            for cp in _page_copies(pt_ref, cckv_hbm, ckrt_hbm, ckv_buf, krt_buf, sem, n_pages, row, slot, p):
                cp.wait()
            return 0
        lax.fori_loop(0, n_pages, wait_body, 0, unroll=8)

    past = ckv_buf.shape[1]
    ck = past // DEC_CHUNKS
    env = {"ckv": [], "s": []}

    def scores(c):
        def run():
            if c == 0:
                env["qa"] = qa_ref[0].astype(BF16)
                env["qr"] = qr_ref[0].astype(BF16)
            ckv = ckv_buf[slot, c * ck:(c + 1) * ck, :].astype(BF16)
            krt = krt_buf[slot, :, c * ck:(c + 1) * ck].astype(BF16)
            env["ckv"].append(ckv)
            env["s"].append(lax.dot_general(env["qa"], ckv, _NT, preferred_element_type=F32)
                            + jnp.dot(env["qr"], krt, preferred_element_type=F32))
        return run

    def values(c):
        def run():
            if c == 0:
                qa, qr = env["qa"], env["qr"]
                s = jnp.concatenate(env["s"], axis=1)
                ckvn = ckvn_ref[0].astype(BF16).astype(F32)
                krn = krn_ref[0].astype(BF16).astype(F32)
                s_n = (jnp.sum(qa.astype(F32) * ckvn, axis=1, keepdims=True)
                       + jnp.sum(qr.astype(F32) * krn, axis=1, keepdims=True))
                m = jnp.maximum(jnp.max(s, axis=1, keepdims=True), s_n)
                p = jnp.exp2((s - m) * QK_PRESCALE)
                p_n = jnp.exp2((s_n - m) * QK_PRESCALE)
                env["l"] = jnp.sum(p, axis=1, keepdims=True) + p_n
                env["p"] = p.astype(BF16)
                env["o"] = p_n.astype(BF16).astype(F32) * ckvn
            env["o"] = env["o"] + jnp.dot(env["p"][:, c * ck:(c + 1) * ck], env["ckv"][c],
                                          preferred_element_type=F32)
            if c == DEC_CHUNKS - 1:
                o_all = _bdot(env["o"] / env["l"], wv_ref[...])
                hrow = lax.broadcasted_iota(jnp.int32, o_all.shape, 0)
                hcol = lax.broadcasted_iota(jnp.int32, o_all.shape, 1) // V_DIM
                att = jnp.sum(jnp.where(hrow == hcol, o_all, 0.0), axis=0, keepdims=True)
                o_ref[0] = (att * sg_ref[0]).astype(BF16)
        return run

    return wait, [scores(c) for c in range(DEC_CHUNKS)] + [values(c) for c in range(DEC_CHUNKS)]


def _attn_kernel(pt_ref, q_ref, k_ref, vt_ref, sgp_ref, qa_ref, qr_ref, ckvn_ref, krn_ref, wv_ref, sgs_ref,
                 cckv_hbm, ckrt_hbm, op_ref, os_ref, st00, st01, st10, st11, ckv_buf, krt_buf, sem,
                 *, tq, tk, n_pages):
    qi = pl.program_id(2)
    r = (pl.program_id(0) * pl.num_programs(1) + pl.program_id(1)) * pl.num_programs(2) + qi
    n_rows = pl.num_programs(0) * pl.num_programs(1) * pl.num_programs(2)
    slot = r % 2
    dma = (pt_ref, cckv_hbm, ckrt_hbm, ckv_buf, krt_buf, sem, n_pages)

    @pl.when(r == 0)
    def _():
        def body(p, _):
            for cp in _page_copies(*dma, 0, 0, p):
                cp.start()
            return 0
        lax.fori_loop(0, n_pages, body, 0, unroll=8)

    nxt = jnp.minimum(r + 1, n_rows - 1)
    assert n_pages % FIRST_CHUNKS == 0
    per_chunk = n_pages // FIRST_CHUNKS

    def issue(c):
        for p in range(c * per_chunk, (c + 1) * per_chunk):
            for cp in _page_copies(*dma, nxt, 1 - slot, p):
                cp.start()

    def row_stages(parity):
        return _decode_stages(pt_ref, qa_ref, qr_ref, ckvn_ref, krn_ref, wv_ref, sgs_ref, cckv_hbm, ckrt_hbm,
                              os_ref, ckv_buf, krt_buf, sem, row=r, slot=parity, n_pages=n_pages)

    _flash_tile(qi, q_ref, k_ref, vt_ref, sgp_ref, op_ref, ((st00, st01), (st10, st11)), issue, row_stages,
                tq=tq, tk=tk)

    @pl.when(r == n_rows - 1)
    def _():
        def body(p, _):
            for cp in _page_copies(*dma, nxt, 1 - slot, p):
                cp.wait()
            return 0
        lax.fori_loop(0, n_pages, body, 0, unroll=8)


def _attention(q, k, vt, sg_p, page_table, qabs, qrope, ckv_new, kr_new, wv, sg_s, cache_ckv, cache_krt,
               tq=512, tk=512):
    b, s, _ = q.shape
    n_pairs = N_HEADS // 2
    nq = s // tq
    n, n_pages = page_table.shape
    assert n == b * n_pairs * nq, "one sample row per prompt attention step"
    assert nq % 2 == 0, "row parity must equal q-tile parity"
    past = n_pages * PAGE_SIZE
    row = lambda bi, hp, qi: (bi * n_pairs + hp) * nq + qi
    per_r = lambda w: pl.BlockSpec((1, 1, w), lambda bi, hp, qi, pt: (row(bi, hp, qi), 0, 0))
    per_h = lambda w: pl.BlockSpec((1, N_HEADS, w), lambda bi, hp, qi, pt: (row(bi, hp, qi), 0, 0))
    grid_spec = pltpu.PrefetchScalarGridSpec(
        num_scalar_prefetch=1,
        grid=(b, n_pairs, nq),
        in_specs=[pl.BlockSpec((1, tq, 2 * HEAD_PAD), lambda bi, hp, qi, pt: (bi, qi, hp)),
                  pl.BlockSpec((1, s, 2 * HEAD_PAD), lambda bi, hp, qi, pt: (bi, 0, hp)),
                  pl.BlockSpec((1, 2 * V_DIM, s), lambda bi, hp, qi, pt: (bi, hp, 0)),
                  pl.BlockSpec((1, tq, 2 * V_DIM), lambda bi, hp, qi, pt: (bi, qi, hp)),
                  per_h(KV_LORA), per_h(ROPE_DIM), per_r(KV_LORA), per_r(ROPE_DIM),
                  pl.BlockSpec(wv.shape, lambda bi, hp, qi, pt: (0, 0)),
                  per_r(D_ATT),
                  pl.BlockSpec(memory_space=pl.ANY),
                  pl.BlockSpec(memory_space=pl.ANY)],
        out_specs=[pl.BlockSpec((1, tq, 2 * V_DIM), lambda bi, hp, qi, pt: (bi, qi, hp)), per_r(D_ATT)],
        scratch_shapes=[pltpu.VMEM((tk, tq), F32)] * 4 + [
            pltpu.VMEM((2, past, KV_LORA), F32),
            pltpu.VMEM((2, ROPE_DIM, past), F32),
            pltpu.SemaphoreType.DMA((2, 2))],
    )
    return pl.pallas_call(
        functools.partial(_attn_kernel, tq=tq, tk=tk, n_pages=n_pages),
        grid_spec=grid_spec,
        out_shape=[jax.ShapeDtypeStruct((b, s, D_ATT), BF16), jax.ShapeDtypeStruct((n, 1, D_ATT), BF16)],
        compiler_params=pltpu.CompilerParams(dimension_semantics=("arbitrary", "arbitrary", "arbitrary"),
                                             vmem_limit_bytes=VMEM_LIMIT),
        name="attention",
    )(page_table.reshape(-1), q, k, vt, sg_p, qabs, qrope, ckv_new, kr_new, wv, sg_s, cache_ckv, cache_krt)


def _combine_kernel(x_ref, gate_ref, ag_ref, mp_ref, wo_ref, fnw_ref, y_ref):
    tm = x_ref.shape[1]
    hr = tm // ROW_PARTS
    parts = [slice(p * hr, (p + 1) * hr) for p in range(ROW_PARTS)]
    projs = [jnp.dot(ag_ref[0, rows, :], wo_ref[0:D_ATT, :], preferred_element_type=F32)
             + jnp.dot(mp_ref[0, rows, :], wo_ref[D_ATT:D_ATT + D_POOL, :], preferred_element_type=F32)
             for rows in parts]
    for rows, proj in zip(parts, projs):
        gate = gate_ref[0] if gate_ref.shape[1] == 1 else gate_ref[0, rows, :]
        y_ref[0, rows, :] = _rms(x_ref[0, rows, :] + gate * proj, fnw_ref[...])


def _combine(x, gate, ag, mp, wo, fnw, tm):
    b, s, d = x.shape
    gr = gate.shape[1]
    gate_spec = (pl.BlockSpec((1, 1, d), lambda bi, i: (bi, 0, 0)) if gr == 1
                 else pl.BlockSpec((1, tm, d), lambda bi, i: (bi, i, 0)))
    row = lambda w: pl.BlockSpec((1, tm, w), lambda bi, i: (bi, i, 0))
    return pl.pallas_call(
        _combine_kernel,
        grid=(b, s // tm),
        in_specs=[row(d), gate_spec, row(D_ATT), row(D_POOL), _full(wo.shape), _full(fnw.shape)],
        out_specs=row(d),
        out_shape=jax.ShapeDtypeStruct((b, s, d), F32),
        compiler_params=pltpu.CompilerParams(dimension_semantics=("arbitrary", "arbitrary"),
                                             vmem_limit_bytes=VMEM_LIMIT),
        name="combine",
    )(x, gate, ag, mp, wo, fnw)


def _rope_tables(offsets, bases):
    inv = ROPE_BASE ** (-jnp.arange(0, ROPE_DIM, 2, dtype=F32) / ROPE_DIM)
    inv = jnp.pad(jnp.concatenate([inv, inv]), (0, HEAD_PAD - ROPE_DIM))
    lanes = (jnp.arange(HEAD_PAD) < ROPE_DIM).astype(F32)
    a_off = offsets.astype(F32)[:, None] * inv[None, :]
    a_base = bases.astype(F32)[:, None, None] * inv[None, None, :]
    return jnp.cos(a_off) * lanes, jnp.sin(a_off) * lanes, jnp.cos(a_base), jnp.sin(a_base)


def _rot_half(w):
    half = w.shape[-1] // 2
    return jnp.concatenate([-w[..., half:], w[..., :half]], axis=-1)


def _pack_weights(norm_w, w_in, q_norm_w, w_uq, kv_norm_w, w_uk, w_uv, w_pool, b_pool, pool_scale):
    d = w_in.shape[0]
    off_kr = Q_LORA + KV_LORA
    wt = jnp.swapaxes(w_in, 0, 1).astype(BF16)
    krc = wt[off_kr:off_kr + ROPE_DIM]
    win = jnp.concatenate([wt[:off_kr], krc, -krc[ROPE_DIM // 2:], krc[:ROPE_DIM // 2],
                           jnp.zeros((HEAD_PAD - 2 * ROPE_DIM, d), BF16), wt[off_kr + ROPE_DIM:]], axis=0)
    pad_q = HEAD_PAD - QK_DIM
    wq = jnp.pad(w_uq, ((0, 0), (0, 0), (0, pad_q))).reshape(Q_LORA, N_HEADS * HEAD_PAD)
    wqr = jnp.pad(_rot_half(w_uq[..., NOPE_DIM:]), ((0, 0), (0, 0), (NOPE_DIM, pad_q)))
    wqr = wqr.reshape(Q_LORA, N_HEADS * HEAD_PAD)
    wk_nope = jnp.pad(w_uk, ((0, 0), (0, 0), (0, HEAD_PAD - NOPE_DIM))).reshape(KV_LORA, N_HEADS * HEAD_PAD)
    place = jnp.pad(jnp.eye(ROPE_DIM, dtype=F32), ((0, HEAD_PAD - ROPE_DIM), (NOPE_DIM, pad_q)))
    wk = jnp.concatenate([wk_nope, jnp.tile(place, (1, N_HEADS))], axis=0)
    wv = w_uv.reshape(KV_LORA, N_HEADS * V_DIM).astype(BF16)
    shared = (norm_w.reshape(1, -1), win, q_norm_w.reshape(1, -1), kv_norm_w.reshape(1, -1),
              wq.astype(BF16), wqr.astype(BF16), w_pool.astype(BF16), b_pool.reshape(1, -1),
              pool_scale.reshape(1, -1))
    return shared, wk.astype(BF16), wv


def kernel(x_prompt, x_sample, c_prompt, c_sample, cache_ckv, cache_krope, state_pool, page_table, ada_w, ada_b, norm_w, w_in, q_norm_w, w_uq, kv_norm_w, w_uk, w_uv, w_pool, b_pool, pool_scale, w_out, final_norm_w):
    assert ada_w.shape[0] == 1, "single-layer trunk only"
    b, s, d = x_prompt.shape
    n = x_sample.shape[0]
    assert x_sample.shape[1] == 1
    past_len = page_table.shape[1] * PAGE_SIZE

    c_all = jnp.concatenate([c_prompt, c_sample], axis=0)
    rows = -(-c_all.shape[0] // 8) * 8
    c_all = jnp.pad(c_all, ((0, rows - c_all.shape[0]), (0, 0)))
    mod = _ada(c_all, ada_w[0], ada_b[0].reshape(1, -1))
    mod_p = mod[:b].reshape(b, 3, d)
    mod_s = mod[b:b + n]

    wts, wk, wv = _pack_weights(norm_w[0], w_in[0], q_norm_w[0], w_uq[0], kv_norm_w[0], w_uk[0], w_uv[0],
                                w_pool[0], b_pool[0], pool_scale[0])
    wo = w_out[0].astype(BF16)
    fnw = final_norm_w.reshape(1, -1)

    tabs_p = _rope_tables(jnp.arange(ROW_TILE), jnp.arange(0, s, ROW_TILE))
    q, k, vt, ckv_p, krt_p, sg_p, mp_p, utail = _inproj_p(x_prompt, mod_p, wts, wk, wv.T, tabs_p, tm=ROW_TILE)

    tabs_s = _rope_tables(jnp.zeros((n,)), jnp.full((1,), past_len))
    wukt = jnp.pad(jnp.transpose(w_uk[0], (1, 2, 0)), ((0, 0), (0, HEAD_PAD - NOPE_DIM), (0, 0))).astype(BF16)
    state_t = jnp.transpose(state_pool[0], (1, 0, 2))
    qabs, qrope, ckv_s, kr_s, sg_s, mp_s, u_s = _inproj_s(x_sample[:, 0, :], mod_s, wts, tabs_s, state_t, wukt)
    qrope = qrope.reshape(n, N_HEADS, HEAD_PAD)[:, :, NOPE_DIM:QK_DIM]
    ag_p, ag_s = _attention(q, k, vt, sg_p, page_table, qabs.reshape(n, N_HEADS, KV_LORA), qrope,
                            ckv_s[:, None, :], kr_s[:, None, :], wv, sg_s[:, None, :],
                            cache_ckv, jnp.swapaxes(cache_krope, 2, 3))
    y_prompt = _combine(x_prompt, mod_p[:, 2:3, :], ag_p, mp_p, wo, fnw, tm=2 * ROW_TILE)
    y_sample = _combine(x_sample.reshape(1, n, d), mod_s[None, :, 2 * d:], ag_s.reshape(1, n, D_ATT),
                        mp_s[None], wo, fnw, tm=n).reshape(n, 1, d)

    new_pool_p = utail[:, HALO - POOL_PREV:, :]
    new_pool_s = jnp.transpose(jnp.concatenate([state_t[1:], u_s[None]], axis=0), (1, 0, 2))
    return (y_prompt, y_sample,
            ckv_p[None], jnp.swapaxes(krt_p, 1, 2)[None], new_pool_p[None],
            ckv_s[:, None, :][None], kr_s[:, None, :][None], new_pool_s[None])
```

```python
import functools
import math

import jax
import jax.numpy as jnp
from jax import lax
from jax.experimental import pallas as pl
from jax.experimental.pallas import tpu as pltpu

F32 = jnp.float32
BF16 = jnp.bfloat16

N_HEADS = 8
NOPE_DIM = 64
ROPE_DIM = 32
QK_DIM = NOPE_DIM + ROPE_DIM
V_DIM = 64
Q_LORA = 256
KV_LORA = 128
D_POOL = 512
D_ATT = 512
POOL_WINDOWS = (2, 4, 8, 16)
POOL_GROUP = 128
POOL_PREV = 15
PAGE_SIZE = 128
ROPE_BASE = 10000.0
EPS = 1e-6
SM_SCALE = QK_DIM ** -0.5
LOG2E = math.log2(math.e)
QK_PRESCALE = SM_SCALE * LOG2E
_NT = (((1,), (1,)), ((), ()))

HEAD_PAD = 128
SUM_ROWS = 16
HALO = 16
C_Q, C_KV, C_KR, C_GA, C_U, C_GP, C_END = 0, 256, 384, 512, 1024, 1536, 2048

ROW_TILE = 512
ROW_PARTS = 2
COL_SPLIT = 1
FIRST_CHUNKS = 2
DEC_CHUNKS = 4
LOOP_BLOCKS = 4
LAST_DEAL = ((2, 2), (2, 2, 1, 1), (2, 2, 0, 0, 1, 1), (2, 2, 0, 0, 0, 0, 1, 1))

VMEM_LIMIT = 56 * 1024 * 1024


def _rms(x, w):
    return (x * lax.rsqrt(jnp.mean(x * x, axis=-1, keepdims=True) + EPS)) * w


def _silu(x):
    return x * jax.nn.sigmoid(x)


def _bdot(a, b):
    return jnp.dot(a.astype(BF16), b, preferred_element_type=F32)


def _ada_kernel(c_ref, w_ref, b_ref, o_ref):
    o_ref[...] = _bdot(_silu(c_ref[...]), w_ref[...].astype(BF16)) + b_ref[...]


def _ada(c_all, ada_w, ada_b):
    rows, d = c_all.shape
    n = ada_w.shape[1]
    bn = 1024
    return pl.pallas_call(
        _ada_kernel,
        grid=(n // bn,),
        in_specs=[pl.BlockSpec((rows, d), lambda j: (0, 0)),
                  pl.BlockSpec((d, bn), lambda j: (0, j)),
                  pl.BlockSpec((1, bn), lambda j: (0, j))],
        out_specs=pl.BlockSpec((rows, bn), lambda j: (0, j)),
        out_shape=jax.ShapeDtypeStruct((rows, n), F32),
        compiler_params=pltpu.CompilerParams(dimension_semantics=("arbitrary",)),
        name="ada",
    )(c_all, ada_w, ada_b)


def _project_z(x, shift, scale, nw, win):
    h = _rms(x, nw) * (1.0 + scale) + shift
    return lax.dot_general(h.astype(BF16), win, _NT, preferred_element_type=F32)


def _project_rest(z, qnw, kvnw, wq, wqr, cr, sr, cb, sb):
    cosk = cb * cr - sb * sr
    sink = sb * cr + cb * sr
    nope = (lax.broadcasted_iota(jnp.int32, (1, HEAD_PAD), 1) < NOPE_DIM).astype(F32)
    cosq = pltpu.roll(cosk, NOPE_DIM, axis=1) + nope
    sinq = pltpu.roll(sink, NOPE_DIM, axis=1)
    q_lat = _rms(z[:, C_Q:C_KV], qnw)
    ckv = _rms(z[:, C_KV:C_KR], kvnw)
    krc = z[:, C_KR:C_GA]
    kr128 = krc * cosk + pltpu.roll(krc, HEAD_PAD - ROPE_DIM, axis=1) * sink
    ql = q_lat.astype(BF16)
    qa = jnp.dot(ql, wq, preferred_element_type=F32)
    qb = jnp.dot(ql, wqr, preferred_element_type=F32)
    return qa, qb, ckv, kr128, z[:, C_GA:C_U], z[:, C_U:C_GP], z[:, C_GP:C_END], cosq, sinq


def _pool_out(pooled_g, g, wpool_ref, bpool_ref, pscale_ref, g_pool):
    lo, hi = g * POOL_GROUP, (g + 1) * POOL_GROUP
    y = _bdot(pooled_g, wpool_ref[g]) + bpool_ref[:, lo:hi]
    y = y * pscale_ref[:, lo:hi]
    return y * _silu(g_pool[:, lo:hi])


def _inproj_p_kernel(x_ref, mod_ref, nw_ref, win_ref, qnw_ref, kvnw_ref, wq_ref, wqr_ref,
                     wpool_ref, bpool_ref, pscale_ref, wk_ref, wvt_ref, cr_ref, sr_ref, cb_ref, sb_ref,
                     q_ref, k_ref, vt_ref, ckv_ref, krt_ref, sg_ref, mp_ref, utail_ref, uext_ref, *, tm):
    i = pl.program_id(1)
    shift = mod_ref[0, 0:1, :]
    scale = mod_ref[0, 1:2, :]

    @pl.when(i == 0)
    def _():
        uext_ref[0:HALO, :] = jnp.zeros((HALO, D_POOL), F32)

    hr = tm // ROW_PARTS
    zs = []
    for part in range(ROW_PARTS):
        rows = slice(part * hr, (part + 1) * hr)
        z = _project_z(x_ref[0, rows, :], shift, scale, nw_ref[...], win_ref[...])
        uext_ref[HALO + part * hr:HALO + (part + 1) * hr, :] = z[:, C_U:C_GP]
        zs.append(z)

    for part in range(ROW_PARTS):
        r0 = part * hr
        rows = slice(r0, r0 + hr)
        qa, qb, ckv, kr128, g_att, u, g_pool, cosq, sinq = _project_rest(
            zs[part], qnw_ref[...], kvnw_ref[...], wq_ref[...], wqr_ref[...],
            cr_ref[rows, :], sr_ref[rows, :], cb_ref[0], sb_ref[0])
        cosq, sinq = cosq * QK_PRESCALE, sinq * QK_PRESCALE
        for h in range(N_HEADS):
            sl = slice(h * HEAD_PAD, (h + 1) * HEAD_PAD)
            q_ref[0, rows, sl] = (qa[:, sl] * cosq + qb[:, sl] * sinq).astype(BF16)
        ckv_b = ckv.astype(BF16)
        k = jnp.dot(jnp.concatenate([ckv_b, kr128.astype(BF16)], axis=1), wk_ref[...],
                    preferred_element_type=F32)
        k_ref[0, rows, :] = k.astype(BF16)
        vt_ref[0, :, rows] = lax.dot_general(wvt_ref[...], ckv_b, _NT, preferred_element_type=F32).astype(BF16)
        ckv_ref[0, rows, :] = ckv
        krt_ref[0, :, rows] = kr128.T[:ROPE_DIM, :]
        sg_ref[0, rows, :] = _silu(g_att)
        pos = (i * tm + r0 + lax.broadcasted_iota(jnp.int32, (hr, 1), 0)).astype(F32)
        for g, w in enumerate(POOL_WINDOWS):
            lo, hi = g * POOL_GROUP, (g + 1) * POOL_GROUP
            wsum = u[:, lo:hi]
            for d in range(1, w):
                wsum = wsum + uext_ref[HALO + r0 - d:HALO + r0 - d + hr, lo:hi]
            cnt = jnp.minimum(pos + 1.0, float(w))
            pooled = wsum / cnt - u[:, lo:hi]
            mp_ref[0, rows, lo:hi] = _pool_out(pooled, g, wpool_ref, bpool_ref, pscale_ref, g_pool).astype(BF16)

    u_last = zs[-1][hr - HALO:hr, C_U:C_GP]
    uext_ref[0:HALO, :] = u_last

    @pl.when(i == pl.num_programs(1) - 1)
    def _():
        utail_ref[0] = u_last


def _full(shape):
    nd = len(shape)
    return pl.BlockSpec(shape, lambda *_: (0,) * nd)


def _inproj_p(x, mod3, wts, wk, wvt, tabs, tm=512):
    b, s, d = x.shape
    nt = s // tm
    row = lambda w: pl.BlockSpec((1, tm, w), lambda bi, i: (bi, i, 0))
    rtab = _full((tm, HEAD_PAD))
    btab = pl.BlockSpec((1, 1, HEAD_PAD), lambda bi, i: (i, 0, 0))
    in_specs = [row(d), pl.BlockSpec((1, 3, d), lambda bi, i: (bi, 0, 0))]
    in_specs += [_full(w.shape) for w in (*wts, wk, wvt)]
    in_specs += [rtab, rtab, btab, btab]
    out_shape = [
        jax.ShapeDtypeStruct((b, s, N_HEADS * HEAD_PAD), BF16),
        jax.ShapeDtypeStruct((b, s, N_HEADS * HEAD_PAD), BF16),
        jax.ShapeDtypeStruct((b, D_ATT, s), BF16),
        jax.ShapeDtypeStruct((b, s, KV_LORA), F32),
        jax.ShapeDtypeStruct((b, ROPE_DIM, s), F32),
        jax.ShapeDtypeStruct((b, s, D_ATT), F32),
        jax.ShapeDtypeStruct((b, s, D_POOL), BF16),
        jax.ShapeDtypeStruct((b, HALO, D_POOL), F32),
    ]
    out_specs = [row(N_HEADS * HEAD_PAD), row(N_HEADS * HEAD_PAD),
                 pl.BlockSpec((1, D_ATT, tm), lambda bi, i: (bi, 0, i)), row(KV_LORA),
                 pl.BlockSpec((1, ROPE_DIM, tm), lambda bi, i: (bi, 0, i)),
                 row(D_ATT), row(D_POOL), pl.BlockSpec((1, HALO, D_POOL), lambda bi, i: (bi, 0, 0))]
    return pl.pallas_call(
        functools.partial(_inproj_p_kernel, tm=tm),
        grid=(b, nt),
        in_specs=in_specs,
        out_specs=out_specs,
        out_shape=out_shape,
        scratch_shapes=[pltpu.VMEM((tm + HALO, D_POOL), F32)],
        compiler_params=pltpu.CompilerParams(dimension_semantics=("arbitrary", "arbitrary"),
                                             vmem_limit_bytes=VMEM_LIMIT),
        name="inproj_p",
    )(x, mod3, *wts, wk, wvt, *tabs)


def _inproj_s_kernel(x_ref, mod_ref, nw_ref, win_ref, qnw_ref, kvnw_ref, wq_ref, wqr_ref,
                     wpool_ref, bpool_ref, pscale_ref, cr_ref, sr_ref, cb_ref, sb_ref,
                     state_ref, wukt_ref,
                     qabs_ref, qrope_ref, ckv_ref, kr_ref, sg_ref, mp_ref, u_ref):
    d = x_ref.shape[1]
    shift = mod_ref[:, 0:d]
    scale = mod_ref[:, d:2 * d]
    z = _project_z(x_ref[...], shift, scale, nw_ref[...], win_ref[...])
    qa, qb, ckv, kr128, g_att, u, g_pool, cosq, sinq = _project_rest(
        z, qnw_ref[...], kvnw_ref[...], wq_ref[...], wqr_ref[...], cr_ref[...], sr_ref[...], cb_ref[0], sb_ref[0])
    for h in range(N_HEADS):
        sl = slice(h * HEAD_PAD, (h + 1) * HEAD_PAD)
        qrope_ref[:, sl] = qa[:, sl] * cosq + qb[:, sl] * sinq
        qabs_ref[:, sl] = _bdot(qa[:, sl], wukt_ref[h])
    ckv_ref[...] = ckv
    kr_ref[...] = kr128[:, :ROPE_DIM]
    sg_ref[...] = _silu(g_att)
    u_ref[...] = u
    for g, w in enumerate(POOL_WINDOWS):
        lo, hi = g * POOL_GROUP, (g + 1) * POOL_GROUP
        wsum = u[:, lo:hi]
        for j in range(1, w):
            wsum = wsum + state_ref[POOL_PREV - j, :, lo:hi]
        pooled = wsum / float(w) - u[:, lo:hi]
        mp_ref[:, lo:hi] = _pool_out(pooled, g, wpool_ref, bpool_ref, pscale_ref, g_pool).astype(BF16)


def _inproj_s(x, mod, wts, tabs, state, wukt):
    n, d = x.shape
    args = (x, mod, *wts, *tabs, state, wukt)
    out_shape = [
        jax.ShapeDtypeStruct((n, N_HEADS * HEAD_PAD), F32),
        jax.ShapeDtypeStruct((n, N_HEADS * HEAD_PAD), F32),
        jax.ShapeDtypeStruct((n, KV_LORA), F32),
        jax.ShapeDtypeStruct((n, ROPE_DIM), F32),
        jax.ShapeDtypeStruct((n, D_ATT), F32),
        jax.ShapeDtypeStruct((n, D_POOL), BF16),
        jax.ShapeDtypeStruct((n, D_POOL), F32),
    ]
    return pl.pallas_call(
        _inproj_s_kernel,
        in_specs=[_full(a.shape) for a in args],
        out_specs=[_full(o.shape) for o in out_shape],
        out_shape=out_shape,
        grid=(1,),
        compiler_params=pltpu.CompilerParams(dimension_semantics=("arbitrary",), vmem_limit_bytes=VMEM_LIMIT),
        name="inproj_s",
    )(*args)


def _flash_tile(qi, q_ref, k_ref, vt_ref, sg_ref, o_ref, st_ref, between, side, *, tq, tk):
    assert tq == tk
    cw = tq // COL_SPLIT
    units = [(hh, cs) for hh in range(2) for cs in range(COL_SPLIT)]
    qs = [q_ref[0, cs * cw:(cs + 1) * cw, hh * HEAD_PAD:(hh + 1) * HEAD_PAD] for hh, cs in units]

    def produce(j, slot, u, rows=None):
        hh, cs = units[u]
        r0, r1 = (0, tk) if rows is None else rows
        kb = k_ref[0, pl.ds(pl.multiple_of(j * tk, tk) + r0, r1 - r0), hh * HEAD_PAD:(hh + 1) * HEAD_PAD]
        st_ref[slot][hh][r0:r1, cs * cw:(cs + 1) * cw] = lax.dot_general(kb, qs[u], _NT,
                                                                         preferred_element_type=F32)

    def consume(j, slot, u, state, diagonal=False):
        hh, cs = units[u]
        m, acc = state
        st = st_ref[slot][hh][:, cs * cw:(cs + 1) * cw]
        if diagonal:
            kpos = lax.broadcasted_iota(jnp.int32, (tk, cw), 0)
            qpos = cs * cw + lax.broadcasted_iota(jnp.int32, (tk, cw), 1)
            st = jnp.where(kpos <= qpos, st, -jnp.inf)
        vtb = vt_ref[0, hh * V_DIM:(hh + 1) * V_DIM, pl.ds(pl.multiple_of(j * tk, tk), tk)]
        vtb = jnp.concatenate([vtb, jnp.ones((SUM_ROWS, tk), BF16)], axis=0)
        m_new = jnp.maximum(m, jnp.max(st, axis=0, keepdims=True))
        alpha = jnp.exp2(m - m_new)
        pt = jnp.exp2(st - m_new).astype(BF16)
        return m_new, alpha * acc + jnp.dot(vtb, pt, preferred_element_type=F32)

    def finish(states):
        heads = []
        for hh in range(2):
            acc = jnp.concatenate([states[u][1] for u in range(len(units)) if units[u][0] == hh], axis=1)
            heads.append(acc[:V_DIM] / acc[V_DIM:V_DIM + 1])
        o_ref[0] = (jnp.concatenate(heads, axis=0).T * sg_ref[0]).astype(BF16)

    def trip(t, states):
        j = LOOP_BLOCKS * t
        cur = list(states)
        for i in range(LOOP_BLOCKS):
            for u in range(len(units)):
                produce(j + i + 1, (i + 1) % 2, u)
                cur[u] = consume(j + i, i % 2, u, cur[u])
        return tuple(cur)

    init = tuple((jnp.full((1, cw), -jnp.inf, F32), jnp.zeros((V_DIM + SUM_ROWS, cw), F32)) for _ in units)
    pieces = FIRST_CHUNKS // len(units)
    for u in range(len(units)):
        for c in range(pieces):
            produce(0, 0, u, (c * tk // pieces, (c + 1) * tk // pieces))
            between(u * pieces + c)
    states = lax.fori_loop(0, qi // LOOP_BLOCKS, trip, init)

    def run_interleaved(steps, side_pieces, counts):
        side_pieces = list(side_pieces)
        for step, count in zip(steps, counts):
            for piece in side_pieces[:count]:
                piece()
            del side_pieces[:count]
            step()
        for piece in side_pieces:
            piece()

    for rest in range(LOOP_BLOCKS):
        @pl.when(qi % LOOP_BLOCKS == rest)
        def _():
            wait, side_pieces = side(rest % 2)
            wait()
            j0 = qi - rest
            cur = list(states)
            steps = []
            for i in range(rest):
                for u in range(len(units)):
                    def run(i=i, u=u):
                        produce(j0 + i + 1, (i + 1) % 2, u)
                        cur[u] = consume(j0 + i, i % 2, u, cur[u])
                    steps.append(run)
            for u in range(len(units)):
                def run(u=u):
                    cur[u] = consume(qi, rest % 2, u, cur[u], diagonal=True)
                steps.append(run)
            run_interleaved(steps, side_pieces, LAST_DEAL[rest])
            finish(cur)


def _page_copies(pt_ref, cckv_hbm, ckrt_hbm, ckv_buf, krt_buf, sem, n_pages, row, sl, p):
    page = pt_ref[row * n_pages + p]
    cols = pl.ds(p * PAGE_SIZE if isinstance(p, int) else pl.multiple_of(p * PAGE_SIZE, PAGE_SIZE), PAGE_SIZE)
    return (pltpu.make_async_copy(cckv_hbm.at[0, page], ckv_buf.at[sl, cols], sem.at[0, sl]),
            pltpu.make_async_copy(ckrt_hbm.at[0, page], krt_buf.at[sl, :, cols], sem.at[1, sl]))


def _decode_stages(pt_ref, qa_ref, qr_ref, ckvn_ref, krn_ref, wv_ref, sg_ref, cckv_hbm, ckrt_hbm,
                   o_ref, ckv_buf, krt_buf, sem, *, row, slot, n_pages):
    def wait():
        def wait_body(p, _):
            for cp in _page_copies(pt_ref, cckv_hbm, ckrt_hbm, ckv_buf, krt_buf, sem, n_pages, row, slot, p):
                cp.wait()
            return 0
        lax.fori_loop(0, n_pages, wait_body, 0, unroll=8)

    past = ckv_buf.shape[1]
    ck = past // DEC_CHUNKS
    env = {"ckv": [], "s": []}

    def scores(c):
        def run():
            if c == 0:
                env["qa"] = qa_ref[0].astype(BF16)
                env["qr"] = qr_ref[0].astype(BF16)
            ckv = ckv_buf[slot, c * ck:(c + 1) * ck, :].astype(BF16)
            krt = krt_buf[slot, :, c * ck:(c + 1) * ck].astype(BF16)
            env["ckv"].append(ckv)
            env["s"].append(lax.dot_general(env["qa"], ckv, _NT, preferred_element_type=F32)
                            + jnp.dot(env["qr"], krt, preferred_element_type=F32))
        return run

    def values(c):
        def run():
            if c == 0:
                qa, qr = env["qa"], env["qr"]
                s = jnp.concatenate(env["s"], axis=1)
                ckvn = ckvn_ref[0].astype(BF16).astype(F32)
                krn = krn_ref[0].astype(BF16).astype(F32)
                s_n = (jnp.sum(qa.astype(F32) * ckvn, axis=1, keepdims=True)
                       + jnp.sum(qr.astype(F32) * krn, axis=1, keepdims=True))
                m = jnp.maximum(jnp.max(s, axis=1, keepdims=True), s_n)
                p = jnp.exp2((s - m) * QK_PRESCALE)
                p_n = jnp.exp2((s_n - m) * QK_PRESCALE)
                env["l"] = jnp.sum(p, axis=1, keepdims=True) + p_n
                env["p"] = p.astype(BF16)
                env["o"] = p_n.astype(BF16).astype(F32) * ckvn
            env["o"] = env["o"] + jnp.dot(env["p"][:, c * ck:(c + 1) * ck], env["ckv"][c],
                                          preferred_element_type=F32)
            if c == DEC_CHUNKS - 1:
                o_all = _bdot(env["o"] / env["l"], wv_ref[...])
                hrow = lax.broadcasted_iota(jnp.int32, o_all.shape, 0)
                hcol = lax.broadcasted_iota(jnp.int32, o_all.shape, 1) // V_DIM
                att = jnp.sum(jnp.where(hrow == hcol, o_all, 0.0), axis=0, keepdims=True)
                o_ref[0] = (att * sg_ref[0]).astype(BF16)
        return run

    return wait, [scores(c) for c in range(DEC_CHUNKS)] + [values(c) for c in range(DEC_CHUNKS)]


def _tile_of_step(step, n_tiles):
    return jnp.where(step % 2 == 0, step, n_tiles - step)


def _attn_kernel(pt_ref, q_ref, k_ref, vt_ref, sgp_ref, qa_ref, qr_ref, ckvn_ref, krn_ref, wv_ref, sgs_ref,
                 cckv_hbm, ckrt_hbm, op_ref, os_ref, st00, st01, st10, st11, ckv_buf, krt_buf, sem,
                 *, tq, tk, n_pages):
    qi = _tile_of_step(pl.program_id(2), pl.num_programs(2))
    r = (pl.program_id(0) * pl.num_programs(1) + pl.program_id(1)) * pl.num_programs(2) + pl.program_id(2)
    n_rows = pl.num_programs(0) * pl.num_programs(1) * pl.num_programs(2)
    slot = r % 2
    dma = (pt_ref, cckv_hbm, ckrt_hbm, ckv_buf, krt_buf, sem, n_pages)

    @pl.when(r == 0)
    def _():
        def body(p, _):
            for cp in _page_copies(*dma, 0, 0, p):
                cp.start()
            return 0
        lax.fori_loop(0, n_pages, body, 0, unroll=8)

    nxt = jnp.minimum(r + 1, n_rows - 1)
    assert n_pages % FIRST_CHUNKS == 0
    per_chunk = n_pages // FIRST_CHUNKS

    def issue(c):
        for p in range(c * per_chunk, (c + 1) * per_chunk):
            for cp in _page_copies(*dma, nxt, 1 - slot, p):
                cp.start()

    def row_stages(parity):
        return _decode_stages(pt_ref, qa_ref, qr_ref, ckvn_ref, krn_ref, wv_ref, sgs_ref, cckv_hbm, ckrt_hbm,
                              os_ref, ckv_buf, krt_buf, sem, row=r, slot=parity, n_pages=n_pages)

    _flash_tile(qi, q_ref, k_ref, vt_ref, sgp_ref, op_ref, ((st00, st01), (st10, st11)), issue, row_stages,
                tq=tq, tk=tk)

    @pl.when(r == n_rows - 1)
    def _():
        def body(p, _):
            for cp in _page_copies(*dma, nxt, 1 - slot, p):
                cp.wait()
            return 0
        lax.fori_loop(0, n_pages, body, 0, unroll=8)


def _attention(q, k, vt, sg_p, page_table, qabs, qrope, ckv_new, kr_new, wv, sg_s, cache_ckv, cache_krt,
               tq=512, tk=512):
    b, s, _ = q.shape
    n_pairs = N_HEADS // 2
    nq = s // tq
    n, n_pages = page_table.shape
    assert n == b * n_pairs * nq, "one sample row per prompt attention step"
    assert nq % 2 == 0, "row parity must equal q-tile parity"
    past = n_pages * PAGE_SIZE
    row = lambda bi, hp, qi: (bi * n_pairs + hp) * nq + qi
    per_r = lambda w: pl.BlockSpec((1, 1, w), lambda bi, hp, qi, pt: (row(bi, hp, qi), 0, 0))
    per_h = lambda w: pl.BlockSpec((1, N_HEADS, w), lambda bi, hp, qi, pt: (row(bi, hp, qi), 0, 0))
    grid_spec = pltpu.PrefetchScalarGridSpec(
        num_scalar_prefetch=1,
        grid=(b, n_pairs, nq),
        in_specs=[pl.BlockSpec((1, tq, 2 * HEAD_PAD), lambda bi, hp, st, pt: (bi, _tile_of_step(st, nq), hp)),
                  pl.BlockSpec((1, s, 2 * HEAD_PAD), lambda bi, hp, qi, pt: (bi, 0, hp)),
                  pl.BlockSpec((1, 2 * V_DIM, s), lambda bi, hp, qi, pt: (bi, hp, 0)),
                  pl.BlockSpec((1, tq, 2 * V_DIM), lambda bi, hp, st, pt: (bi, _tile_of_step(st, nq), hp)),
                  per_h(KV_LORA), per_h(ROPE_DIM), per_r(KV_LORA), per_r(ROPE_DIM),
                  pl.BlockSpec(wv.shape, lambda bi, hp, qi, pt: (0, 0)),
                  per_r(D_ATT),
                  pl.BlockSpec(memory_space=pl.ANY),
                  pl.BlockSpec(memory_space=pl.ANY)],
        out_specs=[pl.BlockSpec((1, tq, 2 * V_DIM), lambda bi, hp, st, pt: (bi, _tile_of_step(st, nq), hp)),
                   per_r(D_ATT)],
        scratch_shapes=[pltpu.VMEM((tk, tq), F32)] * 4 + [
            pltpu.VMEM((2, past, KV_LORA), F32),
            pltpu.VMEM((2, ROPE_DIM, past), F32),
            pltpu.SemaphoreType.DMA((2, 2))],
    )
    return pl.pallas_call(
        functools.partial(_attn_kernel, tq=tq, tk=tk, n_pages=n_pages),
        grid_spec=grid_spec,
        out_shape=[jax.ShapeDtypeStruct((b, s, D_ATT), BF16), jax.ShapeDtypeStruct((n, 1, D_ATT), BF16)],
        compiler_params=pltpu.CompilerParams(dimension_semantics=("arbitrary", "arbitrary", "arbitrary"),
                                             vmem_limit_bytes=VMEM_LIMIT),
        name="attention",
    )(page_table.reshape(-1), q, k, vt, sg_p, qabs, qrope, ckv_new, kr_new, wv, sg_s, cache_ckv, cache_krt)


def _combine_kernel(x_ref, gate_ref, ag_ref, mp_ref, wo_ref, fnw_ref, y_ref):
    tm = x_ref.shape[1]
    hr = tm // ROW_PARTS
    parts = [slice(p * hr, (p + 1) * hr) for p in range(ROW_PARTS)]
    projs = [jnp.dot(ag_ref[0, rows, :], wo_ref[0:D_ATT, :], preferred_element_type=F32)
             + jnp.dot(mp_ref[0, rows, :], wo_ref[D_ATT:D_ATT + D_POOL, :], preferred_element_type=F32)
             for rows in parts]
    for rows, proj in zip(parts, projs):
        gate = gate_ref[0] if gate_ref.shape[1] == 1 else gate_ref[0, rows, :]
        y_ref[0, rows, :] = _rms(x_ref[0, rows, :] + gate * proj, fnw_ref[...])


def _combine(x, gate, ag, mp, wo, fnw, tm):
    b, s, d = x.shape
    gr = gate.shape[1]
    gate_spec = (pl.BlockSpec((1, 1, d), lambda bi, i: (bi, 0, 0)) if gr == 1
                 else pl.BlockSpec((1, tm, d), lambda bi, i: (bi, i, 0)))
    row = lambda w: pl.BlockSpec((1, tm, w), lambda bi, i: (bi, i, 0))
    return pl.pallas_call(
        _combine_kernel,
        grid=(b, s // tm),
        in_specs=[row(d), gate_spec, row(D_ATT), row(D_POOL), _full(wo.shape), _full(fnw.shape)],
        out_specs=row(d),
        out_shape=jax.ShapeDtypeStruct((b, s, d), F32),
        compiler_params=pltpu.CompilerParams(dimension_semantics=("arbitrary", "arbitrary"),
                                             vmem_limit_bytes=VMEM_LIMIT),
        name="combine",
    )(x, gate, ag, mp, wo, fnw)


def _rope_tables(offsets, bases):
    inv = ROPE_BASE ** (-jnp.arange(0, ROPE_DIM, 2, dtype=F32) / ROPE_DIM)
    inv = jnp.pad(jnp.concatenate([inv, inv]), (0, HEAD_PAD - ROPE_DIM))
    lanes = (jnp.arange(HEAD_PAD) < ROPE_DIM).astype(F32)
    a_off = offsets.astype(F32)[:, None] * inv[None, :]
    a_base = bases.astype(F32)[:, None, None] * inv[None, None, :]
    return jnp.cos(a_off) * lanes, jnp.sin(a_off) * lanes, jnp.cos(a_base), jnp.sin(a_base)


def _rot_half(w):
    half = w.shape[-1] // 2
    return jnp.concatenate([-w[..., half:], w[..., :half]], axis=-1)


def _pack_weights(norm_w, w_in, q_norm_w, w_uq, kv_norm_w, w_uk, w_uv, w_pool, b_pool, pool_scale):
    d = w_in.shape[0]
    off_kr = Q_LORA + KV_LORA
    wt = jnp.swapaxes(w_in, 0, 1).astype(BF16)
    krc = wt[off_kr:off_kr + ROPE_DIM]
    win = jnp.concatenate([wt[:off_kr], krc, -krc[ROPE_DIM // 2:], krc[:ROPE_DIM // 2],
                           jnp.zeros((HEAD_PAD - 2 * ROPE_DIM, d), BF16), wt[off_kr + ROPE_DIM:]], axis=0)
    pad_q = HEAD_PAD - QK_DIM
    wq = jnp.pad(w_uq, ((0, 0), (0, 0), (0, pad_q))).reshape(Q_LORA, N_HEADS * HEAD_PAD)
    wqr = jnp.pad(_rot_half(w_uq[..., NOPE_DIM:]), ((0, 0), (0, 0), (NOPE_DIM, pad_q)))
    wqr = wqr.reshape(Q_LORA, N_HEADS * HEAD_PAD)
    wk_nope = jnp.pad(w_uk, ((0, 0), (0, 0), (0, HEAD_PAD - NOPE_DIM))).reshape(KV_LORA, N_HEADS * HEAD_PAD)
    place = jnp.pad(jnp.eye(ROPE_DIM, dtype=F32), ((0, HEAD_PAD - ROPE_DIM), (NOPE_DIM, pad_q)))
    wk = jnp.concatenate([wk_nope, jnp.tile(place, (1, N_HEADS))], axis=0)
    wv = w_uv.reshape(KV_LORA, N_HEADS * V_DIM).astype(BF16)
    shared = (norm_w.reshape(1, -1), win, q_norm_w.reshape(1, -1), kv_norm_w.reshape(1, -1),
              wq.astype(BF16), wqr.astype(BF16), w_pool.astype(BF16), b_pool.reshape(1, -1),
              pool_scale.reshape(1, -1))
    return shared, wk.astype(BF16), wv


def kernel(x_prompt, x_sample, c_prompt, c_sample, cache_ckv, cache_krope, state_pool, page_table, ada_w, ada_b, norm_w, w_in, q_norm_w, w_uq, kv_norm_w, w_uk, w_uv, w_pool, b_pool, pool_scale, w_out, final_norm_w):
    assert ada_w.shape[0] == 1, "single-layer trunk only"
    b, s, d = x_prompt.shape
    n = x_sample.shape[0]
    assert x_sample.shape[1] == 1
    past_len = page_table.shape[1] * PAGE_SIZE

    c_all = jnp.concatenate([c_prompt, c_sample], axis=0)
    rows = -(-c_all.shape[0] // 8) * 8
    c_all = jnp.pad(c_all, ((0, rows - c_all.shape[0]), (0, 0)))
    mod = _ada(c_all, ada_w[0], ada_b[0].reshape(1, -1))
    mod_p = mod[:b].reshape(b, 3, d)
    mod_s = mod[b:b + n]

    wts, wk, wv = _pack_weights(norm_w[0], w_in[0], q_norm_w[0], w_uq[0], kv_norm_w[0], w_uk[0], w_uv[0],
                                w_pool[0], b_pool[0], pool_scale[0])
    wo = w_out[0].astype(BF16)
    fnw = final_norm_w.reshape(1, -1)

    tabs_p = _rope_tables(jnp.arange(ROW_TILE), jnp.arange(0, s, ROW_TILE))
    q, k, vt, ckv_p, krt_p, sg_p, mp_p, utail = _inproj_p(x_prompt, mod_p, wts, wk, wv.T, tabs_p, tm=ROW_TILE)

    tabs_s = _rope_tables(jnp.zeros((n,)), jnp.full((1,), past_len))
    wukt = jnp.pad(jnp.transpose(w_uk[0], (1, 2, 0)), ((0, 0), (0, HEAD_PAD - NOPE_DIM), (0, 0))).astype(BF16)
    state_t = jnp.transpose(state_pool[0], (1, 0, 2))
    qabs, qrope, ckv_s, kr_s, sg_s, mp_s, u_s = _inproj_s(x_sample[:, 0, :], mod_s, wts, tabs_s, state_t, wukt)
    qrope = qrope.reshape(n, N_HEADS, HEAD_PAD)[:, :, NOPE_DIM:QK_DIM]
    ag_p, ag_s = _attention(q, k, vt, sg_p, page_table, qabs.reshape(n, N_HEADS, KV_LORA), qrope,
                            ckv_s[:, None, :], kr_s[:, None, :], wv, sg_s[:, None, :],
                            cache_ckv, jnp.swapaxes(cache_krope, 2, 3))
    y_prompt = _combine(x_prompt, mod_p[:, 2:3, :], ag_p, mp_p, wo, fnw, tm=2 * ROW_TILE)
    y_sample = _combine(x_sample.reshape(1, n, d), mod_s[None, :, 2 * d:], ag_s.reshape(1, n, D_ATT),
                        mp_s[None], wo, fnw, tm=n).reshape(n, 1, d)

    new_pool_p = utail[:, HALO - POOL_PREV:, :]
    new_pool_s = jnp.transpose(jnp.concatenate([state_t[1:], u_s[None]], axis=0), (1, 0, 2))
    return (y_prompt, y_sample,
            ckv_p[None], jnp.swapaxes(krt_p, 1, 2)[None], new_pool_p[None],
            ckv_s[:, None, :][None], kr_s[:, None, :][None], new_pool_s[None])
```

```python
import functools
import math

import jax
import jax.numpy as jnp
from jax import lax
from jax.experimental import pallas as pl
from jax.experimental.pallas import tpu as pltpu

F32 = jnp.float32
BF16 = jnp.bfloat16

N_HEADS = 8
NOPE_DIM = 64
ROPE_DIM = 32
QK_DIM = NOPE_DIM + ROPE_DIM
V_DIM = 64
Q_LORA = 256
KV_LORA = 128
D_POOL = 512
D_ATT = 512
POOL_WINDOWS = (2, 4, 8, 16)
POOL_GROUP = 128
POOL_PREV = 15
PAGE_SIZE = 128
ROPE_BASE = 10000.0
EPS = 1e-6
SM_SCALE = QK_DIM ** -0.5
LOG2E = math.log2(math.e)
QK_PRESCALE = SM_SCALE * LOG2E
_NT = (((1,), (1,)), ((), ()))

HEAD_PAD = 128
SUM_ROWS = 16
HALO = 16
C_Q, C_KV, C_KR, C_GA, C_U, C_GP, C_END = 0, 256, 384, 512, 1024, 1536, 2048

ROW_TILE = 512
ROW_PARTS = 2
COL_SPLIT = 1
FIRST_CHUNKS = 2
DEC_CHUNKS = 4
LOOP_BLOCKS = 4
LAST_DEAL = ((2, 2), (2, 2, 1, 1), (2, 2, 0, 0, 1, 1), (2, 2, 0, 0, 0, 0, 1, 1))

VMEM_LIMIT = 56 * 1024 * 1024


def _rms(x, w):
    return (x * lax.rsqrt(jnp.mean(x * x, axis=-1, keepdims=True) + EPS)) * w


def _silu(x):
    return x * jax.nn.sigmoid(x)


def _bdot(a, b):
    return jnp.dot(a.astype(BF16), b, preferred_element_type=F32)


def _ada_kernel(c_ref, w_ref, b_ref, o_ref):
    o_ref[...] = _bdot(_silu(c_ref[...]), w_ref[...].astype(BF16)) + b_ref[...]


def _ada(c_all, ada_w, ada_b):
    rows, d = c_all.shape
    n = ada_w.shape[1]
    bn = 1024
    return pl.pallas_call(
        _ada_kernel,
        grid=(n // bn,),
        in_specs=[pl.BlockSpec((rows, d), lambda j: (0, 0)),
                  pl.BlockSpec((d, bn), lambda j: (0, j)),
                  pl.BlockSpec((1, bn), lambda j: (0, j))],
        out_specs=pl.BlockSpec((rows, bn), lambda j: (0, j)),
        out_shape=jax.ShapeDtypeStruct((rows, n), F32),
        compiler_params=pltpu.CompilerParams(dimension_semantics=("arbitrary",)),
        name="ada",
    )(c_all, ada_w, ada_b)


def _project_z(x, shift, scale, nw, win):
    h = _rms(x, nw) * (1.0 + scale) + shift
    return lax.dot_general(h.astype(BF16), win, _NT, preferred_element_type=F32)


def _project_rest(z, qnw, kvnw, wq, wqr, cr, sr, cb, sb):
    cosk = cb * cr - sb * sr
    sink = sb * cr + cb * sr
    nope = (lax.broadcasted_iota(jnp.int32, (1, HEAD_PAD), 1) < NOPE_DIM).astype(F32)
    cosq = pltpu.roll(cosk, NOPE_DIM, axis=1) + nope
    sinq = pltpu.roll(sink, NOPE_DIM, axis=1)
    q_lat = _rms(z[:, C_Q:C_KV], qnw)
    ckv = _rms(z[:, C_KV:C_KR], kvnw)
    krc = z[:, C_KR:C_GA]
    kr128 = krc * cosk + pltpu.roll(krc, HEAD_PAD - ROPE_DIM, axis=1) * sink
    ql = q_lat.astype(BF16)
    qa = jnp.dot(ql, wq, preferred_element_type=F32)
    qb = jnp.dot(ql, wqr, preferred_element_type=F32)
    return qa, qb, ckv, kr128, z[:, C_GA:C_U], z[:, C_U:C_GP], z[:, C_GP:C_END], cosq, sinq


def _pool_out(pooled_g, g, wpool_ref, bpool_ref, pscale_ref, g_pool):
    lo, hi = g * POOL_GROUP, (g + 1) * POOL_GROUP
    y = _bdot(pooled_g, wpool_ref[g]) + bpool_ref[:, lo:hi]
    y = y * pscale_ref[:, lo:hi]
    return y * _silu(g_pool[:, lo:hi])


def _inproj_p_kernel(x_ref, mod_ref, nw_ref, win_ref, qnw_ref, kvnw_ref, wq_ref, wqr_ref,
                     wpool_ref, bpool_ref, pscale_ref, wk_ref, wvt_ref, cr_ref, sr_ref, cb_ref, sb_ref,
                     q_ref, k_ref, vt_ref, ckv_ref, krt_ref, sg_ref, mp_ref, utail_ref, uext_ref, *, tm):
    i = pl.program_id(1)
    shift = mod_ref[0, 0:1, :]
    scale = mod_ref[0, 1:2, :]

    @pl.when(i == 0)
    def _():
        uext_ref[0:HALO, :] = jnp.zeros((HALO, D_POOL), F32)

    hr = tm // ROW_PARTS
    zs = []
    for part in range(ROW_PARTS):
        rows = slice(part * hr, (part + 1) * hr)
        z = _project_z(x_ref[0, rows, :], shift, scale, nw_ref[...], win_ref[...])
        uext_ref[HALO + part * hr:HALO + (part + 1) * hr, :] = z[:, C_U:C_GP]
        zs.append(z)

    for part in range(ROW_PARTS):
        r0 = part * hr
        rows = slice(r0, r0 + hr)
        qa, qb, ckv, kr128, g_att, u, g_pool, cosq, sinq = _project_rest(
            zs[part], qnw_ref[...], kvnw_ref[...], wq_ref[...], wqr_ref[...],
            cr_ref[rows, :], sr_ref[rows, :], cb_ref[0], sb_ref[0])
        cosq, sinq = cosq * QK_PRESCALE, sinq * QK_PRESCALE
        for h in range(N_HEADS):
            sl = slice(h * HEAD_PAD, (h + 1) * HEAD_PAD)
            q_ref[0, rows, sl] = (qa[:, sl] * cosq + qb[:, sl] * sinq).astype(BF16)
        ckv_b = ckv.astype(BF16)
        k = jnp.dot(jnp.concatenate([ckv_b, kr128.astype(BF16)], axis=1), wk_ref[...],
                    preferred_element_type=F32)
        k_ref[0, rows, :] = k.astype(BF16)
        vt_ref[0, :, rows] = lax.dot_general(wvt_ref[...], ckv_b, _NT, preferred_element_type=F32).astype(BF16)
        ckv_ref[0, rows, :] = ckv
        krt_ref[0, :, rows] = kr128.T[:ROPE_DIM, :]
        sg_ref[0, rows, :] = _silu(g_att)
        pos = (i * tm + r0 + lax.broadcasted_iota(jnp.int32, (hr, 1), 0)).astype(F32)
        for g, w in enumerate(POOL_WINDOWS):
            lo, hi = g * POOL_GROUP, (g + 1) * POOL_GROUP
            wsum = u[:, lo:hi]
            for d in range(1, w):
                wsum = wsum + uext_ref[HALO + r0 - d:HALO + r0 - d + hr, lo:hi]
            cnt = jnp.minimum(pos + 1.0, float(w))
            pooled = wsum / cnt - u[:, lo:hi]
            mp_ref[0, rows, lo:hi] = _pool_out(pooled, g, wpool_ref, bpool_ref, pscale_ref, g_pool).astype(BF16)

    u_last = zs[-1][hr - HALO:hr, C_U:C_GP]
    uext_ref[0:HALO, :] = u_last

    @pl.when(i == pl.num_programs(1) - 1)
    def _():
        utail_ref[0] = u_last


def _full(shape):
    nd = len(shape)
    return pl.BlockSpec(shape, lambda *_: (0,) * nd)


def _inproj_p(x, mod3, wts, wk, wvt, tabs, tm=512):
    b, s, d = x.shape
    nt = s // tm
    row = lambda w: pl.BlockSpec((1, tm, w), lambda bi, i: (bi, i, 0))
    rtab = _full((tm, HEAD_PAD))
    btab = pl.BlockSpec((1, 1, HEAD_PAD), lambda bi, i: (i, 0, 0))
    in_specs = [row(d), pl.BlockSpec((1, 3, d), lambda bi, i: (bi, 0, 0))]
    in_specs += [_full(w.shape) for w in (*wts, wk, wvt)]
    in_specs += [rtab, rtab, btab, btab]
    out_shape = [
        jax.ShapeDtypeStruct((b, s, N_HEADS * HEAD_PAD), BF16),
        jax.ShapeDtypeStruct((b, s, N_HEADS * HEAD_PAD), BF16),
        jax.ShapeDtypeStruct((b, D_ATT, s), BF16),
        jax.ShapeDtypeStruct((b, s, KV_LORA), F32),
        jax.ShapeDtypeStruct((b, ROPE_DIM, s), F32),
        jax.ShapeDtypeStruct((b, s, D_ATT), F32),
        jax.ShapeDtypeStruct((b, s, D_POOL), BF16),
        jax.ShapeDtypeStruct((b, HALO, D_POOL), F32),
    ]
    out_specs = [row(N_HEADS * HEAD_PAD), row(N_HEADS * HEAD_PAD),
                 pl.BlockSpec((1, D_ATT, tm), lambda bi, i: (bi, 0, i)), row(KV_LORA),
                 pl.BlockSpec((1, ROPE_DIM, tm), lambda bi, i: (bi, 0, i)),
                 row(D_ATT), row(D_POOL), pl.BlockSpec((1, HALO, D_POOL), lambda bi, i: (bi, 0, 0))]
    return pl.pallas_call(
        functools.partial(_inproj_p_kernel, tm=tm),
        grid=(b, nt),
        in_specs=in_specs,
        out_specs=out_specs,
        out_shape=out_shape,
        scratch_shapes=[pltpu.VMEM((tm + HALO, D_POOL), F32)],
        compiler_params=pltpu.CompilerParams(dimension_semantics=("arbitrary", "arbitrary"),
                                             vmem_limit_bytes=VMEM_LIMIT),
        name="inproj_p",
    )(x, mod3, *wts, wk, wvt, *tabs)


def _inproj_s_kernel(x_ref, mod_ref, nw_ref, win_ref, qnw_ref, kvnw_ref, wq_ref, wqr_ref,
                     wpool_ref, bpool_ref, pscale_ref, cr_ref, sr_ref, cb_ref, sb_ref,
                     state_ref, wukt_ref,
                     qabs_ref, qrope_ref, ckv_ref, kr_ref, sg_ref, mp_ref, u_ref):
    d = x_ref.shape[1]
    shift = mod_ref[:, 0:d]
    scale = mod_ref[:, d:2 * d]
    z = _project_z(x_ref[...], shift, scale, nw_ref[...], win_ref[...])
    qa, qb, ckv, kr128, g_att, u, g_pool, cosq, sinq = _project_rest(
        z, qnw_ref[...], kvnw_ref[...], wq_ref[...], wqr_ref[...], cr_ref[...], sr_ref[...], cb_ref[0], sb_ref[0])
    for h in range(N_HEADS):
        sl = slice(h * HEAD_PAD, (h + 1) * HEAD_PAD)
        qrope_ref[:, sl] = qa[:, sl] * cosq + qb[:, sl] * sinq
        qabs_ref[:, sl] = _bdot(qa[:, sl], wukt_ref[h])
    ckv_ref[...] = ckv
    kr_ref[...] = kr128[:, :ROPE_DIM]
    sg_ref[...] = _silu(g_att)
    u_ref[...] = u
    for g, w in enumerate(POOL_WINDOWS):
        lo, hi = g * POOL_GROUP, (g + 1) * POOL_GROUP
        wsum = u[:, lo:hi]
        for j in range(1, w):
            wsum = wsum + state_ref[POOL_PREV - j, :, lo:hi]
        pooled = wsum / float(w) - u[:, lo:hi]
        mp_ref[:, lo:hi] = _pool_out(pooled, g, wpool_ref, bpool_ref, pscale_ref, g_pool).astype(BF16)


def _inproj_s(x, mod, wts, tabs, state, wukt):
    n, d = x.shape
    args = (x, mod, *wts, *tabs, state, wukt)
    out_shape = [
        jax.ShapeDtypeStruct((n, N_HEADS * HEAD_PAD), F32),
        jax.ShapeDtypeStruct((n, N_HEADS * HEAD_PAD), F32),
        jax.ShapeDtypeStruct((n, KV_LORA), F32),
        jax.ShapeDtypeStruct((n, ROPE_DIM), F32),
        jax.ShapeDtypeStruct((n, D_ATT), F32),
        jax.ShapeDtypeStruct((n, D_POOL), BF16),
        jax.ShapeDtypeStruct((n, D_POOL), F32),
    ]
    return pl.pallas_call(
        _inproj_s_kernel,
        in_specs=[_full(a.shape) for a in args],
        out_specs=[_full(o.shape) for o in out_shape],
        out_shape=out_shape,
        grid=(1,),
        compiler_params=pltpu.CompilerParams(dimension_semantics=("arbitrary",), vmem_limit_bytes=VMEM_LIMIT),
        name="inproj_s",
    )(*args)


def _flash_tile(qi, q_ref, k_ref, vt_ref, sg_ref, o_ref, st_ref, between, side, *, tq, tk):
    assert tq == tk
    cw = tq // COL_SPLIT
    units = [(hh, cs) for hh in range(2) for cs in range(COL_SPLIT)]
    qs = [q_ref[0, cs * cw:(cs + 1) * cw, hh * HEAD_PAD:(hh + 1) * HEAD_PAD] for hh, cs in units]

    def produce(j, slot, u, rows=None):
        hh, cs = units[u]
        r0, r1 = (0, tk) if rows is None else rows
        kb = k_ref[0, pl.ds(pl.multiple_of(j * tk, tk) + r0, r1 - r0), hh * HEAD_PAD:(hh + 1) * HEAD_PAD]
        st_ref[slot][hh][r0:r1, cs * cw:(cs + 1) * cw] = lax.dot_general(kb, qs[u], _NT,
                                                                         preferred_element_type=F32)

    def consume(j, slot, u, state, diagonal=False):
        hh, cs = units[u]
        m, acc = state
        st = st_ref[slot][hh][:, cs * cw:(cs + 1) * cw]
        if diagonal:
            kpos = lax.broadcasted_iota(jnp.int32, (tk, cw), 0)
            qpos = cs * cw + lax.broadcasted_iota(jnp.int32, (tk, cw), 1)
            st = jnp.where(kpos <= qpos, st, -jnp.inf)
        vtb = vt_ref[0, hh * V_DIM:(hh + 1) * V_DIM, pl.ds(pl.multiple_of(j * tk, tk), tk)]
        vtb = jnp.concatenate([vtb, jnp.ones((SUM_ROWS, tk), BF16)], axis=0)
        m_new = jnp.maximum(m, jnp.max(st, axis=0, keepdims=True))
        alpha = jnp.exp2(m - m_new)
        pt = jnp.exp2(st - m_new).astype(BF16)
        return m_new, alpha * acc + jnp.dot(vtb, pt, preferred_element_type=F32)

    def finish(states):
        heads = []
        for hh in range(2):
            acc = jnp.concatenate([states[u][1] for u in range(len(units)) if units[u][0] == hh], axis=1)
            heads.append(acc[:V_DIM] / acc[V_DIM:V_DIM + 1])
        o_ref[0] = (jnp.concatenate(heads, axis=0).T * sg_ref[0]).astype(BF16)

    def trip(t, states):
        j = LOOP_BLOCKS * t
        cur = list(states)
        for i in range(LOOP_BLOCKS):
            for u in range(len(units)):
                produce(j + i + 1, (i + 1) % 2, u)
                cur[u] = consume(j + i, i % 2, u, cur[u])
        return tuple(cur)

    init = tuple((jnp.full((1, cw), -jnp.inf, F32), jnp.zeros((V_DIM + SUM_ROWS, cw), F32)) for _ in units)
    pieces = FIRST_CHUNKS // len(units)
    for u in range(len(units)):
        for c in range(pieces):
            produce(0, 0, u, (c * tk // pieces, (c + 1) * tk // pieces))
            between(u * pieces + c)
    states = lax.fori_loop(0, qi // LOOP_BLOCKS, trip, init)

    def run_interleaved(steps, side_pieces, counts):
        side_pieces = list(side_pieces)
        for step, count in zip(steps, counts):
            for piece in side_pieces[:count]:
                piece()
            del side_pieces[:count]
            step()
        for piece in side_pieces:
            piece()

    for rest in range(LOOP_BLOCKS):
        @pl.when(qi % LOOP_BLOCKS == rest)
        def _():
            wait, side_pieces = side(rest % 2)
            wait()
            j0 = qi - rest
            cur = list(states)
            steps = []
            for i in range(rest):
                for u in range(len(units)):
                    def run(i=i, u=u):
                        produce(j0 + i + 1, (i + 1) % 2, u)
                        cur[u] = consume(j0 + i, i % 2, u, cur[u])
                    steps.append(run)
            for u in range(len(units)):
                def run(u=u):
                    cur[u] = consume(qi, rest % 2, u, cur[u], diagonal=True)
                steps.append(run)
            run_interleaved(steps, side_pieces, LAST_DEAL[rest])
            finish(cur)


def _page_copies(pt_ref, cckv_hbm, ckrt_hbm, ckv_buf, krt_buf, sem, n_pages, row, sl, p):
    page = pt_ref[row * n_pages + p]
    cols = pl.ds(p * PAGE_SIZE if isinstance(p, int) else pl.multiple_of(p * PAGE_SIZE, PAGE_SIZE), PAGE_SIZE)
    return (pltpu.make_async_copy(cckv_hbm.at[0, page], ckv_buf.at[sl, cols], sem.at[0, sl]),
            pltpu.make_async_copy(ckrt_hbm.at[0, page], krt_buf.at[sl, :, cols], sem.at[1, sl]))


def _decode_stages(pt_ref, qa_ref, qr_ref, ckvn_ref, krn_ref, wv_ref, sg_ref, cckv_hbm, ckrt_hbm,
                   o_ref, ckv_buf, krt_buf, sem, *, row, slot, n_pages):
    def wait():
        def wait_body(p, _):
            for cp in _page_copies(pt_ref, cckv_hbm, ckrt_hbm, ckv_buf, krt_buf, sem, n_pages, row, slot, p):
                cp.wait()
            return 0
        lax.fori_loop(0, n_pages, wait_body, 0, unroll=8)

    past = ckv_buf.shape[1]
    ck = past // DEC_CHUNKS
    env = {"ckv": [], "s": []}

    def scores(c):
        def run():
            if c == 0:
                env["qa"] = qa_ref[0].astype(BF16)
                env["qr"] = qr_ref[0].astype(BF16)
            ckv = ckv_buf[slot, c * ck:(c + 1) * ck, :].astype(BF16)
            krt = krt_buf[slot, :, c * ck:(c + 1) * ck].astype(BF16)
            env["ckv"].append(ckv)
            env["s"].append(lax.dot_general(env["qa"], ckv, _NT, preferred_element_type=F32)
                            + jnp.dot(env["qr"], krt, preferred_element_type=F32))
        return run

    def values(c):
        def run():
            if c == 0:
                qa, qr = env["qa"], env["qr"]
                s = jnp.concatenate(env["s"], axis=1)
                ckvn = ckvn_ref[0].astype(BF16).astype(F32)
                krn = krn_ref[0].astype(BF16).astype(F32)
                s_n = (jnp.sum(qa.astype(F32) * ckvn, axis=1, keepdims=True)
                       + jnp.sum(qr.astype(F32) * krn, axis=1, keepdims=True))
                m = jnp.maximum(jnp.max(s, axis=1, keepdims=True), s_n)
                p = jnp.exp2((s - m) * QK_PRESCALE)
                p_n = jnp.exp2((s_n - m) * QK_PRESCALE)
                env["l"] = jnp.sum(p, axis=1, keepdims=True) + p_n
                env["p"] = p.astype(BF16)
                env["o"] = p_n.astype(BF16).astype(F32) * ckvn
            env["o"] = env["o"] + jnp.dot(env["p"][:, c * ck:(c + 1) * ck], env["ckv"][c],
                                          preferred_element_type=F32)
            if c == DEC_CHUNKS - 1:
                o_all = _bdot(env["o"] / env["l"], wv_ref[...])
                hrow = lax.broadcasted_iota(jnp.int32, o_all.shape, 0)
                hcol = lax.broadcasted_iota(jnp.int32, o_all.shape, 1) // V_DIM
                att = jnp.sum(jnp.where(hrow == hcol, o_all, 0.0), axis=0, keepdims=True)
                o_ref[0] = (att * sg_ref[0]).astype(BF16)
        return run

    return wait, [scores(c) for c in range(DEC_CHUNKS)] + [values(c) for c in range(DEC_CHUNKS)]


def _tile_of_step(step, n_tiles):
    return jnp.where(step % 2 == 0, step, n_tiles - step)


def _attn_kernel(pt_ref, q_ref, k_ref, vt_ref, sgp_ref, qa_ref, qr_ref, ckvn_ref, krn_ref, wv_ref, sgs_ref,
                 cckv_hbm, ckrt_hbm, op_ref, os_ref, st00, st01, st10, st11, ckv_buf, krt_buf, sem,
                 *, tq, tk, n_pages):
    qi = _tile_of_step(pl.program_id(2), pl.num_programs(2))
    r = (pl.program_id(0) * pl.num_programs(1) + pl.program_id(1)) * pl.num_programs(2) + pl.program_id(2)
    n_rows = pl.num_programs(0) * pl.num_programs(1) * pl.num_programs(2)
    slot = r % 2
    dma = (pt_ref, cckv_hbm, ckrt_hbm, ckv_buf, krt_buf, sem, n_pages)

    @pl.when(r == 0)
    def _():
        def body(p, _):
            for cp in _page_copies(*dma, 0, 0, p):
                cp.start()
            return 0
        lax.fori_loop(0, n_pages, body, 0, unroll=8)

    nxt = jnp.minimum(r + 1, n_rows - 1)
    assert n_pages % FIRST_CHUNKS == 0
    per_chunk = n_pages // FIRST_CHUNKS

    def issue(c):
        for p in range(c * per_chunk, (c + 1) * per_chunk):
            for cp in _page_copies(*dma, nxt, 1 - slot, p):
                cp.start(priority=p % 2)

    def row_stages(parity):
        return _decode_stages(pt_ref, qa_ref, qr_ref, ckvn_ref, krn_ref, wv_ref, sgs_ref, cckv_hbm, ckrt_hbm,
                              os_ref, ckv_buf, krt_buf, sem, row=r, slot=parity, n_pages=n_pages)

    _flash_tile(qi, q_ref, k_ref, vt_ref, sgp_ref, op_ref, ((st00, st01), (st10, st11)), issue, row_stages,
                tq=tq, tk=tk)

    @pl.when(r == n_rows - 1)
    def _():
        def body(p, _):
            for cp in _page_copies(*dma, nxt, 1 - slot, p):
                cp.wait()
            return 0
        lax.fori_loop(0, n_pages, body, 0, unroll=8)


def _attention(q, k, vt, sg_p, page_table, qabs, qrope, ckv_new, kr_new, wv, sg_s, cache_ckv, cache_krt,
               tq=512, tk=512):
    b, s, _ = q.shape
    n_pairs = N_HEADS // 2
    nq = s // tq
    n, n_pages = page_table.shape
    assert n == b * n_pairs * nq, "one sample row per prompt attention step"
    assert nq % 2 == 0, "row parity must equal q-tile parity"
    past = n_pages * PAGE_SIZE
    row = lambda bi, hp, qi: (bi * n_pairs + hp) * nq + qi
    per_r = lambda w: pl.BlockSpec((1, 1, w), lambda bi, hp, qi, pt: (row(bi, hp, qi), 0, 0))
    per_h = lambda w: pl.BlockSpec((1, N_HEADS, w), lambda bi, hp, qi, pt: (row(bi, hp, qi), 0, 0))
    grid_spec = pltpu.PrefetchScalarGridSpec(
        num_scalar_prefetch=1,
        grid=(b, n_pairs, nq),
        in_specs=[pl.BlockSpec((1, tq, 2 * HEAD_PAD), lambda bi, hp, st, pt: (bi, _tile_of_step(st, nq), hp)),
                  pl.BlockSpec((1, s, 2 * HEAD_PAD), lambda bi, hp, qi, pt: (bi, 0, hp)),
                  pl.BlockSpec((1, 2 * V_DIM, s), lambda bi, hp, qi, pt: (bi, hp, 0)),
                  pl.BlockSpec((1, tq, 2 * V_DIM), lambda bi, hp, st, pt: (bi, _tile_of_step(st, nq), hp)),
                  per_h(KV_LORA), per_h(ROPE_DIM), per_r(KV_LORA), per_r(ROPE_DIM),
                  pl.BlockSpec(wv.shape, lambda bi, hp, qi, pt: (0, 0)),
                  per_r(D_ATT),
                  pl.BlockSpec(memory_space=pl.ANY),
                  pl.BlockSpec(memory_space=pl.ANY)],
        out_specs=[pl.BlockSpec((1, tq, 2 * V_DIM), lambda bi, hp, st, pt: (bi, _tile_of_step(st, nq), hp)),
                   per_r(D_ATT)],
        scratch_shapes=[pltpu.VMEM((tk, tq), F32)] * 4 + [
            pltpu.VMEM((2, past, KV_LORA), F32),
            pltpu.VMEM((2, ROPE_DIM, past), F32),
            pltpu.SemaphoreType.DMA((2, 2))],
    )
    return pl.pallas_call(
        functools.partial(_attn_kernel, tq=tq, tk=tk, n_pages=n_pages),
        grid_spec=grid_spec,
        out_shape=[jax.ShapeDtypeStruct((b, s, D_ATT), BF16), jax.ShapeDtypeStruct((n, 1, D_ATT), BF16)],
        compiler_params=pltpu.CompilerParams(dimension_semantics=("arbitrary", "arbitrary", "arbitrary"),
                                             vmem_limit_bytes=VMEM_LIMIT),
        name="attention",
    )(page_table.reshape(-1), q, k, vt, sg_p, qabs, qrope, ckv_new, kr_new, wv, sg_s, cache_ckv, cache_krt)


def _combine_kernel(x_ref, gate_ref, ag_ref, mp_ref, wo_ref, fnw_ref, y_ref):
    tm = x_ref.shape[1]
    hr = tm // ROW_PARTS
    parts = [slice(p * hr, (p + 1) * hr) for p in range(ROW_PARTS)]
    projs = [jnp.dot(ag_ref[0, rows, :], wo_ref[0:D_ATT, :], preferred_element_type=F32)
             + jnp.dot(mp_ref[0, rows, :], wo_ref[D_ATT:D_ATT + D_POOL, :], preferred_element_type=F32)
             for rows in parts]
    for rows, proj in zip(parts, projs):
        gate = gate_ref[0] if gate_ref.shape[1] == 1 else gate_ref[0, rows, :]
        y_ref[0, rows, :] = _rms(x_ref[0, rows, :] + gate * proj, fnw_ref[...])


def _combine(x, gate, ag, mp, wo, fnw, tm):
    b, s, d = x.shape
    gr = gate.shape[1]
    gate_spec = (pl.BlockSpec((1, 1, d), lambda bi, i: (bi, 0, 0)) if gr == 1
                 else pl.BlockSpec((1, tm, d), lambda bi, i: (bi, i, 0)))
    row = lambda w: pl.BlockSpec((1, tm, w), lambda bi, i: (bi, i, 0))
    return pl.pallas_call(
        _combine_kernel,
        grid=(b, s // tm),
        in_specs=[row(d), gate_spec, row(D_ATT), row(D_POOL), _full(wo.shape), _full(fnw.shape)],
        out_specs=row(d),
        out_shape=jax.ShapeDtypeStruct((b, s, d), F32),
        compiler_params=pltpu.CompilerParams(dimension_semantics=("arbitrary", "arbitrary"),
                                             vmem_limit_bytes=VMEM_LIMIT),
        name="combine",
    )(x, gate, ag, mp, wo, fnw)


def _rope_tables(offsets, bases):
    inv = ROPE_BASE ** (-jnp.arange(0, ROPE_DIM, 2, dtype=F32) / ROPE_DIM)
    inv = jnp.pad(jnp.concatenate([inv, inv]), (0, HEAD_PAD - ROPE_DIM))
    lanes = (jnp.arange(HEAD_PAD) < ROPE_DIM).astype(F32)
    a_off = offsets.astype(F32)[:, None] * inv[None, :]
    a_base = bases.astype(F32)[:, None, None] * inv[None, None, :]
    return jnp.cos(a_off) * lanes, jnp.sin(a_off) * lanes, jnp.cos(a_base), jnp.sin(a_base)


def _rot_half(w):
    half = w.shape[-1] // 2
    return jnp.concatenate([-w[..., half:], w[..., :half]], axis=-1)


def _pack_weights(norm_w, w_in, q_norm_w, w_uq, kv_norm_w, w_uk, w_uv, w_pool, b_pool, pool_scale):
    d = w_in.shape[0]
    off_kr = Q_LORA + KV_LORA
    wt = jnp.swapaxes(w_in, 0, 1).astype(BF16)
    krc = wt[off_kr:off_kr + ROPE_DIM]
    win = jnp.concatenate([wt[:off_kr], krc, -krc[ROPE_DIM // 2:], krc[:ROPE_DIM // 2],
                           jnp.zeros((HEAD_PAD - 2 * ROPE_DIM, d), BF16), wt[off_kr + ROPE_DIM:]], axis=0)
    pad_q = HEAD_PAD - QK_DIM
    wq = jnp.pad(w_uq, ((0, 0), (0, 0), (0, pad_q))).reshape(Q_LORA, N_HEADS * HEAD_PAD)
    wqr = jnp.pad(_rot_half(w_uq[..., NOPE_DIM:]), ((0, 0), (0, 0), (NOPE_DIM, pad_q)))
    wqr = wqr.reshape(Q_LORA, N_HEADS * HEAD_PAD)
    wk_nope = jnp.pad(w_uk, ((0, 0), (0, 0), (0, HEAD_PAD - NOPE_DIM))).reshape(KV_LORA, N_HEADS * HEAD_PAD)
    place = jnp.pad(jnp.eye(ROPE_DIM, dtype=F32), ((0, HEAD_PAD - ROPE_DIM), (NOPE_DIM, pad_q)))
    wk = jnp.concatenate([wk_nope, jnp.tile(place, (1, N_HEADS))], axis=0)
    wv = w_uv.reshape(KV_LORA, N_HEADS * V_DIM).astype(BF16)
    shared = (norm_w.reshape(1, -1), win, q_norm_w.reshape(1, -1), kv_norm_w.reshape(1, -1),
              wq.astype(BF16), wqr.astype(BF16), w_pool.astype(BF16), b_pool.reshape(1, -1),
              pool_scale.reshape(1, -1))
    return shared, wk.astype(BF16), wv


def kernel(x_prompt, x_sample, c_prompt, c_sample, cache_ckv, cache_krope, state_pool, page_table, ada_w, ada_b, norm_w, w_in, q_norm_w, w_uq, kv_norm_w, w_uk, w_uv, w_pool, b_pool, pool_scale, w_out, final_norm_w):
    assert ada_w.shape[0] == 1, "single-layer trunk only"
    b, s, d = x_prompt.shape
    n = x_sample.shape[0]
    assert x_sample.shape[1] == 1
    past_len = page_table.shape[1] * PAGE_SIZE

    c_all = jnp.concatenate([c_prompt, c_sample], axis=0)
    rows = -(-c_all.shape[0] // 8) * 8
    c_all = jnp.pad(c_all, ((0, rows - c_all.shape[0]), (0, 0)))
    mod = _ada(c_all, ada_w[0], ada_b[0].reshape(1, -1))
    mod_p = mod[:b].reshape(b, 3, d)
    mod_s = mod[b:b + n]

    wts, wk, wv = _pack_weights(norm_w[0], w_in[0], q_norm_w[0], w_uq[0], kv_norm_w[0], w_uk[0], w_uv[0],
                                w_pool[0], b_pool[0], pool_scale[0])
    wo = w_out[0].astype(BF16)
    fnw = final_norm_w.reshape(1, -1)

    tabs_p = _rope_tables(jnp.arange(ROW_TILE), jnp.arange(0, s, ROW_TILE))
    q, k, vt, ckv_p, krt_p, sg_p, mp_p, utail = _inproj_p(x_prompt, mod_p, wts, wk, wv.T, tabs_p, tm=ROW_TILE)

    tabs_s = _rope_tables(jnp.zeros((n,)), jnp.full((1,), past_len))
    wukt = jnp.pad(jnp.transpose(w_uk[0], (1, 2, 0)), ((0, 0), (0, HEAD_PAD - NOPE_DIM), (0, 0))).astype(BF16)
    state_t = jnp.transpose(state_pool[0], (1, 0, 2))
    qabs, qrope, ckv_s, kr_s, sg_s, mp_s, u_s = _inproj_s(x_sample[:, 0, :], mod_s, wts, tabs_s, state_t, wukt)
    qrope = qrope.reshape(n, N_HEADS, HEAD_PAD)[:, :, NOPE_DIM:QK_DIM]
    ag_p, ag_s = _attention(q, k, vt, sg_p, page_table, qabs.reshape(n, N_HEADS, KV_LORA), qrope,
                            ckv_s[:, None, :], kr_s[:, None, :], wv, sg_s[:, None, :],
                            cache_ckv, jnp.swapaxes(cache_krope, 2, 3))
    y_prompt = _combine(x_prompt, mod_p[:, 2:3, :], ag_p, mp_p, wo, fnw, tm=2 * ROW_TILE)
    y_sample = _combine(x_sample.reshape(1, n, d), mod_s[None, :, 2 * d:], ag_s.reshape(1, n, D_ATT),
                        mp_s[None], wo, fnw, tm=n).reshape(n, 1, d)

    new_pool_p = utail[:, HALO - POOL_PREV:, :]
    new_pool_s = jnp.transpose(jnp.concatenate([state_t[1:], u_s[None]], axis=0), (1, 0, 2))
    return (y_prompt, y_sample,
            ckv_p[None], jnp.swapaxes(krt_p, 1, 2)[None], new_pool_p[None],
            ckv_s[:, None, :][None], kr_s[:, None, :][None], new_pool_s[None])
```
